```python
import math
import jax
import jax.numpy as jnp
from jax import lax
import numpy as np

D_MODEL = 1024
BATCH = 8
SEQ = 4096
DEPTH = 2

N_A_LAYERS = DEPTH // 2
N_B_LAYERS = DEPTH - N_A_LAYERS
N_DENSE_FFN = (DEPTH + 1) // 2
N_MOE_FFN = DEPTH // 2

D_RNN = D_MODEL
RNN_BLOCKS = 8
RNN_BLOCK_W = D_RNN // RNN_BLOCKS
CONV_W = 4
LRU_C = 8.0

MEM_TOKENS = 256
MEM_HEADS = 4
MEM_HEAD_DIM = D_MODEL // 8
MEM_W = MEM_HEADS * MEM_HEAD_DIM

DIFF_HEADS = 8
DIFF_HEAD_DIM = D_MODEL // 16
DIFF_V_DIM = 2 * DIFF_HEAD_DIM
DIFF_QK_W = DIFF_HEADS * 2 * DIFF_HEAD_DIM
DIFF_V_W = DIFF_HEADS * DIFF_V_DIM
Q_BLOCK = 128

FFN_DIM = (7 * D_MODEL) // 2
N_EXPERTS = 8
TOP_K = 2
EXPERT_DIM = FFN_DIM
MOE_BLOCK = 256

LN_EPS = 1e-5
DEEPNORM_ALPHA = (2.0 * DEPTH) ** 0.25
DEEPNORM_BETA = (8.0 * DEPTH) ** -0.25

kernel_name = "yoco_rglru_diffattn_moe_deepnorm"


def layer_norm(x, g, b):
    xf = x.astype(jnp.float32)
    mu = jnp.mean(xf, -1, keepdims=True)
    var = jnp.mean(jnp.square(xf - mu), -1, keepdims=True)
    return ((xf - mu) * lax.rsqrt(var + LN_EPS) * g + b).astype(x.dtype)


def rms_norm(x, g):
    xf = x.astype(jnp.float32)
    return (xf * lax.rsqrt(jnp.mean(jnp.square(xf), -1, keepdims=True) + LN_EPS) * g).astype(x.dtype)


def memory_attention(q, mem, w_mem_kv):
    B, T, _ = q.shape
    kv = mem @ w_mem_kv
    k = kv[..., :MEM_W].reshape(B, -1, MEM_HEADS, MEM_HEAD_DIM)
    v = kv[..., MEM_W:].reshape(B, -1, MEM_HEADS, MEM_HEAD_DIM)
    qh = q.reshape(B, T, MEM_HEADS, MEM_HEAD_DIM)
    s = jnp.einsum('bthd,bmhd->bhtm', qh, k).astype(jnp.float32) * (MEM_HEAD_DIM ** -0.5)
    p = jax.nn.softmax(s, axis=-1).astype(v.dtype)
    o = jnp.einsum('bhtm,bmhd->bthd', p, v)
    return o.reshape(B, T, MEM_W)


def causal_depthwise_conv(u, w, b):
    out = lax.conv_general_dilated(
        u, w[:, None, :], window_strides=(1,), padding=[(CONV_W - 1, 0)],
        dimension_numbers=('NWC', 'WIO', 'NWC'), feature_group_count=u.shape[-1])
    return out + b


def rg_lru(u, w_r, b_r, w_i, b_i, lam):
    B, T, C = u.shape
    ub = u.reshape(B, T, RNN_BLOCKS, RNN_BLOCK_W)
    r = jax.nn.sigmoid(jnp.einsum('btnc,ncd->btnd', ub, w_r).reshape(B, T, C) + b_r)
    i = jax.nn.sigmoid(jnp.einsum('btnc,ncd->btnd', ub, w_i).reshape(B, T, C) + b_i)
    log_a = -LRU_C * r.astype(jnp.float32) * jax.nn.softplus(-lam.astype(jnp.float32))
    a = jnp.exp(log_a)
    mult = jnp.sqrt(-jnp.expm1(2.0 * log_a))
    mult = jnp.where((jnp.arange(T) == 0)[None, :, None], 1.0, mult)
    bx = mult * (i * u).astype(jnp.float32)

    def combine(left, right):
        a_l, b_l = left
        a_r, b_r2 = right
        return a_r * a_l, a_r * b_l + b_r2

    _, h = lax.associative_scan(combine, (a, bx), axis=1)
    return h.astype(u.dtype)


def recurrent_mixer(h, mem, w_in, conv_w, conv_b, w_r, b_r, w_i, b_i, lam, w_out, w_mem_kv):
    z = h @ w_in
    gate = jax.nn.gelu(z[..., :D_RNN])
    rec = causal_depthwise_conv(z[..., D_RNN:2 * D_RNN], conv_w, conv_b)
    rnn_out = rg_lru(rec, w_r, b_r, w_i, b_i, lam) * gate
    mem_out = memory_attention(z[..., 2 * D_RNN:], mem, w_mem_kv)
    return jnp.concatenate([rnn_out, mem_out], axis=-1) @ w_out


def differential_attention(q, k, v, lam_vecs, subln_g, lam_init):
    B, T = q.shape[0], q.shape[1]
    lv = lam_vecs.astype(jnp.float32)
    lam = jnp.exp(jnp.sum(lv[0] * lv[1])) - jnp.exp(jnp.sum(lv[2] * lv[3])) + lam_init
    slopes = jnp.exp2(-8.0 * (jnp.arange(DIFF_HEADS, dtype=jnp.float32) + 1.0) / DIFF_HEADS)
    pos = jnp.arange(T, dtype=jnp.float32)
    scale = DIFF_HEAD_DIM ** -0.5
    outs = []
    for start in range(0, T, Q_BLOCK):
        end = start + Q_BLOCK
        s = jnp.einsum('bqhcd,bkhcd->bhcqk', q[:, start:end], k[:, :end]).astype(jnp.float32) * scale
        dist = pos[start:end, None] - pos[None, :end]
        s = s - slopes[None, :, None, None, None] * dist
        s = jnp.where(dist >= 0, s, -jnp.inf)
        p = jax.nn.softmax(s, axis=-1)
        wgt = (p[:, :, 0] - lam * p[:, :, 1]).astype(v.dtype)
        outs.append(jnp.einsum('bhqk,bkhe->bqhe', wgt, v[:, :end]))
    o = jnp.concatenate(outs, axis=1)
    o = rms_norm(o, subln_g) * (1.0 - lam_init)
    return o.reshape(B, T, DIFF_V_W)


def diff_mixer(h, mem, k_sh, v_sh, w_q, lam_vecs, subln_g, w_out, w_mem_kv, lam_init):
    B, T, _ = h.shape
    q = h @ w_q
    q_diff = q[..., :DIFF_QK_W].reshape(B, T, DIFF_HEADS, 2, DIFF_HEAD_DIM)
    attn = differential_attention(q_diff, k_sh, v_sh, lam_vecs, subln_g, lam_init)
    mem_out = memory_attention(q[..., DIFF_QK_W:], mem, w_mem_kv)
    return jnp.concatenate([attn, mem_out], axis=-1) @ w_out


def swiglu(h, w13, w2):
    gu = h @ w13
    f = w2.shape[0]
    return (jax.nn.silu(gu[..., :f]) * gu[..., f:]) @ w2


def moe_swiglu(h, w_router, w13, w2):
    B, T, D = h.shape
    xt = h.reshape(-1, D)
    n_tok = xt.shape[0]
    logits = (xt @ w_router).astype(jnp.float32)
    top_v, top_e = lax.top_k(logits, TOP_K)
    gates = jax.nn.softmax(top_v, axis=-1)
    flat_e = top_e.reshape(-1)
    order = jnp.argsort(flat_e)
    sorted_e = flat_e[order]
    sorted_tok = order // TOP_K
    sorted_gate = gates.reshape(-1)[order].astype(xt.dtype)
    counts = jnp.bincount(flat_e, length=N_EXPERTS)
    padded = (counts + MOE_BLOCK - 1) // MOE_BLOCK * MOE_BLOCK
    start = jnp.cumsum(counts) - counts
    pstart = jnp.cumsum(padded) - padded
    dest = pstart[sorted_e] + jnp.arange(n_tok * TOP_K) - start[sorted_e]
    n_rows = n_tok * TOP_K + N_EXPERTS * MOE_BLOCK
    n_blocks = n_rows // MOE_BLOCK
    buf = jnp.zeros((n_rows, D), xt.dtype).at[dest].set(xt[sorted_tok])
    block_e = jnp.searchsorted(jnp.cumsum(padded), jnp.arange(n_blocks) * MOE_BLOCK, side='right')
    block_e = jnp.minimum(block_e, N_EXPERTS - 1)

    def expert_block(args):
        xb, e = args
        gu = xb @ w13[e]
        return (jax.nn.silu(gu[:, :EXPERT_DIM]) * gu[:, EXPERT_DIM:]) @ w2[e]

    yb = lax.map(expert_block, (buf.reshape(n_blocks, MOE_BLOCK, D), block_e))
    y_sorted = yb.reshape(n_rows, D)[dest] * sorted_gate[:, None]
    y = jax.ops.segment_sum(y_sorted, sorted_tok, num_segments=n_tok)
    return y.reshape(B, T, D)


def setup_inputs(seed: int = 0) -> dict:
    key = jax.random.key(seed)
    ks = jax.random.split(key, 32)
    D = D_MODEL

    def nrm(k, shape, scale):
        return jax.random.normal(k, shape, jnp.float32) * scale

    a_pow_c = jax.random.uniform(ks[9], (N_A_LAYERS, D_RNN), jnp.float32, 0.9, 0.999)
    a_base = a_pow_c ** (1.0 / LRU_C)
    a_lambda = jnp.log(a_base) - jnp.log1p(-a_base)
    return {
        "x": nrm(ks[0], (BATCH, SEQ, D), 1.0),
        "mem": nrm(ks[1], (BATCH, MEM_TOKENS, D), 1.0),
        "a_w_in": nrm(ks[2], (N_A_LAYERS, D, 2 * D_RNN + MEM_W), D ** -0.5),
        "a_conv_w": nrm(ks[3], (N_A_LAYERS, CONV_W, D_RNN), CONV_W ** -0.5),
        "a_conv_b": nrm(ks[4], (N_A_LAYERS, D_RNN), 0.01),
        "a_w_rgate": nrm(ks[5], (N_A_LAYERS, RNN_BLOCKS, RNN_BLOCK_W, RNN_BLOCK_W), RNN_BLOCK_W ** -0.5),
        "a_b_rgate": nrm(ks[6], (N_A_LAYERS, D_RNN), 0.1),
        "a_w_igate": nrm(ks[7], (N_A_LAYERS, RNN_BLOCKS, RNN_BLOCK_W, RNN_BLOCK_W), RNN_BLOCK_W ** -0.5),
        "a_b_igate": nrm(ks[8], (N_A_LAYERS, D_RNN), 0.1),
        "a_lambda": a_lambda,
        "a_w_out": nrm(ks[10], (N_A_LAYERS, D_RNN + MEM_W, D), (D_RNN + MEM_W) ** -0.5 * DEEPNORM_BETA),
        "w_kv_shared": nrm(ks[11], (D, DIFF_QK_W + DIFF_V_W), D ** -0.5),
        "b_w_q": nrm(ks[12], (N_B_LAYERS, D, DIFF_QK_W + MEM_W), D ** -0.5),
        "b_lambda": nrm(ks[13], (N_B_LAYERS, 4, DIFF_HEAD_DIM), 0.1),
        "b_subln_g": 1.0 + nrm(ks[14], (N_B_LAYERS, DIFF_V_DIM), 0.02),
        "b_w_out": nrm(ks[15], (N_B_LAYERS, DIFF_V_W + MEM_W, D), (DIFF_V_W + MEM_W) ** -0.5 * DEEPNORM_BETA),
        "mem_w_kv": nrm(ks[16], (DEPTH, D, 2 * MEM_W), D ** -0.5),
        "ffn_w13": nrm(ks[17], (N_DENSE_FFN, D, 2 * FFN_DIM), D ** -0.5),
        "ffn_w2": nrm(ks[18], (N_DENSE_FFN, FFN_DIM, D), FFN_DIM ** -0.5 * DEEPNORM_BETA),
        "moe_router": nrm(ks[19], (N_MOE_FFN, D, N_EXPERTS), D ** -0.5),
        "moe_w13": nrm(ks[20], (N_MOE_FFN, N_EXPERTS, D, 2 * EXPERT_DIM), D ** -0.5),
        "moe_w2": nrm(ks[21], (N_MOE_FFN, N_EXPERTS, EXPERT_DIM, D), EXPERT_DIM ** -0.5 * DEEPNORM_BETA),
        "ln_g": 1.0 + nrm(ks[22], (DEPTH, 2, D), 0.02),
        "ln_b": nrm(ks[23], (DEPTH, 2, D), 0.01),
    }


def reference(x, mem, a_w_in, a_conv_w, a_conv_b, a_w_rgate, a_b_rgate, a_w_igate, a_b_igate,
              a_lambda, a_w_out, w_kv_shared, b_w_q, b_lambda, b_subln_g, b_w_out, mem_w_kv,
              ffn_w13, ffn_w2, moe_router, moe_w13, moe_w2, ln_g, ln_b):
    h = x
    B, T, _ = x.shape
    k_sh = None
    v_sh = None
    for layer in range(DEPTH):
        if layer < N_A_LAYERS:
            i = layer
            y = recurrent_mixer(h, mem, a_w_in[i], a_conv_w[i], a_conv_b[i], a_w_rgate[i], a_b_rgate[i],
                                a_w_igate[i], a_b_igate[i], a_lambda[i], a_w_out[i], mem_w_kv[layer])
        else:
            j = layer - N_A_LAYERS
            if j == 0:
                kv = h @ w_kv_shared
                k_sh = kv[..., :DIFF_QK_W].reshape(B, T, DIFF_HEADS, 2, DIFF_HEAD_DIM)
                v_sh = kv[..., DIFF_QK_W:].reshape(B, T, DIFF_HEADS, DIFF_V_DIM)
            lam_init = 0.8 - 0.6 * math.exp(-0.3 * layer)
            y = diff_mixer(h, mem, k_sh, v_sh, b_w_q[j], b_lambda[j], b_subln_g[j], b_w_out[j],
                           mem_w_kv[layer], lam_init)
        h = layer_norm(DEEPNORM_ALPHA * h + y, ln_g[layer, 0], ln_b[layer, 0])
        if layer % 2 == 0:
            y = swiglu(h, ffn_w13[layer // 2], ffn_w2[layer // 2])
        else:
            y = moe_swiglu(h, moe_router[layer // 2], moe_w13[layer // 2], moe_w2[layer // 2])
        h = layer_norm(DEEPNORM_ALPHA * h + y, ln_g[layer, 1], ln_b[layer, 1])
    return h
```

```python
import functools
import math

import jax
import jax.numpy as jnp
from jax import lax
from jax.experimental import pallas as pl
from jax.experimental.pallas import tpu as pltpu

F32 = jnp.float32
BF16 = jnp.bfloat16

D_MODEL = 1024
DEPTH = 2
D_RNN = D_MODEL
RNN_BLOCKS = 8
RNN_BLOCK_W = D_RNN // RNN_BLOCKS
CONV_W = 4
LRU_C = 8.0
MEM_HEADS = 4
MEM_HEAD_DIM = D_MODEL // 8
MEM_W = MEM_HEADS * MEM_HEAD_DIM
DIFF_HEADS = 8
DIFF_HEAD_DIM = D_MODEL // 16
DIFF_V_DIM = 2 * DIFF_HEAD_DIM
DIFF_QK_W = DIFF_HEADS * 2 * DIFF_HEAD_DIM
DIFF_V_W = DIFF_HEADS * DIFF_V_DIM
FFN_DIM = (7 * D_MODEL) // 2
N_EXPERTS = 8
TOP_K = 2
LN_EPS = 1e-5
DEEPNORM_ALPHA = (2.0 * DEPTH) ** 0.25

LANES = 128
SUBLANES = 8
VMEM_LIMIT = 52 * 1024 * 1024

FFN_TILE = 512
MOE_ROWS = 512
NEG_BIG = -1e30


def _params(semantics):
    return pltpu.CompilerParams(dimension_semantics=semantics, vmem_limit_bytes=VMEM_LIMIT)


def _layer_norm(z, g, b):
    mu = jnp.mean(z, axis=-1, keepdims=True)
    zc = z - mu
    var = jnp.mean(zc * zc, axis=-1, keepdims=True)
    return zc * lax.rsqrt(var + LN_EPS) * g + b


def _proj_body(x_ref, w_ref, *o_refs, splits, post):
    xb = x_ref[...].astype(BF16)
    for o_ref, (c0, c1), fn in zip(o_refs, splits, post):
        z = jnp.dot(xb, w_ref[:, c0:c1], preferred_element_type=F32)
        if fn is not None:
            z = fn(z)
        o_ref[...] = z.astype(o_ref.dtype)


def _proj(x, w, splits, post, out_dtypes, tm, name):
    m, k = x.shape
    return pl.pallas_call(
        functools.partial(_proj_body, splits=tuple(splits), post=tuple(post)),
        grid=(m // tm,),
        in_specs=[pl.BlockSpec((tm, k), lambda i: (i, 0)),
                  pl.BlockSpec(w.shape, lambda i: (0, 0))],
        out_specs=[pl.BlockSpec((tm, c1 - c0), lambda i: (i, 0)) for c0, c1 in splits],
        out_shape=[jax.ShapeDtypeStruct((m, c1 - c0), dt) for (c0, c1), dt in zip(splits, out_dtypes)],
        compiler_params=_params(("parallel",)),
        name=name,
    )(x, w)


def _rglru_body(u_ref, g_ref, cw_ref, cb_ref, wri_ref, br_ref, bi_ref, lam_ref, o_ref,
                ext_ref, rec_ref, a_ref, b_ref, h_ref, *, tt):
    t = pl.program_id(1)

    @pl.when(t == 0)
    def _():
        ext_ref[0:SUBLANES, :] = jnp.zeros((SUBLANES, D_RNN), F32)
        h_ref[...] = jnp.zeros((SUBLANES, D_RNN), F32)

    ext_ref[SUBLANES:, :] = u_ref[0].astype(F32)
    rec = cb_ref[...] + cw_ref[CONV_W - 1:CONV_W, :] * ext_ref[SUBLANES:SUBLANES + tt, :]
    for j in range(CONV_W - 1):
        off = SUBLANES - (CONV_W - 1) + j
        rec = rec + cw_ref[j:j + 1, :] * ext_ref[off:off + tt, :]
    rec_ref[...] = rec
    ext_ref[0:SUBLANES, :] = ext_ref[tt:tt + SUBLANES, :]

    sp = jax.nn.softplus(-lam_ref[...])
    first_row = (lax.broadcasted_iota(jnp.int32, (tt, RNN_BLOCK_W), 0) + t * tt) == 0
    for n in range(RNN_BLOCKS):
        blk = slice(n * RNN_BLOCK_W, (n + 1) * RNN_BLOCK_W)
        u_n = rec_ref[:, blk]
        ri = jnp.dot(u_n.astype(BF16), wri_ref[n], preferred_element_type=F32)
        r = jax.nn.sigmoid(ri[:, :RNN_BLOCK_W] + br_ref[:, blk])
        i = jax.nn.sigmoid(ri[:, RNN_BLOCK_W:] + bi_ref[:, blk])
        log_a = -LRU_C * r * sp[:, blk]
        a = jnp.exp(log_a)
        th = jnp.tanh(-log_a)
        mult = jnp.sqrt(2.0 * th / (1.0 + th))
        mult = jnp.where(first_row, 1.0, mult)
        a_ref[:, blk] = a
        b_ref[:, blk] = mult * (i * u_n)

    row = lax.broadcasted_iota(jnp.int32, (SUBLANES, D_RNN), 0)

    def tile(i, h_prev):
        r0 = pl.multiple_of(i * SUBLANES, SUBLANES)
        a = a_ref[pl.ds(r0, SUBLANES), :]
        b = b_ref[pl.ds(r0, SUBLANES), :]
        for s in (1, 2, 4):
            a_s = pltpu.roll(a, s, 0)
            b_s = pltpu.roll(b, s, 0)
            keep = row >= s
            b = jnp.where(keep, a * b_s + b, b)
            a = jnp.where(keep, a * a_s, a)
        h = b + a * h_prev
        b_ref[pl.ds(r0, SUBLANES), :] = h
        return jnp.broadcast_to(h[SUBLANES - 1:SUBLANES, :], (SUBLANES, D_RNN))

    h_ref[...] = lax.fori_loop(0, tt // SUBLANES, tile, h_ref[...])
    o_ref[0] = (b_ref[...] * g_ref[0].astype(F32)).astype(o_ref.dtype)


def _rglru(u_pre, gate, conv_w, conv_b, w_ri, b_r, b_i, lam, tt):
    bsz, t, c = u_pre.shape
    row = lambda a: a.reshape(1, c)
    full = lambda shape: pl.BlockSpec(shape, lambda b, i: (0,) * len(shape))
    return pl.pallas_call(
        functools.partial(_rglru_body, tt=tt),
        grid=(bsz, t // tt),
        in_specs=[pl.BlockSpec((1, tt, c), lambda b, i: (b, i, 0)),
                  pl.BlockSpec((1, tt, c), lambda b, i: (b, i, 0)),
                  full((CONV_W, c)), full((1, c)), full(w_ri.shape),
                  full((1, c)), full((1, c)), full((1, c))],
        out_specs=pl.BlockSpec((1, tt, c), lambda b, i: (b, i, 0)),
        out_shape=jax.ShapeDtypeStruct((bsz, t, c), BF16),
        scratch_shapes=[pltpu.VMEM((tt + SUBLANES, c), F32), pltpu.VMEM((tt, c), F32),
                        pltpu.VMEM((tt, c), F32), pltpu.VMEM((tt, c), F32),
                        pltpu.VMEM((SUBLANES, c), F32)],
        compiler_params=_params(("parallel", "arbitrary")),
        name="rglru",
    )(u_pre, gate, conv_w, row(conv_b), w_ri, row(b_r), row(b_i), row(lam))


def _outproj_ln_body(a_ref, qm_ref, kv_ref, w_ref, res_ref, g_ref, b_ref, o_ref):
    mix_w = a_ref.shape[-1]
    y = jnp.dot(a_ref[0], w_ref[0:mix_w, :], preferred_element_type=F32)
    scale = MEM_HEAD_DIM ** -0.5
    heads = []
    for h in range(MEM_HEADS):
        hs = slice(h * MEM_HEAD_DIM, (h + 1) * MEM_HEAD_DIM)
        vs = slice(MEM_W + h * MEM_HEAD_DIM, MEM_W + (h + 1) * MEM_HEAD_DIM)
        s = lax.dot_general(qm_ref[0, :, hs], kv_ref[0, :, hs], (((1,), (1,)), ((), ())),
                            preferred_element_type=F32) * scale
        e = jnp.exp(s - jnp.max(s, axis=-1, keepdims=True))
        p = e / jnp.sum(e, axis=-1, keepdims=True)
        heads.append(jnp.dot(p.astype(BF16), kv_ref[0, :, vs], preferred_element_type=F32))
    mem_out = jnp.concatenate(heads, axis=1).astype(BF16)
    y = y + jnp.dot(mem_out, w_ref[mix_w:, :], preferred_element_type=F32)
    z = DEEPNORM_ALPHA * res_ref[0] + y
    o_ref[0] = _layer_norm(z, g_ref[...], b_ref[...])


def _outproj_ln(mix, q_mem, kv_mem, w_out, resid, g, b, tm, name):
    bsz, t, mix_w = mix.shape
    n_mem = kv_mem.shape[1]
    return pl.pallas_call(
        _outproj_ln_body,
        grid=(bsz, t // tm),
        in_specs=[pl.BlockSpec((1, tm, mix_w), lambda b, i: (b, i, 0)),
                  pl.BlockSpec((1, tm, MEM_W), lambda b, i: (b, i, 0)),
                  pl.BlockSpec((1, n_mem, 2 * MEM_W), lambda b, i: (b, 0, 0)),
                  pl.BlockSpec(w_out.shape, lambda b, i: (0, 0)),
                  pl.BlockSpec((1, tm, D_MODEL), lambda b, i: (b, i, 0)),
                  pl.BlockSpec((1, D_MODEL), lambda b, i: (0, 0)),
                  pl.BlockSpec((1, D_MODEL), lambda b, i: (0, 0))],
        out_specs=pl.BlockSpec((1, tm, D_MODEL), lambda b, i: (b, i, 0)),
        out_shape=jax.ShapeDtypeStruct((bsz, t, D_MODEL), F32),
        compiler_params=_params(("parallel", "parallel")),
        name=name,
    )(mix, q_mem, kv_mem, w_out, resid, g.reshape(1, -1), b.reshape(1, -1))


def _ffn_ln_body(h_ref, w1_ref, w3_ref, w2_ref, g_ref, b_ref, o_ref, hb_ref, acc_ref):
    f = pl.program_id(1)

    @pl.when(f == 0)
    def _():
        hb_ref[...] = h_ref[...].astype(BF16)
        acc_ref[...] = jnp.zeros_like(acc_ref)

    hb = hb_ref[...]
    gate = jnp.dot(hb, w1_ref[...], preferred_element_type=F32)
    up = jnp.dot(hb, w3_ref[...], preferred_element_type=F32)
    act = (jax.nn.silu(gate) * up).astype(BF16)
    acc_ref[...] += jnp.dot(act, w2_ref[...], preferred_element_type=F32)

    @pl.when(f == pl.num_programs(1) - 1)
    def _():
        z = DEEPNORM_ALPHA * h_ref[...] + acc_ref[...]
        o_ref[...] = _layer_norm(z, g_ref[...], b_ref[...])


def _ffn_ln(h, w13, w2, g, b, tm):
    m, d = h.shape
    nf = FFN_DIM // FFN_TILE
    return pl.pallas_call(
        _ffn_ln_body,
        grid=(m // tm, nf),
        in_specs=[pl.BlockSpec((tm, d), lambda i, f: (i, 0)),
                  pl.BlockSpec((d, FFN_TILE), lambda i, f: (0, f)),
                  pl.BlockSpec((d, FFN_TILE), lambda i, f: (0, nf + f)),
                  pl.BlockSpec((FFN_TILE, d), lambda i, f: (f, 0)),
                  pl.BlockSpec((1, d), lambda i, f: (0, 0)),
                  pl.BlockSpec((1, d), lambda i, f: (0, 0))],
        out_specs=pl.BlockSpec((tm, d), lambda i, f: (i, 0)),
        out_shape=jax.ShapeDtypeStruct((m, d), F32),
        scratch_shapes=[pltpu.VMEM((tm, d), BF16), pltpu.VMEM((tm, d), F32)],
        compiler_params=_params(("parallel", "arbitrary")),
        name="ffn_ln",
    )(h, w13, w13, w2, g.reshape(1, -1), b.reshape(1, -1))


def _diffattn_body(lamv_ref, sub_ref, slope_ref, q_ref, k_ref, v_ref, o_ref, *, tq, lam_init):
    i = pl.program_id(2)
    lane = lax.broadcasted_iota(jnp.int32, (tq, 2 * DIFF_HEAD_DIM), 1)
    q = q_ref[0]
    zero = jnp.zeros_like(q)
    qq = jnp.concatenate([jnp.where(lane < DIFF_HEAD_DIM, q, zero),
                          jnp.where(lane < DIFF_HEAD_DIM, zero, q)], axis=0)
    r_idx = lax.broadcasted_iota(jnp.int32, (2 * tq, tq), 0)
    r_idx = jnp.where(r_idx >= tq, r_idx - tq, r_idx)
    c_idx = lax.broadcasted_iota(jnp.int32, (2 * tq, tq), 1)
    rel = (r_idx - c_idx).astype(F32)
    slope = slope_ref[0]

    def step(j, carry, masked):
        m, l, acc = carry
        k0 = pl.multiple_of(j * tq, tq)
        kj = k_ref[0, pl.ds(k0, tq), :]
        vj = v_ref[0, pl.ds(k0, tq), :]
        s = lax.dot_general(qq, kj, (((1,), (1,)), ((), ())), preferred_element_type=F32)
        dist = rel + ((i - j) * tq).astype(F32)
        s = s - slope * dist
        if masked:
            s = jnp.where(rel >= 0.0, s, -jnp.inf)
        m_new = jnp.maximum(m, jnp.max(s, axis=-1, keepdims=True))
        corr = jnp.exp(m - m_new)
        p = jnp.exp(s - m_new)
        l = corr * l + jnp.sum(p, axis=-1, keepdims=True)
        acc = corr * acc + jnp.dot(p.astype(BF16), vj, preferred_element_type=F32)
        return m_new, l, acc

    init = (jnp.full((2 * tq, 1), NEG_BIG, F32), jnp.zeros((2 * tq, 1), F32),
            jnp.zeros((2 * tq, DIFF_V_DIM), F32))
    carry = lax.fori_loop(0, i, functools.partial(step, masked=False), init)
    _, l, acc = step(i, carry, True)
    o = acc / l
    lv = lamv_ref[...]
    lam = (jnp.exp(jnp.sum(lv[0:1] * lv[1:2], axis=-1, keepdims=True))
           - jnp.exp(jnp.sum(lv[2:3] * lv[3:4], axis=-1, keepdims=True)) + lam_init)
    o = o[:tq] - lam * o[tq:]
    o = o * lax.rsqrt(jnp.mean(o * o, axis=-1, keepdims=True) + LN_EPS) * sub_ref[...]
    o_ref[0] = (o * (1.0 - lam_init)).astype(o_ref.dtype)


def _diffattn(q, k, v, lam_vecs, subln_g, lam_init, tq):
    bsz, t, _ = q.shape
    slopes = jnp.exp2(-8.0 * (jnp.arange(DIFF_HEADS, dtype=F32) + 1.0) / DIFF_HEADS)
    slopes = jnp.broadcast_to(slopes[:, None, None], (DIFF_HEADS, 1, tq))
    hd = 2 * DIFF_HEAD_DIM
    return pl.pallas_call(
        functools.partial(_diffattn_body, tq=tq, lam_init=lam_init),
        grid=(bsz, DIFF_HEADS, t // tq),
        in_specs=[pl.BlockSpec(lam_vecs.shape, lambda b, h, i: (0, 0)),
                  pl.BlockSpec((1, DIFF_V_DIM), lambda b, h, i: (0, 0)),
                  pl.BlockSpec((1, 1, tq), lambda b, h, i: (h, 0, 0)),
                  pl.BlockSpec((1, tq, hd), lambda b, h, i: (b, i, h)),
                  pl.BlockSpec((1, t, hd), lambda b, h, i: (b, 0, h)),
                  pl.BlockSpec((1, t, DIFF_V_DIM), lambda b, h, i: (b, 0, h))],
        out_specs=pl.BlockSpec((1, tq, DIFF_V_DIM), lambda b, h, i: (b, i, h)),
        out_shape=jax.ShapeDtypeStruct((bsz, t, DIFF_V_W), BF16),
        compiler_params=_params(("parallel", "parallel", "arbitrary")),
        name="diffattn",
    )(lam_vecs, subln_g.reshape(1, -1), slopes, q, k, v)


def _router_body(h_ref, w_ref, o_ref, cnt_ref, carry_ref, *, tm):
    @pl.when(pl.program_id(0) == 0)
    def _():
        carry_ref[...] = jnp.zeros_like(carry_ref)

    logits = jnp.dot(h_ref[...], w_ref[...], preferred_element_type=F32,
                     precision=lax.Precision.HIGHEST)
    lane = lax.broadcasted_iota(jnp.int32, (tm, LANES), 1)
    lg = jnp.where(lane < N_EXPERTS, logits, -jnp.inf)
    v1 = jnp.max(lg, axis=-1, keepdims=True)
    i1 = jnp.min(jnp.where(lg == v1, lane, LANES), axis=-1, keepdims=True)
    oh1 = lane == i1
    lg2 = jnp.where(oh1, -jnp.inf, lg)
    v2 = jnp.max(lg2, axis=-1, keepdims=True)
    i2 = jnp.min(jnp.where(lg2 == v2, lane, LANES), axis=-1, keepdims=True)
    oh2 = lane == i2
    e2 = jnp.exp(v2 - v1)
    g1 = 1.0 / (1.0 + e2)
    g2 = e2 / (1.0 + e2)

    both = jnp.where(oh1, 1.0, jnp.where(oh2, 1.0, 0.0))
    tri = (lax.broadcasted_iota(jnp.int32, (tm, tm), 0)
           > lax.broadcasted_iota(jnp.int32, (tm, tm), 1))
    tri = jnp.where(tri, 1.0, 0.0).astype(BF16)
    before = jnp.dot(tri, both.astype(BF16), preferred_element_type=F32) + carry_ref[0:1, :]
    rank1 = jnp.sum(jnp.where(oh1, before, 0.0), axis=-1, keepdims=True)
    rank2 = jnp.sum(jnp.where(oh2, before, 0.0), axis=-1, keepdims=True)
    total = carry_ref[...] + jnp.sum(both, axis=0, keepdims=True)
    carry_ref[...] = total
    cnt_ref[...] = total

    out = jnp.where(lane == 0, i1.astype(F32), 0.0)
    out = jnp.where(lane == 1, i2.astype(F32), out)
    out = jnp.where(lane == 2, rank1, out)
    out = jnp.where(lane == 3, rank2, out)
    out = jnp.where(lane == 4, g1, out)
    out = jnp.where(lane == 5, g2, out)
    o_ref[...] = out


def _router(h, w_router, tm):
    m, d = h.shape
    w = jnp.zeros((d, LANES), F32).at[:, :N_EXPERTS].set(w_router)
    return pl.pallas_call(
        functools.partial(_router_body, tm=tm),
        grid=(m // tm,),
        in_specs=[pl.BlockSpec((tm, d), lambda i: (i, 0)),
                  pl.BlockSpec((d, LANES), lambda i: (0, 0))],
        out_specs=[pl.BlockSpec((tm, LANES), lambda i: (i, 0)),
                   pl.BlockSpec((SUBLANES, LANES), lambda i: (0, 0))],
        out_shape=[jax.ShapeDtypeStruct((m, LANES), F32),
                   jax.ShapeDtypeStruct((SUBLANES, LANES), F32)],
        scratch_shapes=[pltpu.VMEM((SUBLANES, LANES), F32)],
        compiler_params=_params(("arbitrary",)),
        name="router",
    )(h, w)


def _dispatch_body(dest_ref, h_ref, buf_in_ref, buf_ref, sem, *, tm, n_tok):
    del buf_in_ref
    base = pl.program_id(0) * tm

    def row_copy(r, d):
        return pltpu.make_async_copy(h_ref.at[pl.ds(r, 1), :], buf_ref.at[pl.ds(d, 1), :], sem)

    def issue(r, c):
        row_copy(r, dest_ref[base + r]).start()
        row_copy(r, dest_ref[n_tok + base + r]).start()
        return c

    def drain(r, c):
        row_copy(0, 0).wait()
        row_copy(0, 0).wait()
        return c

    lax.fori_loop(0, tm, issue, 0)
    lax.fori_loop(0, tm, drain, 0)


def _dispatch(dest, h, n_rows, tm):
    m, d = h.shape
    buf0 = jnp.zeros((n_rows, d), F32)
    return pl.pallas_call(
        functools.partial(_dispatch_body, tm=tm, n_tok=m),
        grid_spec=pltpu.PrefetchScalarGridSpec(
            num_scalar_prefetch=1,
            grid=(m // tm,),
            in_specs=[pl.BlockSpec((tm, d), lambda i, dest: (i, 0)),
                      pl.BlockSpec(memory_space=pl.ANY)],
            out_specs=pl.BlockSpec(memory_space=pl.ANY),
            scratch_shapes=[pltpu.SemaphoreType.DMA(())]),
        out_shape=jax.ShapeDtypeStruct((n_rows, d), F32),
        input_output_aliases={2: 0},
        compiler_params=_params(("arbitrary",)),
        name="moe_dispatch",
    )(dest, h, buf0)


def _moe_body(be_ref, nv_ref, x_ref, w1_ref, w3_ref, w2_ref, o_ref, xb_ref, acc_ref):
    b = pl.program_id(0)
    f = pl.program_id(1)

    @pl.when(b < nv_ref[0])
    def _():
        @pl.when(f == 0)
        def _():
            xb_ref[...] = x_ref[...].astype(BF16)
            acc_ref[...] = jnp.zeros_like(acc_ref)

        xb = xb_ref[...]
        gate = jnp.dot(xb, w1_ref[0], preferred_element_type=F32)
        up = jnp.dot(xb, w3_ref[0], preferred_element_type=F32)
        act = (jax.nn.silu(gate) * up).astype(BF16)
        acc_ref[...] += jnp.dot(act, w2_ref[0], preferred_element_type=F32)

        @pl.when(f == pl.num_programs(1) - 1)
        def _():
            o_ref[...] = acc_ref[...]

    @pl.when(jnp.logical_and(b >= nv_ref[0], f == pl.num_programs(1) - 1))
    def _():
        o_ref[...] = jnp.zeros_like(o_ref)


def _moe(block_e, n_valid, buf, w13, w2):
    n_rows, d = buf.shape
    nb = n_rows // MOE_ROWS
    nf = FFN_DIM // FFN_TILE

    def blk(b, nv):
        return jnp.minimum(b, nv[0] - 1)

    def ftile(b, f, nv):
        return jnp.where(b < nv[0], f, nf - 1)

    return pl.pallas_call(
        _moe_body,
        grid_spec=pltpu.PrefetchScalarGridSpec(
            num_scalar_prefetch=2,
            grid=(nb, nf),
            in_specs=[pl.BlockSpec((MOE_ROWS, d), lambda b, f, be, nv: (blk(b, nv), 0)),
                      pl.BlockSpec((1, d, FFN_TILE),
                                   lambda b, f, be, nv: (be[blk(b, nv)], 0, ftile(b, f, nv))),
                      pl.BlockSpec((1, d, FFN_TILE),
                                   lambda b, f, be, nv: (be[blk(b, nv)], 0, nf + ftile(b, f, nv))),
                      pl.BlockSpec((1, FFN_TILE, d),
                                   lambda b, f, be, nv: (be[blk(b, nv)], ftile(b, f, nv), 0))],
            out_specs=pl.BlockSpec((MOE_ROWS, d), lambda b, f, be, nv: (b, 0)),
            scratch_shapes=[pltpu.VMEM((MOE_ROWS, d), BF16), pltpu.VMEM((MOE_ROWS, d), F32)]),
        out_shape=jax.ShapeDtypeStruct((n_rows, d), F32),
        compiler_params=_params(("arbitrary", "arbitrary")),
        name="moe_experts",
    )(block_e, n_valid, buf, w13, w13, w2)


def _combine_ln_body(dest_ref, h_ref, route_ref, g_ref, b_ref, yb_ref, o_ref, rows_ref, sem,
                     *, tm, n_tok):
    base = pl.program_id(0) * tm

    def row_copy(r, d, slot):
        return pltpu.make_async_copy(yb_ref.at[pl.ds(d, 1), :],
                                     rows_ref.at[slot, pl.ds(r, 1), :], sem)

    def issue(r, c):
        row_copy(r, dest_ref[base + r], 0).start()
        row_copy(r, dest_ref[n_tok + base + r], 1).start()
        return c

    def drain(r, c):
        row_copy(0, 0, 0).wait()
        row_copy(0, 0, 1).wait()
        return c

    lax.fori_loop(0, tm, issue, 0)
    lax.fori_loop(0, tm, drain, 0)
    route = route_ref[...]
    y = route[:, 4:5] * rows_ref[0] + route[:, 5:6] * rows_ref[1]
    z = DEEPNORM_ALPHA * h_ref[...] + y
    o_ref[...] = _layer_norm(z, g_ref[...], b_ref[...])


def _combine_ln(dest, h, route, yb, g, b, tm):
    m, d = h.shape
    return pl.pallas_call(
        functools.partial(_combine_ln_body, tm=tm, n_tok=m),
        grid_spec=pltpu.PrefetchScalarGridSpec(
            num_scalar_prefetch=1,
            grid=(m // tm,),
            in_specs=[pl.BlockSpec((tm, d), lambda i, dest: (i, 0)),
                      pl.BlockSpec((tm, LANES), lambda i, dest: (i, 0)),
                      pl.BlockSpec((1, d), lambda i, dest: (0, 0)),
                      pl.BlockSpec((1, d), lambda i, dest: (0, 0)),
                      pl.BlockSpec(memory_space=pl.ANY)],
            out_specs=pl.BlockSpec((tm, d), lambda i, dest: (i, 0)),
            scratch_shapes=[pltpu.VMEM((TOP_K, tm, d), F32), pltpu.SemaphoreType.DMA(())]),
        out_shape=jax.ShapeDtypeStruct((m, d), F32),
        compiler_params=_params(("arbitrary",)),
        name="moe_combine_ln",
    )(dest, h, route, g.reshape(1, -1), b.reshape(1, -1), yb)


def _moe_layer(h, w_router, w13, w2, g, b):
    n_tok, _ = h.shape
    route, cnt = _router(h, w_router, tm=512)
    counts = cnt[0, :N_EXPERTS].astype(jnp.int32)
    padded = (counts + MOE_ROWS - 1) // MOE_ROWS * MOE_ROWS
    pstart = jnp.cumsum(padded) - padded
    experts = route[:, 0:2].astype(jnp.int32)
    ranks = route[:, 2:4].astype(jnp.int32)
    dest = (pstart[experts] + ranks).T.reshape(-1)
    n_rows = n_tok * TOP_K + N_EXPERTS * MOE_ROWS
    n_blocks = n_rows // MOE_ROWS
    block_end = jnp.cumsum(padded // MOE_ROWS)
    block_e = jnp.searchsorted(block_end, jnp.arange(n_blocks), side="right")
    block_e = jnp.minimum(block_e, N_EXPERTS - 1).astype(jnp.int32)
    n_valid = block_end[-1:].astype(jnp.int32)
    buf = _dispatch(dest, h, n_rows, tm=256)
    yb = _moe(block_e, n_valid, buf, w13, w2)
    return _combine_ln(dest, h, route, yb, g, b, tm=256)


def kernel(x, mem, a_w_in, a_conv_w, a_conv_b, a_w_rgate, a_b_rgate, a_w_igate, a_b_igate, a_lambda,
           a_w_out, w_kv_shared, b_w_q, b_lambda, b_subln_g, b_w_out, mem_w_kv, ffn_w13, ffn_w2,
           moe_router, moe_w13, moe_w2, ln_g, ln_b):
    bsz, t, d = x.shape
    n_tok = bsz * t
    n_mem = mem.shape[1]
    bf = lambda a: a.astype(BF16)

    kv_mem = _proj(mem.reshape(bsz * n_mem, d), bf(jnp.concatenate([mem_w_kv[0], mem_w_kv[1]], axis=1)),
                   [(0, 2 * MEM_W), (2 * MEM_W, 4 * MEM_W)], [None, None], [BF16, BF16],
                   tm=min(1024, bsz * n_mem), name="mem_kv")
    kv_mem = [a.reshape(bsz, n_mem, 2 * MEM_W) for a in kv_mem]

    x2 = x.reshape(n_tok, d)
    gate, u_pre, q_mem = _proj(
        x2, bf(a_w_in[0]), [(0, D_RNN), (D_RNN, 2 * D_RNN), (2 * D_RNN, 2 * D_RNN + MEM_W)],
        [jax.nn.gelu, None, None], [BF16, BF16, BF16], tm=512, name="proj_in")
    w_ri = bf(jnp.concatenate([a_w_rgate[0], a_w_igate[0]], axis=-1))
    rnn = _rglru(u_pre.reshape(bsz, t, D_RNN), gate.reshape(bsz, t, D_RNN), a_conv_w[0], a_conv_b[0],
                 w_ri, a_b_rgate[0], a_b_igate[0], a_lambda[0], tt=512)
    h = _outproj_ln(rnn, q_mem.reshape(bsz, t, MEM_W), kv_mem[0], bf(a_w_out[0]), x,
                    ln_g[0, 0], ln_b[0, 0], tm=512, name="outproj_ln_a")
    h = _ffn_ln(h.reshape(n_tok, d), bf(ffn_w13[0]), bf(ffn_w2[0]), ln_g[0, 1], ln_b[0, 1], tm=1024)

    layer = 1
    lam_init = 0.8 - 0.6 * math.exp(-0.3 * layer)
    w_cat = bf(jnp.concatenate([w_kv_shared, b_w_q[0]], axis=1))
    scale = DIFF_HEAD_DIM ** -0.5
    k_sh, v_sh, q_diff, q_mem = _proj(
        h, w_cat, [(0, DIFF_QK_W), (DIFF_QK_W, DIFF_QK_W + DIFF_V_W),
                   (DIFF_QK_W + DIFF_V_W, 2 * DIFF_QK_W + DIFF_V_W),
                   (2 * DIFF_QK_W + DIFF_V_W, 2 * DIFF_QK_W + DIFF_V_W + MEM_W)],
        [None, None, lambda z: z * scale, None], [BF16, BF16, BF16, BF16], tm=512, name="proj_kvq")
    attn = _diffattn(q_diff.reshape(bsz, t, DIFF_QK_W), k_sh.reshape(bsz, t, DIFF_QK_W),
                     v_sh.reshape(bsz, t, DIFF_V_W), b_lambda[0], b_subln_g[0], lam_init, tq=256)
    h = _outproj_ln(attn, q_mem.reshape(bsz, t, MEM_W), kv_mem[1], bf(b_w_out[0]),
                    h.reshape(bsz, t, d), ln_g[1, 0], ln_b[1, 0], tm=512, name="outproj_ln_b")
    out = _moe_layer(h.reshape(n_tok, d), moe_router[0], bf(moe_w13[0]), bf(moe_w2[0]),
                     ln_g[1, 1], ln_b[1, 1])
    return out.reshape(bsz, t, d)
```

```python
import functools
import math

import jax
import jax.numpy as jnp
from jax import lax
from jax.experimental import pallas as pl
from jax.experimental.pallas import tpu as pltpu

F32 = jnp.float32
BF16 = jnp.bfloat16

D_MODEL = 1024
DEPTH = 2
D_RNN = D_MODEL
RNN_BLOCKS = 8
RNN_BLOCK_W = D_RNN // RNN_BLOCKS
CONV_W = 4
LRU_C = 8.0
MEM_HEADS = 4
MEM_HEAD_DIM = D_MODEL // 8
MEM_W = MEM_HEADS * MEM_HEAD_DIM
DIFF_HEADS = 8
DIFF_HEAD_DIM = D_MODEL // 16
DIFF_V_DIM = 2 * DIFF_HEAD_DIM
DIFF_QK_W = DIFF_HEADS * 2 * DIFF_HEAD_DIM
DIFF_V_W = DIFF_HEADS * DIFF_V_DIM
FFN_DIM = (7 * D_MODEL) // 2
N_EXPERTS = 8
TOP_K = 2
LN_EPS = 1e-5
DEEPNORM_ALPHA = (2.0 * DEPTH) ** 0.25

LANES = 128
SUBLANES = 8
VMEM_LIMIT = 52 * 1024 * 1024

FFN_TILE = 512
MOE_ROWS = 512
NEG_BIG = -1e30
LOG2E = math.log2(math.e)


def _params(semantics):
    return pltpu.CompilerParams(dimension_semantics=semantics, vmem_limit_bytes=VMEM_LIMIT)


def _layer_norm(z, g, b):
    mu = jnp.mean(z, axis=-1, keepdims=True)
    zc = z - mu
    var = jnp.mean(zc * zc, axis=-1, keepdims=True)
    return zc * lax.rsqrt(var + LN_EPS) * g + b


def _proj_body(x_ref, w_ref, *o_refs, splits, post):
    xb = x_ref[...].astype(BF16)
    for o_ref, (c0, c1), fn in zip(o_refs, splits, post):
        z = jnp.dot(xb, w_ref[:, c0:c1], preferred_element_type=F32)
        if fn is not None:
            z = fn(z)
        o_ref[...] = z.astype(o_ref.dtype)


def _proj(x, w, splits, post, out_dtypes, tm, name):
    m, k = x.shape
    return pl.pallas_call(
        functools.partial(_proj_body, splits=tuple(splits), post=tuple(post)),
        grid=(m // tm,),
        in_specs=[pl.BlockSpec((tm, k), lambda i: (i, 0)),
                  pl.BlockSpec(w.shape, lambda i: (0, 0))],
        out_specs=[pl.BlockSpec((tm, c1 - c0), lambda i: (i, 0)) for c0, c1 in splits],
        out_shape=[jax.ShapeDtypeStruct((m, c1 - c0), dt) for (c0, c1), dt in zip(splits, out_dtypes)],
        compiler_params=_params(("parallel",)),
        name=name,
    )(x, w)


def _rglru_body(u_ref, g_ref, cw_ref, cb_ref, wri_ref, br_ref, bi_ref, lam_ref, o_ref,
                ext_ref, rec_ref, a_ref, b_ref, h_ref, *, tt):
    t = pl.program_id(1)

    @pl.when(t == 0)
    def _():
        ext_ref[0:SUBLANES, :] = jnp.zeros((SUBLANES, D_RNN), F32)
        h_ref[...] = jnp.zeros((SUBLANES, D_RNN), F32)

    ext_ref[SUBLANES:, :] = u_ref[0].astype(F32)
    rec = cb_ref[...] + cw_ref[CONV_W - 1:CONV_W, :] * ext_ref[SUBLANES:SUBLANES + tt, :]
    for j in range(CONV_W - 1):
        off = SUBLANES - (CONV_W - 1) + j
        rec = rec + cw_ref[j:j + 1, :] * ext_ref[off:off + tt, :]
    rec_ref[...] = rec
    ext_ref[0:SUBLANES, :] = ext_ref[tt:tt + SUBLANES, :]

    sp = jax.nn.softplus(-lam_ref[...])
    first_row = (lax.broadcasted_iota(jnp.int32, (tt, RNN_BLOCK_W), 0) + t * tt) == 0
    for n in range(RNN_BLOCKS):
        blk = slice(n * RNN_BLOCK_W, (n + 1) * RNN_BLOCK_W)
        u_n = rec_ref[:, blk]
        ri = jnp.dot(u_n.astype(BF16), wri_ref[n], preferred_element_type=F32)
        r = jax.nn.sigmoid(ri[:, :RNN_BLOCK_W] + br_ref[:, blk])
        i = jax.nn.sigmoid(ri[:, RNN_BLOCK_W:] + bi_ref[:, blk])
        log_a = -LRU_C * r * sp[:, blk]
        a = jnp.exp(log_a)
        th = jnp.tanh(-log_a)
        mult = jnp.sqrt(2.0 * th / (1.0 + th))
        mult = jnp.where(first_row, 1.0, mult)
        a_ref[:, blk] = a
        b_ref[:, blk] = mult * (i * u_n)

    row = lax.broadcasted_iota(jnp.int32, (SUBLANES, D_RNN), 0)

    def tile(i, h_prev):
        r0 = pl.multiple_of(i * SUBLANES, SUBLANES)
        a = a_ref[pl.ds(r0, SUBLANES), :]
        b = b_ref[pl.ds(r0, SUBLANES), :]
        for s in (1, 2, 4):
            a_s = pltpu.roll(a, s, 0)
            b_s = pltpu.roll(b, s, 0)
            keep = row >= s
            b = jnp.where(keep, a * b_s + b, b)
            a = jnp.where(keep, a * a_s, a)
        h = b + a * h_prev
        b_ref[pl.ds(r0, SUBLANES), :] = h
        return jnp.broadcast_to(h[SUBLANES - 1:SUBLANES, :], (SUBLANES, D_RNN))

    h_ref[...] = lax.fori_loop(0, tt // SUBLANES, tile, h_ref[...])
    o_ref[0] = (b_ref[...] * g_ref[0].astype(F32)).astype(o_ref.dtype)


def _rglru(u_pre, gate, conv_w, conv_b, w_ri, b_r, b_i, lam, tt):
    bsz, t, c = u_pre.shape
    row = lambda a: a.reshape(1, c)
    full = lambda shape: pl.BlockSpec(shape, lambda b, i: (0,) * len(shape))
    return pl.pallas_call(
        functools.partial(_rglru_body, tt=tt),
        grid=(bsz, t // tt),
        in_specs=[pl.BlockSpec((1, tt, c), lambda b, i: (b, i, 0)),
                  pl.BlockSpec((1, tt, c), lambda b, i: (b, i, 0)),
                  full((CONV_W, c)), full((1, c)), full(w_ri.shape),
                  full((1, c)), full((1, c)), full((1, c))],
        out_specs=pl.BlockSpec((1, tt, c), lambda b, i: (b, i, 0)),
        out_shape=jax.ShapeDtypeStruct((bsz, t, c), BF16),
        scratch_shapes=[pltpu.VMEM((tt + SUBLANES, c), F32), pltpu.VMEM((tt, c), F32),
                        pltpu.VMEM((tt, c), F32), pltpu.VMEM((tt, c), F32),
                        pltpu.VMEM((SUBLANES, c), F32)],
        compiler_params=_params(("parallel", "arbitrary")),
        name="rglru",
    )(u_pre, gate, conv_w, row(conv_b), w_ri, row(b_r), row(b_i), row(lam))


def _outproj_ln_body(a_ref, qm_ref, kv_ref, w_ref, res_ref, g_ref, b_ref, o_ref):
    mix_w = a_ref.shape[-1]
    y = jnp.dot(a_ref[0], w_ref[0:mix_w, :], preferred_element_type=F32)
    scale = MEM_HEAD_DIM ** -0.5
    heads = []
    for h in range(MEM_HEADS):
        hs = slice(h * MEM_HEAD_DIM, (h + 1) * MEM_HEAD_DIM)
        vs = slice(MEM_W + h * MEM_HEAD_DIM, MEM_W + (h + 1) * MEM_HEAD_DIM)
        s = lax.dot_general(qm_ref[0, :, hs], kv_ref[0, :, hs], (((1,), (1,)), ((), ())),
                            preferred_element_type=F32) * scale
        e = jnp.exp(s - jnp.max(s, axis=-1, keepdims=True))
        p = e / jnp.sum(e, axis=-1, keepdims=True)
        heads.append(jnp.dot(p.astype(BF16), kv_ref[0, :, vs], preferred_element_type=F32))
    mem_out = jnp.concatenate(heads, axis=1).astype(BF16)
    y = y + jnp.dot(mem_out, w_ref[mix_w:, :], preferred_element_type=F32)
    z = DEEPNORM_ALPHA * res_ref[0] + y
    o_ref[0] = _layer_norm(z, g_ref[...], b_ref[...])


def _outproj_ln(mix, q_mem, kv_mem, w_out, resid, g, b, tm, name):
    bsz, t, mix_w = mix.shape
    n_mem = kv_mem.shape[1]
    return pl.pallas_call(
        _outproj_ln_body,
        grid=(bsz, t // tm),
        in_specs=[pl.BlockSpec((1, tm, mix_w), lambda b, i: (b, i, 0)),
                  pl.BlockSpec((1, tm, MEM_W), lambda b, i: (b, i, 0)),
                  pl.BlockSpec((1, n_mem, 2 * MEM_W), lambda b, i: (b, 0, 0)),
                  pl.BlockSpec(w_out.shape, lambda b, i: (0, 0)),
                  pl.BlockSpec((1, tm, D_MODEL), lambda b, i: (b, i, 0)),
                  pl.BlockSpec((1, D_MODEL), lambda b, i: (0, 0)),
                  pl.BlockSpec((1, D_MODEL), lambda b, i: (0, 0))],
        out_specs=pl.BlockSpec((1, tm, D_MODEL), lambda b, i: (b, i, 0)),
        out_shape=jax.ShapeDtypeStruct((bsz, t, D_MODEL), F32),
        compiler_params=_params(("parallel", "parallel")),
        name=name,
    )(mix, q_mem, kv_mem, w_out, resid, g.reshape(1, -1), b.reshape(1, -1))


def _ffn_ln_body(h_ref, w1_ref, w3_ref, w2_ref, g_ref, b_ref, o_ref, hb_ref, acc_ref):
    f = pl.program_id(1)

    @pl.when(f == 0)
    def _():
        hb_ref[...] = h_ref[...].astype(BF16)
        acc_ref[...] = jnp.zeros_like(acc_ref)

    hb = hb_ref[...]
    gate = jnp.dot(hb, w1_ref[...], preferred_element_type=F32)
    up = jnp.dot(hb, w3_ref[...], preferred_element_type=F32)
    act = (jax.nn.silu(gate) * up).astype(BF16)
    acc_ref[...] += jnp.dot(act, w2_ref[...], preferred_element_type=F32)

    @pl.when(f == pl.num_programs(1) - 1)
    def _():
        z = DEEPNORM_ALPHA * h_ref[...] + acc_ref[...]
        o_ref[...] = _layer_norm(z, g_ref[...], b_ref[...])


def _ffn_ln(h, w13, w2, g, b, tm):
    m, d = h.shape
    nf = FFN_DIM // FFN_TILE
    return pl.pallas_call(
        _ffn_ln_body,
        grid=(m // tm, nf),
        in_specs=[pl.BlockSpec((tm, d), lambda i, f: (i, 0)),
                  pl.BlockSpec((d, FFN_TILE), lambda i, f: (0, f)),
                  pl.BlockSpec((d, FFN_TILE), lambda i, f: (0, nf + f)),
                  pl.BlockSpec((FFN_TILE, d), lambda i, f: (f, 0)),
                  pl.BlockSpec((1, d), lambda i, f: (0, 0)),
                  pl.BlockSpec((1, d), lambda i, f: (0, 0))],
        out_specs=pl.BlockSpec((tm, d), lambda i, f: (i, 0)),
        out_shape=jax.ShapeDtypeStruct((m, d), F32),
        scratch_shapes=[pltpu.VMEM((tm, d), BF16), pltpu.VMEM((tm, d), F32)],
        compiler_params=_params(("parallel", "arbitrary")),
        name="ffn_ln",
    )(h, w13, w13, w2, g.reshape(1, -1), b.reshape(1, -1))


ATT_STRIP = 128


def _diffattn_body(pi_ref, pj_ref, lamv_ref, sub_ref, coef_ref, q_ref, k_ref, v_ref, o_ref,
                   kaug_ref, vaug_ref, qq_ref, mask_ref, s_ref, p_ref, c_ref, m_ref, acc_ref,
                   *, tq, n_pairs, unroll, lam_init):
    t = k_ref.shape[1]
    hd = 2 * DIFF_HEAD_DIM
    n_strips = 2 * tq // ATT_STRIP

    pos = lax.broadcasted_iota(jnp.int32, (t, LANES), 0)
    feat_lane = lax.broadcasted_iota(jnp.int32, (t, LANES), 1)
    feat = jnp.where(feat_lane < 3, pos // tq, jnp.where(feat_lane < 6, pos % tq, 0))
    kaug_ref[:, 0:hd] = k_ref[0]
    kaug_ref[:, hd:] = feat.astype(F32).astype(BF16)
    vaug_ref[:, 0:DIFF_V_DIM] = v_ref[0]
    vaug_ref[:, DIFF_V_DIM:] = jnp.ones((t, LANES), BF16)
    lane = lax.broadcasted_iota(jnp.int32, (tq, hd), 1)
    coef_rows = jnp.broadcast_to(coef_ref[0], (2 * tq, LANES))
    for qb in range(t // tq):
        q = q_ref[0, qb * tq:(qb + 1) * tq, :]
        zero = jnp.zeros_like(q)
        qq_ref[qb * 2 * tq:qb * 2 * tq + tq, 0:hd] = jnp.where(lane < DIFF_HEAD_DIM, q, zero)
        qq_ref[qb * 2 * tq + tq:(qb + 1) * 2 * tq, 0:hd] = jnp.where(lane < DIFF_HEAD_DIM, zero, q)
        qq_ref[qb * 2 * tq:(qb + 1) * 2 * tq, hd:] = coef_rows
    q_row = lax.broadcasted_iota(jnp.int32, (tq, tq), 0)
    k_col = lax.broadcasted_iota(jnp.int32, (tq, tq), 1)
    mask_ref[0] = jnp.zeros((tq, tq), F32)
    mask_ref[1] = jnp.where(k_col <= q_row, 0.0, NEG_BIG)
    m_ref[...] = jnp.full(m_ref.shape, NEG_BIG, F32)
    acc_ref[...] = jnp.zeros_like(acc_ref)

    def stage_qk(p, slot):
        q0 = pl.multiple_of(pi_ref[p] * (2 * tq), 2 * tq)
        k0 = pl.multiple_of(pj_ref[p] * tq, tq)
        s_ref[slot] = lax.dot_general(qq_ref[pl.ds(q0, 2 * tq), :], kaug_ref[pl.ds(k0, tq), :],
                                      (((1,), (1,)), ((), ())), preferred_element_type=F32)

    def stage_sm(p, slot):
        i = pi_ref[p]
        diag = (i == pj_ref[p]).astype(jnp.int32)
        for r in range(n_strips):
            rows = slice(r * ATT_STRIP, (r + 1) * ATT_STRIP)
            mrows = slice((r * ATT_STRIP) % tq, (r * ATT_STRIP) % tq + ATT_STRIP)
            s = s_ref[slot, rows, :] + mask_ref[diag, mrows, :]
            m_prev = m_ref[i, rows, :]
            m_next = jnp.maximum(m_prev, jnp.max(s, axis=-1, keepdims=True))
            p_blk = jnp.exp2(s - jnp.tile(m_next, (1, tq // LANES)))
            p_ref[slot, rows, :] = p_blk.astype(BF16)
            c_ref[slot, rows, :] = jnp.exp2(m_prev - m_next)
            m_ref[i, rows, :] = m_next

    def stage_pv(p, slot):
        i = pi_ref[p]
        k0 = pl.multiple_of(pj_ref[p] * tq, tq)
        pv = jnp.dot(p_ref[slot], vaug_ref[pl.ds(k0, tq), :], preferred_element_type=F32)
        acc_ref[i] = acc_ref[i] * jnp.tile(c_ref[slot], (1, 2)) + pv

    stage_qk(0, 0)
    stage_qk(1, 1)
    stage_sm(0, 0)

    def pipeline_step(step, c):
        for u in range(unroll):
            p = unroll * step + u
            stage_qk(p + 2, u % 2)
            stage_sm(p + 1, (u + 1) % 2)
            stage_pv(p, u % 2)
        return c

    lax.fori_loop(0, n_pairs // unroll, pipeline_step, 0)

    lv = lamv_ref[...]
    lam = (jnp.exp(jnp.sum(lv[0:1] * lv[1:2], axis=-1, keepdims=True))
           - jnp.exp(jnp.sum(lv[2:3] * lv[3:4], axis=-1, keepdims=True)) + lam_init)

    def finalize(i, c):
        o = acc_ref[i, :, 0:DIFF_V_DIM] / acc_ref[i, :, DIFF_V_DIM:]
        o = o[:tq] - lam * o[tq:]
        o = o * lax.rsqrt(jnp.mean(o * o, axis=-1, keepdims=True) + LN_EPS) * sub_ref[...]
        r0 = pl.multiple_of(i * tq, tq)
        o_ref[0, pl.ds(r0, tq), :] = (o * (1.0 - lam_init)).astype(o_ref.dtype)
        return c

    lax.fori_loop(0, t // tq, finalize, 0)


def _diffattn(q, k, v, lam_vecs, subln_g, lam_init, tq):
    bsz, t, _ = q.shape
    slopes = jnp.exp2(-8.0 * (jnp.arange(DIFF_HEADS, dtype=F32) + 1.0) / DIFF_HEADS) * LOG2E
    def pieces(c):
        c1 = c.astype(BF16)
        c2 = (c - c1.astype(F32)).astype(BF16)
        c3 = (c - c1.astype(F32) - c2.astype(F32)).astype(BF16)
        return [c1, c2, c3]

    coef = jnp.zeros((DIFF_HEADS, 1, LANES), BF16)
    coef = coef.at[:, 0, 0:6].set(jnp.stack(pieces(slopes * tq) + pieces(slopes), axis=1))
    hd = 2 * DIFF_HEAD_DIM
    nq = t // tq
    assert nq <= 256, "key block indices must be exact in bf16"
    pairs = [(i, j) for i in range(nq) for j in range(i + 1)]
    n_pairs = len(pairs)
    unroll = 4 if n_pairs % 4 == 0 else 2
    assert n_pairs % unroll == 0
    pairs = pairs + [pairs[-1]] * 2
    pair_i = jnp.asarray([p[0] for p in pairs], jnp.int32)
    pair_j = jnp.asarray([p[1] for p in pairs], jnp.int32)
    head = lambda shape: pl.BlockSpec(shape, lambda b, h, pi, pj: (b, 0, h))
    return pl.pallas_call(
        functools.partial(_diffattn_body, tq=tq, n_pairs=n_pairs, unroll=unroll, lam_init=lam_init),
        grid_spec=pltpu.PrefetchScalarGridSpec(
            num_scalar_prefetch=2,
            grid=(bsz, DIFF_HEADS),
            in_specs=[pl.BlockSpec(lam_vecs.shape, lambda b, h, pi, pj: (0, 0)),
                      pl.BlockSpec((1, DIFF_V_DIM), lambda b, h, pi, pj: (0, 0)),
                      pl.BlockSpec((1, 1, LANES), lambda b, h, pi, pj: (h, 0, 0)),
                      head((1, t, hd)), head((1, t, hd)), head((1, t, DIFF_V_DIM))],
            out_specs=head((1, t, DIFF_V_DIM)),
            scratch_shapes=[pltpu.VMEM((t, hd + LANES), BF16),
                            pltpu.VMEM((t, DIFF_V_DIM + LANES), BF16),
                            pltpu.VMEM((2 * t, hd + LANES), BF16),
                            pltpu.VMEM((2, tq, tq), F32),
                            pltpu.VMEM((2, 2 * tq, tq), F32),
                            pltpu.VMEM((2, 2 * tq, tq), BF16),
                            pltpu.VMEM((2, 2 * tq, LANES), F32),
                            pltpu.VMEM((nq, 2 * tq, LANES), F32),
                            pltpu.VMEM((nq, 2 * tq, DIFF_V_DIM + LANES), F32)]),
        out_shape=jax.ShapeDtypeStruct((bsz, t, DIFF_V_W), BF16),
        compiler_params=_params(("parallel", "parallel")),
        name="diffattn",
    )(pair_i, pair_j, lam_vecs, subln_g.reshape(1, -1), coef, q, k, v)


def _router_body(h_ref, w_ref, o_ref, cnt_ref, carry_ref, *, tm):
    @pl.when(pl.program_id(0) == 0)
    def _():
        carry_ref[...] = jnp.zeros_like(carry_ref)

    logits = jnp.dot(h_ref[...], w_ref[...], preferred_element_type=F32,
                     precision=lax.Precision.HIGHEST)
    lane = lax.broadcasted_iota(jnp.int32, (tm, LANES), 1)
    lg = jnp.where(lane < N_EXPERTS, logits, -jnp.inf)
    v1 = jnp.max(lg, axis=-1, keepdims=True)
    i1 = jnp.min(jnp.where(lg == v1, lane, LANES), axis=-1, keepdims=True)
    oh1 = lane == i1
    lg2 = jnp.where(oh1, -jnp.inf, lg)
    v2 = jnp.max(lg2, axis=-1, keepdims=True)
    i2 = jnp.min(jnp.where(lg2 == v2, lane, LANES), axis=-1, keepdims=True)
    oh2 = lane == i2
    e2 = jnp.exp(v2 - v1)
    g1 = 1.0 / (1.0 + e2)
    g2 = e2 / (1.0 + e2)

    both = jnp.where(oh1, 1.0, jnp.where(oh2, 1.0, 0.0))
    tri = (lax.broadcasted_iota(jnp.int32, (tm, tm), 0)
           > lax.broadcasted_iota(jnp.int32, (tm, tm), 1))
    tri = jnp.where(tri, 1.0, 0.0).astype(BF16)
    before = jnp.dot(tri, both.astype(BF16), preferred_element_type=F32) + carry_ref[0:1, :]
    rank1 = jnp.sum(jnp.where(oh1, before, 0.0), axis=-1, keepdims=True)
    rank2 = jnp.sum(jnp.where(oh2, before, 0.0), axis=-1, keepdims=True)
    total = carry_ref[...] + jnp.sum(both, axis=0, keepdims=True)
    carry_ref[...] = total
    cnt_ref[...] = total

    out = jnp.where(lane == 0, i1.astype(F32), 0.0)
    out = jnp.where(lane == 1, i2.astype(F32), out)
    out = jnp.where(lane == 2, rank1, out)
    out = jnp.where(lane == 3, rank2, out)
    out = jnp.where(lane == 4, g1, out)
    out = jnp.where(lane == 5, g2, out)
    o_ref[...] = out


def _router(h, w_router, tm):
    m, d = h.shape
    w = jnp.zeros((d, LANES), F32).at[:, :N_EXPERTS].set(w_router)
    return pl.pallas_call(
        functools.partial(_router_body, tm=tm),
        grid=(m // tm,),
        in_specs=[pl.BlockSpec((tm, d), lambda i: (i, 0)),
                  pl.BlockSpec((d, LANES), lambda i: (0, 0))],
        out_specs=[pl.BlockSpec((tm, LANES), lambda i: (i, 0)),
                   pl.BlockSpec((SUBLANES, LANES), lambda i: (0, 0))],
        out_shape=[jax.ShapeDtypeStruct((m, LANES), F32),
                   jax.ShapeDtypeStruct((SUBLANES, LANES), F32)],
        scratch_shapes=[pltpu.VMEM((SUBLANES, LANES), F32)],
        compiler_params=_params(("arbitrary",)),
        name="router",
    )(h, w)


def _dispatch_body(dest_ref, h_ref, buf_in_ref, buf_ref, sem, *, tm, n_tok):
    del buf_in_ref
    base = pl.program_id(0) * tm

    def row_copy(r, d):
        return pltpu.make_async_copy(h_ref.at[pl.ds(r, 1), :], buf_ref.at[pl.ds(d, 1), :], sem)

    def issue(r, c):
        row_copy(r, dest_ref[base + r]).start()
        row_copy(r, dest_ref[n_tok + base + r]).start()
        return c

    def drain(r, c):
        row_copy(0, 0).wait()
        row_copy(0, 0).wait()
        return c

    lax.fori_loop(0, tm, issue, 0)
    lax.fori_loop(0, tm, drain, 0)


def _dispatch(dest, h, n_rows, tm):
    m, d = h.shape
    buf0 = jnp.zeros((n_rows, d), F32)
    return pl.pallas_call(
        functools.partial(_dispatch_body, tm=tm, n_tok=m),
        grid_spec=pltpu.PrefetchScalarGridSpec(
            num_scalar_prefetch=1,
            grid=(m // tm,),
            in_specs=[pl.BlockSpec((tm, d), lambda i, dest: (i, 0)),
                      pl.BlockSpec(memory_space=pl.ANY)],
            out_specs=pl.BlockSpec(memory_space=pl.ANY),
            scratch_shapes=[pltpu.SemaphoreType.DMA(())]),
        out_shape=jax.ShapeDtypeStruct((n_rows, d), F32),
        input_output_aliases={2: 0},
        compiler_params=_params(("arbitrary",)),
        name="moe_dispatch",
    )(dest, h, buf0)


def _moe_body(be_ref, nv_ref, x_ref, w1_ref, w3_ref, w2_ref, o_ref, xb_ref, acc_ref):
    b = pl.program_id(0)
    f = pl.program_id(1)

    @pl.when(b < nv_ref[0])
    def _():
        @pl.when(f == 0)
        def _():
            xb_ref[...] = x_ref[...].astype(BF16)
            acc_ref[...] = jnp.zeros_like(acc_ref)

        xb = xb_ref[...]
        gate = jnp.dot(xb, w1_ref[0], preferred_element_type=F32)
        up = jnp.dot(xb, w3_ref[0], preferred_element_type=F32)
        act = (jax.nn.silu(gate) * up).astype(BF16)
        acc_ref[...] += jnp.dot(act, w2_ref[0], preferred_element_type=F32)

        @pl.when(f == pl.num_programs(1) - 1)
        def _():
            o_ref[...] = acc_ref[...]

    @pl.when(jnp.logical_and(b >= nv_ref[0], f == pl.num_programs(1) - 1))
    def _():
        o_ref[...] = jnp.zeros_like(o_ref)


def _moe(block_e, n_valid, buf, w13, w2):
    n_rows, d = buf.shape
    nb = n_rows // MOE_ROWS
    nf = FFN_DIM // FFN_TILE

    def blk(b, nv):
        return jnp.minimum(b, nv[0] - 1)

    def ftile(b, f, nv):
        return jnp.where(b < nv[0], f, nf - 1)

    return pl.pallas_call(
        _moe_body,
        grid_spec=pltpu.PrefetchScalarGridSpec(
            num_scalar_prefetch=2,
            grid=(nb, nf),
            in_specs=[pl.BlockSpec((MOE_ROWS, d), lambda b, f, be, nv: (blk(b, nv), 0)),
                      pl.BlockSpec((1, d, FFN_TILE),
                                   lambda b, f, be, nv: (be[blk(b, nv)], 0, ftile(b, f, nv))),
                      pl.BlockSpec((1, d, FFN_TILE),
                                   lambda b, f, be, nv: (be[blk(b, nv)], 0, nf + ftile(b, f, nv))),
                      pl.BlockSpec((1, FFN_TILE, d),
                                   lambda b, f, be, nv: (be[blk(b, nv)], ftile(b, f, nv), 0))],
            out_specs=pl.BlockSpec((MOE_ROWS, d), lambda b, f, be, nv: (b, 0)),
            scratch_shapes=[pltpu.VMEM((MOE_ROWS, d), BF16), pltpu.VMEM((MOE_ROWS, d), F32)]),
        out_shape=jax.ShapeDtypeStruct((n_rows, d), F32),
        compiler_params=_params(("arbitrary", "arbitrary")),
        name="moe_experts",
    )(block_e, n_valid, buf, w13, w13, w2)


def _combine_ln_body(dest_ref, h_ref, route_ref, g_ref, b_ref, yb_ref, o_ref, rows_ref, sem,
                     *, tm, n_tok):
    base = pl.program_id(0) * tm

    def row_copy(r, d, slot):
        return pltpu.make_async_copy(yb_ref.at[pl.ds(d, 1), :],
                                     rows_ref.at[slot, pl.ds(r, 1), :], sem)

    def issue(r, c):
        row_copy(r, dest_ref[base + r], 0).start()
        row_copy(r, dest_ref[n_tok + base + r], 1).start()
        return c

    def drain(r, c):
        row_copy(0, 0, 0).wait()
        row_copy(0, 0, 1).wait()
        return c

    lax.fori_loop(0, tm, issue, 0)
    lax.fori_loop(0, tm, drain, 0)
    route = route_ref[...]
    y = route[:, 4:5] * rows_ref[0] + route[:, 5:6] * rows_ref[1]
    z = DEEPNORM_ALPHA * h_ref[...] + y
    o_ref[...] = _layer_norm(z, g_ref[...], b_ref[...])


def _combine_ln(dest, h, route, yb, g, b, tm):
    m, d = h.shape
    return pl.pallas_call(
        functools.partial(_combine_ln_body, tm=tm, n_tok=m),
        grid_spec=pltpu.PrefetchScalarGridSpec(
            num_scalar_prefetch=1,
            grid=(m // tm,),
            in_specs=[pl.BlockSpec((tm, d), lambda i, dest: (i, 0)),
                      pl.BlockSpec((tm, LANES), lambda i, dest: (i, 0)),
                      pl.BlockSpec((1, d), lambda i, dest: (0, 0)),
                      pl.BlockSpec((1, d), lambda i, dest: (0, 0)),
                      pl.BlockSpec(memory_space=pl.ANY)],
            out_specs=pl.BlockSpec((tm, d), lambda i, dest: (i, 0)),
            scratch_shapes=[pltpu.VMEM((TOP_K, tm, d), F32), pltpu.SemaphoreType.DMA(())]),
        out_shape=jax.ShapeDtypeStruct((m, d), F32),
        compiler_params=_params(("arbitrary",)),
        name="moe_combine_ln",
    )(dest, h, route, g.reshape(1, -1), b.reshape(1, -1), yb)


def _moe_layer(h, w_router, w13, w2, g, b):
    n_tok, _ = h.shape
    route, cnt = _router(h, w_router, tm=512)
    counts = cnt[0, :N_EXPERTS].astype(jnp.int32)
    padded = (counts + MOE_ROWS - 1) // MOE_ROWS * MOE_ROWS
    pstart = jnp.cumsum(padded) - padded
    experts = route[:, 0:2].astype(jnp.int32)
    ranks = route[:, 2:4].astype(jnp.int32)
    dest = (pstart[experts] + ranks).T.reshape(-1)
    n_rows = n_tok * TOP_K + N_EXPERTS * MOE_ROWS
    n_blocks = n_rows // MOE_ROWS
    block_end = jnp.cumsum(padded // MOE_ROWS)
    block_e = jnp.sum(jnp.arange(n_blocks)[:, None] >= block_end[None, :], axis=1)
    block_e = jnp.minimum(block_e, N_EXPERTS - 1).astype(jnp.int32)
    n_valid = block_end[-1:].astype(jnp.int32)
    buf = _dispatch(dest, h, n_rows, tm=256)
    yb = _moe(block_e, n_valid, buf, w13, w2)
    return _combine_ln(dest, h, route, yb, g, b, tm=256)


def kernel(x, mem, a_w_in, a_conv_w, a_conv_b, a_w_rgate, a_b_rgate, a_w_igate, a_b_igate, a_lambda,
           a_w_out, w_kv_shared, b_w_q, b_lambda, b_subln_g, b_w_out, mem_w_kv, ffn_w13, ffn_w2,
           moe_router, moe_w13, moe_w2, ln_g, ln_b):
    bsz, t, d = x.shape
    n_tok = bsz * t
    n_mem = mem.shape[1]
    bf = lambda a: a.astype(BF16)

    kv_mem = _proj(mem.reshape(bsz * n_mem, d), bf(jnp.concatenate([mem_w_kv[0], mem_w_kv[1]], axis=1)),
                   [(0, 2 * MEM_W), (2 * MEM_W, 4 * MEM_W)], [None, None], [BF16, BF16],
                   tm=min(1024, bsz * n_mem), name="mem_kv")
    kv_mem = [a.reshape(bsz, n_mem, 2 * MEM_W) for a in kv_mem]

    x2 = x.reshape(n_tok, d)
    gate, u_pre, q_mem = _proj(
        x2, bf(a_w_in[0]), [(0, D_RNN), (D_RNN, 2 * D_RNN), (2 * D_RNN, 2 * D_RNN + MEM_W)],
        [jax.nn.gelu, None, None], [BF16, BF16, BF16], tm=512, name="proj_in")
    w_ri = bf(jnp.concatenate([a_w_rgate[0], a_w_igate[0]], axis=-1))
    rnn = _rglru(u_pre.reshape(bsz, t, D_RNN), gate.reshape(bsz, t, D_RNN), a_conv_w[0], a_conv_b[0],
                 w_ri, a_b_rgate[0], a_b_igate[0], a_lambda[0], tt=512)
    h = _outproj_ln(rnn, q_mem.reshape(bsz, t, MEM_W), kv_mem[0], bf(a_w_out[0]), x,
                    ln_g[0, 0], ln_b[0, 0], tm=512, name="outproj_ln_a")
    h = _ffn_ln(h.reshape(n_tok, d), bf(ffn_w13[0]), bf(ffn_w2[0]), ln_g[0, 1], ln_b[0, 1], tm=1024)

    layer = 1
    lam_init = 0.8 - 0.6 * math.exp(-0.3 * layer)
    w_cat = bf(jnp.concatenate([w_kv_shared, b_w_q[0]], axis=1))
    scale = DIFF_HEAD_DIM ** -0.5 * LOG2E
    k_sh, v_sh, q_diff, q_mem = _proj(
        h, w_cat, [(0, DIFF_QK_W), (DIFF_QK_W, DIFF_QK_W + DIFF_V_W),
                   (DIFF_QK_W + DIFF_V_W, 2 * DIFF_QK_W + DIFF_V_W),
                   (2 * DIFF_QK_W + DIFF_V_W, 2 * DIFF_QK_W + DIFF_V_W + MEM_W)],
        [None, None, lambda z: z * scale, None], [BF16, BF16, BF16, BF16], tm=512, name="proj_kvq")
    attn = _diffattn(q_diff.reshape(bsz, t, DIFF_QK_W), k_sh.reshape(bsz, t, DIFF_QK_W),
                     v_sh.reshape(bsz, t, DIFF_V_W), b_lambda[0], b_subln_g[0], lam_init, tq=256)
    h = _outproj_ln(attn, q_mem.reshape(bsz, t, MEM_W), kv_mem[1], bf(b_w_out[0]),
                    h.reshape(bsz, t, d), ln_g[1, 0], ln_b[1, 0], tm=512, name="outproj_ln_b")
    out = _moe_layer(h.reshape(n_tok, d), moe_router[0], bf(moe_w13[0]), bf(moe_w2[0]),
                     ln_g[1, 1], ln_b[1, 1])
    return out.reshape(bsz, t, d)
```

```python
import functools
import math

import jax
import jax.numpy as jnp
from jax import lax
from jax.experimental import pallas as pl
from jax.experimental.pallas import tpu as pltpu

F32 = jnp.float32
BF16 = jnp.bfloat16

D_MODEL = 1024
DEPTH = 2
D_RNN = D_MODEL
RNN_BLOCKS = 8
RNN_BLOCK_W = D_RNN // RNN_BLOCKS
CONV_W = 4
LRU_C = 8.0
MEM_HEADS = 4
MEM_HEAD_DIM = D_MODEL // 8
MEM_W = MEM_HEADS * MEM_HEAD_DIM
DIFF_HEADS = 8
DIFF_HEAD_DIM = D_MODEL // 16
DIFF_V_DIM = 2 * DIFF_HEAD_DIM
DIFF_QK_W = DIFF_HEADS * 2 * DIFF_HEAD_DIM
DIFF_V_W = DIFF_HEADS * DIFF_V_DIM
FFN_DIM = (7 * D_MODEL) // 2
N_EXPERTS = 8
TOP_K = 2
LN_EPS = 1e-5
DEEPNORM_ALPHA = (2.0 * DEPTH) ** 0.25

LANES = 128
SUBLANES = 8
VMEM_LIMIT = 52 * 1024 * 1024

FFN_TILE = 1792
MOE_ROWS = 512
NEG_BIG = -1e30
LOG2E = math.log2(math.e)


def _params(semantics):
    return pltpu.CompilerParams(dimension_semantics=semantics, vmem_limit_bytes=VMEM_LIMIT)


def _layer_norm(z, g, b):
    mu = jnp.mean(z, axis=-1, keepdims=True)
    zc = z - mu
    var = jnp.mean(zc * zc, axis=-1, keepdims=True)
    return zc * lax.rsqrt(var + LN_EPS) * g + b


def _proj_body(x_ref, w_ref, *o_refs, splits, post):
    xb = x_ref[...].astype(BF16)
    for o_ref, (c0, c1), fn in zip(o_refs, splits, post):
        z = jnp.dot(xb, w_ref[:, c0:c1], preferred_element_type=F32)
        if fn is not None:
            z = fn(z)
        o_ref[...] = z.astype(o_ref.dtype)


def _proj(x, w, splits, post, out_dtypes, tm, name):
    m, k = x.shape
    return pl.pallas_call(
        functools.partial(_proj_body, splits=tuple(splits), post=tuple(post)),
        grid=(m // tm,),
        in_specs=[pl.BlockSpec((tm, k), lambda i: (i, 0)),
                  pl.BlockSpec(w.shape, lambda i: (0, 0))],
        out_specs=[pl.BlockSpec((tm, c1 - c0), lambda i: (i, 0)) for c0, c1 in splits],
        out_shape=[jax.ShapeDtypeStruct((m, c1 - c0), dt) for (c0, c1), dt in zip(splits, out_dtypes)],
        compiler_params=_params(("parallel",)),
        name=name,
    )(x, w)


def _rglru_body(u_ref, g_ref, cw_ref, cb_ref, wri_ref, br_ref, bi_ref, lam_ref, o_ref,
                ext_ref, rec_ref, a_ref, b_ref, h_ref, *, tt):
    t = pl.program_id(1)

    @pl.when(t == 0)
    def _():
        ext_ref[0:SUBLANES, :] = jnp.zeros((SUBLANES, D_RNN), F32)
        h_ref[...] = jnp.zeros((SUBLANES, D_RNN), F32)

    ext_ref[SUBLANES:, :] = u_ref[0].astype(F32)
    rec = cb_ref[...] + cw_ref[CONV_W - 1:CONV_W, :] * ext_ref[SUBLANES:SUBLANES + tt, :]
    for j in range(CONV_W - 1):
        off = SUBLANES - (CONV_W - 1) + j
        rec = rec + cw_ref[j:j + 1, :] * ext_ref[off:off + tt, :]
    rec_ref[...] = rec
    ext_ref[0:SUBLANES, :] = ext_ref[tt:tt + SUBLANES, :]

    sp = jax.nn.softplus(-lam_ref[...])
    first_row = (lax.broadcasted_iota(jnp.int32, (tt, RNN_BLOCK_W), 0) + t * tt) == 0
    for n in range(RNN_BLOCKS):
        blk = slice(n * RNN_BLOCK_W, (n + 1) * RNN_BLOCK_W)
        u_n = rec_ref[:, blk]
        ri = jnp.dot(u_n.astype(BF16), wri_ref[n], preferred_element_type=F32)
        r = jax.nn.sigmoid(ri[:, :RNN_BLOCK_W] + br_ref[:, blk])
        i = jax.nn.sigmoid(ri[:, RNN_BLOCK_W:] + bi_ref[:, blk])
        log_a = -LRU_C * r * sp[:, blk]
        a = jnp.exp(log_a)
        th = jnp.tanh(-log_a)
        mult = jnp.sqrt(2.0 * th / (1.0 + th))
        mult = jnp.where(first_row, 1.0, mult)
        a_ref[:, blk] = a
        b_ref[:, blk] = mult * (i * u_n)

    row = lax.broadcasted_iota(jnp.int32, (SUBLANES, D_RNN), 0)

    def tile(i, h_prev):
        r0 = pl.multiple_of(i * SUBLANES, SUBLANES)
        a = a_ref[pl.ds(r0, SUBLANES), :]
        b = b_ref[pl.ds(r0, SUBLANES), :]
        for s in (1, 2, 4):
            a_s = pltpu.roll(a, s, 0)
            b_s = pltpu.roll(b, s, 0)
            keep = row >= s
            b = jnp.where(keep, a * b_s + b, b)
            a = jnp.where(keep, a * a_s, a)
        h = b + a * h_prev
        b_ref[pl.ds(r0, SUBLANES), :] = h
        return jnp.broadcast_to(h[SUBLANES - 1:SUBLANES, :], (SUBLANES, D_RNN))

    h_ref[...] = lax.fori_loop(0, tt // SUBLANES, tile, h_ref[...])
    o_ref[0] = (b_ref[...] * g_ref[0].astype(F32)).astype(o_ref.dtype)


def _rglru(u_pre, gate, conv_w, conv_b, w_ri, b_r, b_i, lam, tt):
    bsz, t, c = u_pre.shape
    row = lambda a: a.reshape(1, c)
    full = lambda shape: pl.BlockSpec(shape, lambda b, i: (0,) * len(shape))
    return pl.pallas_call(
        functools.partial(_rglru_body, tt=tt),
        grid=(bsz, t // tt),
        in_specs=[pl.BlockSpec((1, tt, c), lambda b, i: (b, i, 0)),
                  pl.BlockSpec((1, tt, c), lambda b, i: (b, i, 0)),
                  full((CONV_W, c)), full((1, c)), full(w_ri.shape),
                  full((1, c)), full((1, c)), full((1, c))],
        out_specs=pl.BlockSpec((1, tt, c), lambda b, i: (b, i, 0)),
        out_shape=jax.ShapeDtypeStruct((bsz, t, c), BF16),
        scratch_shapes=[pltpu.VMEM((tt + SUBLANES, c), F32), pltpu.VMEM((tt, c), F32),
                        pltpu.VMEM((tt, c), F32), pltpu.VMEM((tt, c), F32),
                        pltpu.VMEM((SUBLANES, c), F32)],
        compiler_params=_params(("parallel", "arbitrary")),
        name="rglru",
    )(u_pre, gate, conv_w, row(conv_b), w_ri, row(b_r), row(b_i), row(lam))


def _outproj_ln_body(a_ref, qm_ref, kv_ref, w_ref, res_ref, g_ref, b_ref, o_ref):
    mix_w = a_ref.shape[-1]
    y = jnp.dot(a_ref[0], w_ref[0:mix_w, :], preferred_element_type=F32)
    scale = MEM_HEAD_DIM ** -0.5
    heads = []
    for h in range(MEM_HEADS):
        hs = slice(h * MEM_HEAD_DIM, (h + 1) * MEM_HEAD_DIM)
        vs = slice(MEM_W + h * MEM_HEAD_DIM, MEM_W + (h + 1) * MEM_HEAD_DIM)
        s = lax.dot_general(qm_ref[0, :, hs], kv_ref[0, :, hs], (((1,), (1,)), ((), ())),
                            preferred_element_type=F32) * scale
        e = jnp.exp(s - jnp.max(s, axis=-1, keepdims=True))
        p = e / jnp.sum(e, axis=-1, keepdims=True)
        heads.append(jnp.dot(p.astype(BF16), kv_ref[0, :, vs], preferred_element_type=F32))
    mem_out = jnp.concatenate(heads, axis=1).astype(BF16)
    y = y + jnp.dot(mem_out, w_ref[mix_w:, :], preferred_element_type=F32)
    z = DEEPNORM_ALPHA * res_ref[0] + y
    o_ref[0] = _layer_norm(z, g_ref[...], b_ref[...])


def _outproj_ln(mix, q_mem, kv_mem, w_out, resid, g, b, tm, name):
    bsz, t, mix_w = mix.shape
    n_mem = kv_mem.shape[1]
    return pl.pallas_call(
        _outproj_ln_body,
        grid=(bsz, t // tm),
        in_specs=[pl.BlockSpec((1, tm, mix_w), lambda b, i: (b, i, 0)),
                  pl.BlockSpec((1, tm, MEM_W), lambda b, i: (b, i, 0)),
                  pl.BlockSpec((1, n_mem, 2 * MEM_W), lambda b, i: (b, 0, 0)),
                  pl.BlockSpec(w_out.shape, lambda b, i: (0, 0)),
                  pl.BlockSpec((1, tm, D_MODEL), lambda b, i: (b, i, 0)),
                  pl.BlockSpec((1, D_MODEL), lambda b, i: (0, 0)),
                  pl.BlockSpec((1, D_MODEL), lambda b, i: (0, 0))],
        out_specs=pl.BlockSpec((1, tm, D_MODEL), lambda b, i: (b, i, 0)),
        out_shape=jax.ShapeDtypeStruct((bsz, t, D_MODEL), F32),
        compiler_params=_params(("parallel", "parallel")),
        name=name,
    )(mix, q_mem, kv_mem, w_out, resid, g.reshape(1, -1), b.reshape(1, -1))


def _ffn_ln_body(h_ref, w1_ref, w3_ref, w2_ref, g_ref, b_ref, o_ref, hb_ref, acc_ref):
    f = pl.program_id(1)

    @pl.when(f == 0)
    def _():
        hb_ref[...] = h_ref[...].astype(BF16)
        acc_ref[...] = jnp.zeros_like(acc_ref)

    hb = hb_ref[...]
    gate = jnp.dot(hb, w1_ref[...], preferred_element_type=F32)
    up = jnp.dot(hb, w3_ref[...], preferred_element_type=F32)
    act = (jax.nn.silu(gate) * up).astype(BF16)
    acc_ref[...] += jnp.dot(act, w2_ref[...], preferred_element_type=F32)

    @pl.when(f == pl.num_programs(1) - 1)
    def _():
        z = DEEPNORM_ALPHA * h_ref[...] + acc_ref[...]
        o_ref[...] = _layer_norm(z, g_ref[...], b_ref[...])


def _ffn_ln(h, w13, w2, g, b, tm):
    m, d = h.shape
    nf = FFN_DIM // FFN_TILE
    return pl.pallas_call(
        _ffn_ln_body,
        grid=(m // tm, nf),
        in_specs=[pl.BlockSpec((tm, d), lambda i, f: (i, 0)),
                  pl.BlockSpec((d, FFN_TILE), lambda i, f: (0, f)),
                  pl.BlockSpec((d, FFN_TILE), lambda i, f: (0, nf + f)),
                  pl.BlockSpec((FFN_TILE, d), lambda i, f: (f, 0)),
                  pl.BlockSpec((1, d), lambda i, f: (0, 0)),
                  pl.BlockSpec((1, d), lambda i, f: (0, 0))],
        out_specs=pl.BlockSpec((tm, d), lambda i, f: (i, 0)),
        out_shape=jax.ShapeDtypeStruct((m, d), F32),
        scratch_shapes=[pltpu.VMEM((tm, d), BF16), pltpu.VMEM((tm, d), F32)],
        compiler_params=_params(("parallel", "arbitrary")),
        name="ffn_ln",
    )(h, w13, w13, w2, g.reshape(1, -1), b.reshape(1, -1))


ATT_STRIP = 128


def _diffattn_body(pi_ref, pj_ref, lamv_ref, sub_ref, coef_ref, q_ref, k_ref, v_ref, o_ref,
                   kaug_ref, vaug_ref, qq_ref, mask_ref, s_ref, p_ref, c_ref, m_ref, acc_ref,
                   *, tq, n_pairs, unroll, lam_init):
    t = k_ref.shape[1]
    hd = 2 * DIFF_HEAD_DIM
    n_strips = 2 * tq // ATT_STRIP

    pos = lax.broadcasted_iota(jnp.int32, (t, LANES), 0)
    feat_lane = lax.broadcasted_iota(jnp.int32, (t, LANES), 1)
    feat = jnp.where(feat_lane < 3, pos // tq, jnp.where(feat_lane < 6, pos % tq, 0))
    kaug_ref[:, 0:hd] = k_ref[0]
    kaug_ref[:, hd:] = feat.astype(F32).astype(BF16)
    vaug_ref[:, 0:DIFF_V_DIM] = v_ref[0]
    vaug_ref[:, DIFF_V_DIM:] = jnp.ones((t, LANES), BF16)
    lane = lax.broadcasted_iota(jnp.int32, (tq, hd), 1)
    coef_rows = jnp.broadcast_to(coef_ref[0], (2 * tq, LANES))
    for qb in range(t // tq):
        q = q_ref[0, qb * tq:(qb + 1) * tq, :]
        zero = jnp.zeros_like(q)
        qq_ref[qb * 2 * tq:qb * 2 * tq + tq, 0:hd] = jnp.where(lane < DIFF_HEAD_DIM, q, zero)
        qq_ref[qb * 2 * tq + tq:(qb + 1) * 2 * tq, 0:hd] = jnp.where(lane < DIFF_HEAD_DIM, zero, q)
        qq_ref[qb * 2 * tq:(qb + 1) * 2 * tq, hd:] = coef_rows
    q_row = lax.broadcasted_iota(jnp.int32, (tq, tq), 0)
    k_col = lax.broadcasted_iota(jnp.int32, (tq, tq), 1)
    mask_ref[0] = jnp.zeros((tq, tq), F32)
    mask_ref[1] = jnp.where(k_col <= q_row, 0.0, NEG_BIG)
    m_ref[...] = jnp.full(m_ref.shape, NEG_BIG, F32)
    acc_ref[...] = jnp.zeros_like(acc_ref)

    def stage_qk(p, slot):
        q0 = pl.multiple_of(pi_ref[p] * (2 * tq), 2 * tq)
        k0 = pl.multiple_of(pj_ref[p] * tq, tq)
        s_ref[slot] = lax.dot_general(qq_ref[pl.ds(q0, 2 * tq), :], kaug_ref[pl.ds(k0, tq), :],
                                      (((1,), (1,)), ((), ())), preferred_element_type=F32)

    def stage_sm(p, slot):
        i = pi_ref[p]
        diag = (i == pj_ref[p]).astype(jnp.int32)
        for r in range(n_strips):
            rows = slice(r * ATT_STRIP, (r + 1) * ATT_STRIP)
            mrows = slice((r * ATT_STRIP) % tq, (r * ATT_STRIP) % tq + ATT_STRIP)
            s = s_ref[slot, rows, :] + mask_ref[diag, mrows, :]
            m_prev = m_ref[i, rows, :]
            m_next = jnp.maximum(m_prev, jnp.max(s, axis=-1, keepdims=True))
            p_blk = jnp.exp2(s - jnp.tile(m_next, (1, tq // LANES)))
            p_ref[slot, rows, :] = p_blk.astype(BF16)
            c_ref[slot, rows, :] = jnp.exp2(m_prev - m_next)
            m_ref[i, rows, :] = m_next

    def stage_pv(p, slot):
        i = pi_ref[p]
        k0 = pl.multiple_of(pj_ref[p] * tq, tq)
        pv = jnp.dot(p_ref[slot], vaug_ref[pl.ds(k0, tq), :], preferred_element_type=F32)
        acc_ref[i] = acc_ref[i] * jnp.tile(c_ref[slot], (1, 2)) + pv

    stage_qk(0, 0)
    stage_qk(1, 1)
    stage_sm(0, 0)

    def pipeline_step(step, c):
        for u in range(unroll):
            p = unroll * step + u
            stage_qk(p + 2, u % 2)
            stage_sm(p + 1, (u + 1) % 2)
            stage_pv(p, u % 2)
        return c

    lax.fori_loop(0, n_pairs // unroll, pipeline_step, 0)

    lv = lamv_ref[...]
    lam = (jnp.exp(jnp.sum(lv[0:1] * lv[1:2], axis=-1, keepdims=True))
           - jnp.exp(jnp.sum(lv[2:3] * lv[3:4], axis=-1, keepdims=True)) + lam_init)

    def finalize(i, c):
        o = acc_ref[i, :, 0:DIFF_V_DIM] / acc_ref[i, :, DIFF_V_DIM:]
        o = o[:tq] - lam * o[tq:]
        o = o * lax.rsqrt(jnp.mean(o * o, axis=-1, keepdims=True) + LN_EPS) * sub_ref[...]
        r0 = pl.multiple_of(i * tq, tq)
        o_ref[0, pl.ds(r0, tq), :] = (o * (1.0 - lam_init)).astype(o_ref.dtype)
        return c

    lax.fori_loop(0, t // tq, finalize, 0)


def _diffattn(q, k, v, lam_vecs, subln_g, lam_init, tq):
    bsz, t, _ = q.shape
    slopes = jnp.exp2(-8.0 * (jnp.arange(DIFF_HEADS, dtype=F32) + 1.0) / DIFF_HEADS) * LOG2E
    def pieces(c):
        c1 = c.astype(BF16)
        c2 = (c - c1.astype(F32)).astype(BF16)
        c3 = (c - c1.astype(F32) - c2.astype(F32)).astype(BF16)
        return [c1, c2, c3]

    coef = jnp.zeros((DIFF_HEADS, 1, LANES), BF16)
    coef = coef.at[:, 0, 0:6].set(jnp.stack(pieces(slopes * tq) + pieces(slopes), axis=1))
    hd = 2 * DIFF_HEAD_DIM
    nq = t // tq
    assert nq <= 256, "key block indices must be exact in bf16"
    pairs = [(i, j) for i in range(nq) for j in range(i + 1)]
    n_pairs = len(pairs)
    unroll = 4 if n_pairs % 4 == 0 else 2
    assert n_pairs % unroll == 0
    pairs = pairs + [pairs[-1]] * 2
    pair_i = jnp.asarray([p[0] for p in pairs], jnp.int32)
    pair_j = jnp.asarray([p[1] for p in pairs], jnp.int32)
    head = lambda shape: pl.BlockSpec(shape, lambda b, h, pi, pj: (b, 0, h))
    return pl.pallas_call(
        functools.partial(_diffattn_body, tq=tq, n_pairs=n_pairs, unroll=unroll, lam_init=lam_init),
        grid_spec=pltpu.PrefetchScalarGridSpec(
            num_scalar_prefetch=2,
            grid=(bsz, DIFF_HEADS),
            in_specs=[pl.BlockSpec(lam_vecs.shape, lambda b, h, pi, pj: (0, 0)),
                      pl.BlockSpec((1, DIFF_V_DIM), lambda b, h, pi, pj: (0, 0)),
                      pl.BlockSpec((1, 1, LANES), lambda b, h, pi, pj: (h, 0, 0)),
                      head((1, t, hd)), head((1, t, hd)), head((1, t, DIFF_V_DIM))],
            out_specs=head((1, t, DIFF_V_DIM)),
            scratch_shapes=[pltpu.VMEM((t, hd + LANES), BF16),
                            pltpu.VMEM((t, DIFF_V_DIM + LANES), BF16),
                            pltpu.VMEM((2 * t, hd + LANES), BF16),
                            pltpu.VMEM((2, tq, tq), F32),
                            pltpu.VMEM((2, 2 * tq, tq), F32),
                            pltpu.VMEM((2, 2 * tq, tq), BF16),
                            pltpu.VMEM((2, 2 * tq, LANES), F32),
                            pltpu.VMEM((nq, 2 * tq, LANES), F32),
                            pltpu.VMEM((nq, 2 * tq, DIFF_V_DIM + LANES), F32)]),
        out_shape=jax.ShapeDtypeStruct((bsz, t, DIFF_V_W), BF16),
        compiler_params=_params(("parallel", "parallel")),
        name="diffattn",
    )(pair_i, pair_j, lam_vecs, subln_g.reshape(1, -1), coef, q, k, v)


def _router_body(h_ref, w_ref, o_ref, cnt_ref, carry_ref, *, tm):
    @pl.when(pl.program_id(0) == 0)
    def _():
        carry_ref[...] = jnp.zeros_like(carry_ref)

    logits = jnp.dot(h_ref[...], w_ref[...], preferred_element_type=F32,
                     precision=lax.Precision.HIGHEST)
    lane = lax.broadcasted_iota(jnp.int32, (tm, LANES), 1)
    lg = jnp.where(lane < N_EXPERTS, logits, -jnp.inf)
    v1 = jnp.max(lg, axis=-1, keepdims=True)
    i1 = jnp.min(jnp.where(lg == v1, lane, LANES), axis=-1, keepdims=True)
    oh1 = lane == i1
    lg2 = jnp.where(oh1, -jnp.inf, lg)
    v2 = jnp.max(lg2, axis=-1, keepdims=True)
    i2 = jnp.min(jnp.where(lg2 == v2, lane, LANES), axis=-1, keepdims=True)
    oh2 = lane == i2
    e2 = jnp.exp(v2 - v1)
    g1 = 1.0 / (1.0 + e2)
    g2 = e2 / (1.0 + e2)

    both = jnp.where(oh1, 1.0, jnp.where(oh2, 1.0, 0.0))
    tri = (lax.broadcasted_iota(jnp.int32, (tm, tm), 0)
           > lax.broadcasted_iota(jnp.int32, (tm, tm), 1))
    tri = jnp.where(tri, 1.0, 0.0).astype(BF16)
    before = jnp.dot(tri, both.astype(BF16), preferred_element_type=F32) + carry_ref[0:1, :]
    rank1 = jnp.sum(jnp.where(oh1, before, 0.0), axis=-1, keepdims=True)
    rank2 = jnp.sum(jnp.where(oh2, before, 0.0), axis=-1, keepdims=True)
    total = carry_ref[...] + jnp.sum(both, axis=0, keepdims=True)
    carry_ref[...] = total
    cnt_ref[...] = total

    out = jnp.where(lane == 0, i1.astype(F32), 0.0)
    out = jnp.where(lane == 1, i2.astype(F32), out)
    out = jnp.where(lane == 2, rank1, out)
    out = jnp.where(lane == 3, rank2, out)
    out = jnp.where(lane == 4, g1, out)
    out = jnp.where(lane == 5, g2, out)
    o_ref[...] = out


def _router(h, w_router, tm):
    m, d = h.shape
    w = jnp.zeros((d, LANES), F32).at[:, :N_EXPERTS].set(w_router)
    return pl.pallas_call(
        functools.partial(_router_body, tm=tm),
        grid=(m // tm,),
        in_specs=[pl.BlockSpec((tm, d), lambda i: (i, 0)),
                  pl.BlockSpec((d, LANES), lambda i: (0, 0))],
        out_specs=[pl.BlockSpec((tm, LANES), lambda i: (i, 0)),
                   pl.BlockSpec((SUBLANES, LANES), lambda i: (0, 0))],
        out_shape=[jax.ShapeDtypeStruct((m, LANES), F32),
                   jax.ShapeDtypeStruct((SUBLANES, LANES), F32)],
        scratch_shapes=[pltpu.VMEM((SUBLANES, LANES), F32)],
        compiler_params=_params(("arbitrary",)),
        name="router",
    )(h, w)


def _dispatch_body(dest_ref, meta_ref, h_ref, buf_ref, zero_ref, sem, *, tm, n_tok, n_rows):
    i = pl.program_id(0)
    base = i * tm

    def row_copy(r, d):
        return pltpu.make_async_copy(h_ref.at[pl.ds(r, 1), :], buf_ref.at[pl.ds(d, 1), :], sem)

    def issue(r, c):
        row_copy(r, dest_ref[base + r]).start()
        row_copy(r, dest_ref[n_tok + base + r]).start()
        return c

    lax.fori_loop(0, tm, issue, 0, unroll=8)
    all_rows = buf_ref.at[pl.ds(0, TOP_K * tm), :]
    pltpu.make_async_copy(all_rows, all_rows, sem).wait()

    @pl.when(i == pl.num_programs(0) - 1)
    def _():
        zero_ref[...] = jnp.zeros_like(zero_ref)

        def zero_row(r):
            return pltpu.make_async_copy(zero_ref.at[pl.ds(0, 1), :], buf_ref.at[pl.ds(r, 1), :], sem)

        def start_zero_row(r, c):
            zero_row(r).start()
            return c

        def wait_zero_row(r, c):
            zero_row(0).wait()
            return c

        for e in range(N_EXPERTS):
            lo = meta_ref[e] + meta_ref[N_EXPERTS + e]
            hi = meta_ref[e] + meta_ref[2 * N_EXPERTS + e]
            lax.fori_loop(lo, hi, start_zero_row, 0)
            lax.fori_loop(lo, hi, wait_zero_row, 0)

        def zero_block(b, c):
            r0 = pl.multiple_of(b * MOE_ROWS, MOE_ROWS)
            cp = pltpu.make_async_copy(zero_ref, buf_ref.at[pl.ds(r0, MOE_ROWS), :], sem)
            cp.start()
            cp.wait()
            return c

        used = meta_ref[N_EXPERTS - 1] + meta_ref[3 * N_EXPERTS - 1]
        lax.fori_loop(used // MOE_ROWS, n_rows // MOE_ROWS, zero_block, 0)


def _dispatch(dest, meta, h, n_rows, tm):
    m, d = h.shape
    return pl.pallas_call(
        functools.partial(_dispatch_body, tm=tm, n_tok=m, n_rows=n_rows),
        grid_spec=pltpu.PrefetchScalarGridSpec(
            num_scalar_prefetch=2,
            grid=(m // tm,),
            in_specs=[pl.BlockSpec((tm, d), lambda i, dest, meta: (i, 0))],
            out_specs=pl.BlockSpec(memory_space=pl.ANY),
            scratch_shapes=[pltpu.VMEM((MOE_ROWS, d), F32), pltpu.SemaphoreType.DMA(())]),
        out_shape=jax.ShapeDtypeStruct((n_rows, d), F32),
        compiler_params=_params(("arbitrary",)),
        name="moe_dispatch",
    )(dest, meta, h)


def _moe_body(be_ref, nv_ref, x_ref, w1_ref, w3_ref, w2_ref, o_ref, xb_ref, acc_ref):
    b = pl.program_id(0)
    f = pl.program_id(1)

    @pl.when(b < nv_ref[0])
    def _():
        @pl.when(f == 0)
        def _():
            xb_ref[...] = x_ref[...].astype(BF16)
            acc_ref[...] = jnp.zeros_like(acc_ref)

        xb = xb_ref[...]
        gate = jnp.dot(xb, w1_ref[0], preferred_element_type=F32)
        up = jnp.dot(xb, w3_ref[0], preferred_element_type=F32)
        act = (jax.nn.silu(gate) * up).astype(BF16)
        acc_ref[...] += jnp.dot(act, w2_ref[0], preferred_element_type=F32)

        @pl.when(f == pl.num_programs(1) - 1)
        def _():
            o_ref[...] = acc_ref[...]

    @pl.when(jnp.logical_and(b >= nv_ref[0], f == pl.num_programs(1) - 1))
    def _():
        o_ref[...] = jnp.zeros_like(o_ref)


def _moe(block_e, n_valid, buf, w13, w2):
    n_rows, d = buf.shape
    nb = n_rows // MOE_ROWS
    nf = FFN_DIM // FFN_TILE

    def blk(b, nv):
        return jnp.minimum(b, nv[0] - 1)

    def ftile(b, f, nv):
        return jnp.where(b < nv[0], f, nf - 1)

    return pl.pallas_call(
        _moe_body,
        grid_spec=pltpu.PrefetchScalarGridSpec(
            num_scalar_prefetch=2,
            grid=(nb, nf),
            in_specs=[pl.BlockSpec((MOE_ROWS, d), lambda b, f, be, nv: (blk(b, nv), 0)),
                      pl.BlockSpec((1, d, FFN_TILE),
                                   lambda b, f, be, nv: (be[blk(b, nv)], 0, ftile(b, f, nv))),
                      pl.BlockSpec((1, d, FFN_TILE),
                                   lambda b, f, be, nv: (be[blk(b, nv)], 0, nf + ftile(b, f, nv))),
                      pl.BlockSpec((1, FFN_TILE, d),
                                   lambda b, f, be, nv: (be[blk(b, nv)], ftile(b, f, nv), 0))],
            out_specs=pl.BlockSpec((MOE_ROWS, d), lambda b, f, be, nv: (b, 0)),
            scratch_shapes=[pltpu.VMEM((MOE_ROWS, d), BF16), pltpu.VMEM((MOE_ROWS, d), F32)]),
        out_shape=jax.ShapeDtypeStruct((n_rows, d), F32),
        compiler_params=_params(("arbitrary", "arbitrary")),
        name="moe_experts",
    )(block_e, n_valid, buf, w13, w13, w2)


def _combine_ln_body(dest_ref, h_ref, route_ref, g_ref, b_ref, yb_ref, o_ref, rows_ref, sem,
                     *, tm, n_tok):
    i = pl.program_id(0)

    def gather_tile(tile, buf):
        base = tile * tm

        def issue(r, c):
            for k in range(TOP_K):
                pltpu.make_async_copy(yb_ref.at[pl.ds(dest_ref[k * n_tok + base + r], 1), :],
                                      rows_ref.at[buf, k, pl.ds(r, 1), :], sem.at[buf]).start()
            return c

        lax.fori_loop(0, tm, issue, 0, unroll=8)

    @pl.when(i == 0)
    def _():
        gather_tile(0, 0)

    @pl.when(i + 1 < pl.num_programs(0))
    def _():
        gather_tile(i + 1, (i + 1) % 2)

    buf = i % 2
    pltpu.make_async_copy(rows_ref.at[buf], rows_ref.at[buf], sem.at[buf]).wait()
    route = route_ref[...]
    y = route[:, 4:5] * rows_ref[buf, 0] + route[:, 5:6] * rows_ref[buf, 1]
    z = DEEPNORM_ALPHA * h_ref[...] + y
    o_ref[...] = _layer_norm(z, g_ref[...], b_ref[...])


def _combine_ln(dest, h, route, yb, g, b, tm):
    m, d = h.shape
    return pl.pallas_call(
        functools.partial(_combine_ln_body, tm=tm, n_tok=m),
        grid_spec=pltpu.PrefetchScalarGridSpec(
            num_scalar_prefetch=1,
            grid=(m // tm,),
            in_specs=[pl.BlockSpec((tm, d), lambda i, dest: (i, 0)),
                      pl.BlockSpec((tm, LANES), lambda i, dest: (i, 0)),
                      pl.BlockSpec((1, d), lambda i, dest: (0, 0)),
                      pl.BlockSpec((1, d), lambda i, dest: (0, 0)),
                      pl.BlockSpec(memory_space=pl.ANY)],
            out_specs=pl.BlockSpec((tm, d), lambda i, dest: (i, 0)),
            scratch_shapes=[pltpu.VMEM((2, TOP_K, tm, d), F32), pltpu.SemaphoreType.DMA((2,))]),
        out_shape=jax.ShapeDtypeStruct((m, d), F32),
        compiler_params=_params(("arbitrary",)),
        name="moe_combine_ln",
    )(dest, h, route, g.reshape(1, -1), b.reshape(1, -1), yb)


def _moe_layer(h, w_router, w13, w2, g, b):
    n_tok, _ = h.shape
    route, cnt = _router(h, w_router, tm=512)
    counts = cnt[0, :N_EXPERTS].astype(jnp.int32)
    padded = (counts + MOE_ROWS - 1) // MOE_ROWS * MOE_ROWS
    pstart = jnp.cumsum(padded) - padded
    experts = route[:, 0:2].astype(jnp.int32)
    ranks = route[:, 2:4].astype(jnp.int32)
    dest = (pstart[experts] + ranks).T.reshape(-1)
    n_rows = n_tok * TOP_K + N_EXPERTS * MOE_ROWS
    n_blocks = n_rows // MOE_ROWS
    block_end = jnp.cumsum(padded // MOE_ROWS)
    block_e = jnp.sum(jnp.arange(n_blocks)[:, None] >= block_end[None, :], axis=1)
    block_e = jnp.minimum(block_e, N_EXPERTS - 1).astype(jnp.int32)
    n_valid = block_end[-1:].astype(jnp.int32)
    meta = jnp.concatenate([pstart, counts, padded]).astype(jnp.int32)
    buf = _dispatch(dest, meta, h, n_rows, tm=512)
    yb = _moe(block_e, n_valid, buf, w13, w2)
    return _combine_ln(dest, h, route, yb, g, b, tm=512)


def kernel(x, mem, a_w_in, a_conv_w, a_conv_b, a_w_rgate, a_b_rgate, a_w_igate, a_b_igate, a_lambda,
           a_w_out, w_kv_shared, b_w_q, b_lambda, b_subln_g, b_w_out, mem_w_kv, ffn_w13, ffn_w2,
           moe_router, moe_w13, moe_w2, ln_g, ln_b):
    bsz, t, d = x.shape
    n_tok = bsz * t
    n_mem = mem.shape[1]
    bf = lambda a: a.astype(BF16)

    kv_mem = _proj(mem.reshape(bsz * n_mem, d), bf(jnp.concatenate([mem_w_kv[0], mem_w_kv[1]], axis=1)),
                   [(0, 2 * MEM_W), (2 * MEM_W, 4 * MEM_W)], [None, None], [BF16, BF16],
                   tm=min(1024, bsz * n_mem), name="mem_kv")
    kv_mem = [a.reshape(bsz, n_mem, 2 * MEM_W) for a in kv_mem]

    x2 = x.reshape(n_tok, d)
    gate, u_pre, q_mem = _proj(
        x2, bf(a_w_in[0]), [(0, D_RNN), (D_RNN, 2 * D_RNN), (2 * D_RNN, 2 * D_RNN + MEM_W)],
        [jax.nn.gelu, None, None], [BF16, BF16, BF16], tm=512, name="proj_in")
    w_ri = bf(jnp.concatenate([a_w_rgate[0], a_w_igate[0]], axis=-1))
    rnn = _rglru(u_pre.reshape(bsz, t, D_RNN), gate.reshape(bsz, t, D_RNN), a_conv_w[0], a_conv_b[0],
                 w_ri, a_b_rgate[0], a_b_igate[0], a_lambda[0], tt=512)
    h = _outproj_ln(rnn, q_mem.reshape(bsz, t, MEM_W), kv_mem[0], bf(a_w_out[0]), x,
                    ln_g[0, 0], ln_b[0, 0], tm=512, name="outproj_ln_a")
    h = _ffn_ln(h.reshape(n_tok, d), bf(ffn_w13[0]), bf(ffn_w2[0]), ln_g[0, 1], ln_b[0, 1], tm=512)

    layer = 1
    lam_init = 0.8 - 0.6 * math.exp(-0.3 * layer)
    w_cat = bf(jnp.concatenate([w_kv_shared, b_w_q[0]], axis=1))
    scale = DIFF_HEAD_DIM ** -0.5 * LOG2E
    k_sh, v_sh, q_diff, q_mem = _proj(
        h, w_cat, [(0, DIFF_QK_W), (DIFF_QK_W, DIFF_QK_W + DIFF_V_W),
                   (DIFF_QK_W + DIFF_V_W, 2 * DIFF_QK_W + DIFF_V_W),
                   (2 * DIFF_QK_W + DIFF_V_W, 2 * DIFF_QK_W + DIFF_V_W + MEM_W)],
        [None, None, lambda z: z * scale, None], [BF16, BF16, BF16, BF16], tm=512, name="proj_kvq")
    attn = _diffattn(q_diff.reshape(bsz, t, DIFF_QK_W), k_sh.reshape(bsz, t, DIFF_QK_W),
                     v_sh.reshape(bsz, t, DIFF_V_W), b_lambda[0], b_subln_g[0], lam_init, tq=256)
    h = _outproj_ln(attn, q_mem.reshape(bsz, t, MEM_W), kv_mem[1], bf(b_w_out[0]),
                    h.reshape(bsz, t, d), ln_g[1, 0], ln_b[1, 0], tm=512, name="outproj_ln_b")
    out = _moe_layer(h.reshape(n_tok, d), moe_router[0], bf(moe_w13[0]), bf(moe_w2[0]),
                     ln_g[1, 1], ln_b[1, 1])
    return out.reshape(bsz, t, d)
```

```python
import functools
import math

import jax
import jax.numpy as jnp
from jax import lax
from jax.experimental import pallas as pl
from jax.experimental.pallas import tpu as pltpu

F32 = jnp.float32
BF16 = jnp.bfloat16

D_MODEL = 1024
DEPTH = 2
D_RNN = D_MODEL
RNN_BLOCKS = 8
RNN_BLOCK_W = D_RNN // RNN_BLOCKS
CONV_W = 4
LRU_C = 8.0
MEM_HEADS = 4
MEM_HEAD_DIM = D_MODEL // 8
MEM_W = MEM_HEADS * MEM_HEAD_DIM
DIFF_HEADS = 8
DIFF_HEAD_DIM = D_MODEL // 16
DIFF_V_DIM = 2 * DIFF_HEAD_DIM
DIFF_QK_W = DIFF_HEADS * 2 * DIFF_HEAD_DIM
DIFF_V_W = DIFF_HEADS * DIFF_V_DIM
FFN_DIM = (7 * D_MODEL) // 2
N_EXPERTS = 8
TOP_K = 2
LN_EPS = 1e-5
DEEPNORM_ALPHA = (2.0 * DEPTH) ** 0.25

LANES = 128
SUBLANES = 8
VMEM_LIMIT = 52 * 1024 * 1024

FFN_TILE = 1792
MOE_ROWS = 512
NEG_BIG = -1e30
LOG2E = math.log2(math.e)


def _params(semantics):
    return pltpu.CompilerParams(dimension_semantics=semantics, vmem_limit_bytes=VMEM_LIMIT)


def _layer_norm(z, g, b):
    mu = jnp.mean(z, axis=-1, keepdims=True)
    zc = z - mu
    var = jnp.mean(zc * zc, axis=-1, keepdims=True)
    return zc * lax.rsqrt(var + LN_EPS) * g + b


def _proj_body(x_ref, w_ref, *o_refs, splits, post):
    xb = x_ref[...].astype(BF16)
    for o_ref, (c0, c1), fn in zip(o_refs, splits, post):
        z = jnp.dot(xb, w_ref[:, c0:c1], preferred_element_type=F32)
        if fn is not None:
            z = fn(z)
        o_ref[...] = z.astype(o_ref.dtype)


def _proj(x, w, splits, post, out_dtypes, tm, name):
    m, k = x.shape
    return pl.pallas_call(
        functools.partial(_proj_body, splits=tuple(splits), post=tuple(post)),
        grid=(m // tm,),
        in_specs=[pl.BlockSpec((tm, k), lambda i: (i, 0)),
                  pl.BlockSpec(w.shape, lambda i: (0, 0))],
        out_specs=[pl.BlockSpec((tm, c1 - c0), lambda i: (i, 0)) for c0, c1 in splits],
        out_shape=[jax.ShapeDtypeStruct((m, c1 - c0), dt) for (c0, c1), dt in zip(splits, out_dtypes)],
        compiler_params=_params(("parallel",)),
        name=name,
    )(x, w)


def _proj_kvq_body(x_ref, w_ref, wvt_ref, k_ref, vt_ref, qd_ref, qm_ref):
    xb = x_ref[...].astype(BF16)
    q_scale = DIFF_HEAD_DIM ** -0.5 * LOG2E
    k_ref[...] = jnp.dot(xb, w_ref[:, 0:DIFF_QK_W], preferred_element_type=F32).astype(BF16)
    vt_ref[0] = lax.dot_general(wvt_ref[...], xb, (((1,), (1,)), ((), ())),
                                preferred_element_type=F32).astype(BF16)
    qd = jnp.dot(xb, w_ref[:, DIFF_QK_W:2 * DIFF_QK_W], preferred_element_type=F32)
    qd_ref[...] = (qd * q_scale).astype(BF16)
    qm_ref[...] = jnp.dot(xb, w_ref[:, 2 * DIFF_QK_W:], preferred_element_type=F32).astype(BF16)


def _proj_kvq(h, w_cat, w_v_t, bsz, t, tm):
    m, d = h.shape
    nt = t // tm
    rows = lambda width: pl.BlockSpec((tm, width), lambda i: (i, 0))
    return pl.pallas_call(
        _proj_kvq_body,
        grid=(m // tm,),
        in_specs=[rows(d), pl.BlockSpec(w_cat.shape, lambda i: (0, 0)),
                  pl.BlockSpec(w_v_t.shape, lambda i: (0, 0))],
        out_specs=[rows(DIFF_QK_W), pl.BlockSpec((1, DIFF_V_W, tm), lambda i: (i // nt, 0, i % nt)),
                   rows(DIFF_QK_W), rows(MEM_W)],
        out_shape=[jax.ShapeDtypeStruct((m, DIFF_QK_W), BF16),
                   jax.ShapeDtypeStruct((bsz, DIFF_V_W, t), BF16),
                   jax.ShapeDtypeStruct((m, DIFF_QK_W), BF16),
                   jax.ShapeDtypeStruct((m, MEM_W), BF16)],
        compiler_params=_params(("parallel",)),
        name="proj_kvq",
    )(h, w_cat, w_v_t)


def _rglru_body(u_ref, g_ref, cw_ref, cb_ref, wri_ref, br_ref, bi_ref, lam_ref, o_ref,
                ext_ref, rec_ref, a_ref, b_ref, h_ref, *, tt):
    t = pl.program_id(1)

    @pl.when(t == 0)
    def _():
        ext_ref[0:SUBLANES, :] = jnp.zeros((SUBLANES, D_RNN), F32)
        h_ref[...] = jnp.zeros((SUBLANES, D_RNN), F32)

    ext_ref[SUBLANES:, :] = u_ref[0].astype(F32)
    rec = cb_ref[...] + cw_ref[CONV_W - 1:CONV_W, :] * ext_ref[SUBLANES:SUBLANES + tt, :]
    for j in range(CONV_W - 1):
        off = SUBLANES - (CONV_W - 1) + j
        rec = rec + cw_ref[j:j + 1, :] * ext_ref[off:off + tt, :]
    rec_ref[...] = rec
    ext_ref[0:SUBLANES, :] = ext_ref[tt:tt + SUBLANES, :]

    sp = jax.nn.softplus(-lam_ref[...])
    first_row = (lax.broadcasted_iota(jnp.int32, (tt, RNN_BLOCK_W), 0) + t * tt) == 0
    for n in range(RNN_BLOCKS):
        blk = slice(n * RNN_BLOCK_W, (n + 1) * RNN_BLOCK_W)
        u_n = rec_ref[:, blk]
        ri = jnp.dot(u_n.astype(BF16), wri_ref[n], preferred_element_type=F32)
        r = jax.nn.sigmoid(ri[:, :RNN_BLOCK_W] + br_ref[:, blk])
        i = jax.nn.sigmoid(ri[:, RNN_BLOCK_W:] + bi_ref[:, blk])
        log_a = -LRU_C * r * sp[:, blk]
        a = jnp.exp(log_a)
        th = jnp.tanh(-log_a)
        mult = jnp.sqrt(2.0 * th / (1.0 + th))
        mult = jnp.where(first_row, 1.0, mult)
        a_ref[:, blk] = a
        b_ref[:, blk] = mult * (i * u_n)

    row = lax.broadcasted_iota(jnp.int32, (SUBLANES, D_RNN), 0)

    def tile(i, h_prev):
        r0 = pl.multiple_of(i * SUBLANES, SUBLANES)
        a = a_ref[pl.ds(r0, SUBLANES), :]
        b = b_ref[pl.ds(r0, SUBLANES), :]
        for s in (1, 2, 4):
            a_s = pltpu.roll(a, s, 0)
            b_s = pltpu.roll(b, s, 0)
            keep = row >= s
            b = jnp.where(keep, a * b_s + b, b)
            a = jnp.where(keep, a * a_s, a)
        h = b + a * h_prev
        b_ref[pl.ds(r0, SUBLANES), :] = h
        return jnp.broadcast_to(h[SUBLANES - 1:SUBLANES, :], (SUBLANES, D_RNN))

    h_ref[...] = lax.fori_loop(0, tt // SUBLANES, tile, h_ref[...])
    o_ref[0] = (b_ref[...] * g_ref[0].astype(F32)).astype(o_ref.dtype)


def _rglru(u_pre, gate, conv_w, conv_b, w_ri, b_r, b_i, lam, tt):
    bsz, t, c = u_pre.shape
    row = lambda a: a.reshape(1, c)
    full = lambda shape: pl.BlockSpec(shape, lambda b, i: (0,) * len(shape))
    return pl.pallas_call(
        functools.partial(_rglru_body, tt=tt),
        grid=(bsz, t // tt),
        in_specs=[pl.BlockSpec((1, tt, c), lambda b, i: (b, i, 0)),
                  pl.BlockSpec((1, tt, c), lambda b, i: (b, i, 0)),
                  full((CONV_W, c)), full((1, c)), full(w_ri.shape),
                  full((1, c)), full((1, c)), full((1, c))],
        out_specs=pl.BlockSpec((1, tt, c), lambda b, i: (b, i, 0)),
        out_shape=jax.ShapeDtypeStruct((bsz, t, c), BF16),
        scratch_shapes=[pltpu.VMEM((tt + SUBLANES, c), F32), pltpu.VMEM((tt, c), F32),
                        pltpu.VMEM((tt, c), F32), pltpu.VMEM((tt, c), F32),
                        pltpu.VMEM((SUBLANES, c), F32)],
        compiler_params=_params(("parallel", "arbitrary")),
        name="rglru",
    )(u_pre, gate, conv_w, row(conv_b), w_ri, row(b_r), row(b_i), row(lam))


def _outproj_ln_body(a_ref, qm_ref, kv_ref, w_ref, res_ref, g_ref, b_ref, o_ref):
    mix_w = a_ref.shape[-1]
    y = jnp.dot(a_ref[0], w_ref[0:mix_w, :], preferred_element_type=F32)
    scale = MEM_HEAD_DIM ** -0.5
    heads = []
    for h in range(MEM_HEADS):
        hs = slice(h * MEM_HEAD_DIM, (h + 1) * MEM_HEAD_DIM)
        vs = slice(MEM_W + h * MEM_HEAD_DIM, MEM_W + (h + 1) * MEM_HEAD_DIM)
        s = lax.dot_general(qm_ref[0, :, hs], kv_ref[0, :, hs], (((1,), (1,)), ((), ())),
                            preferred_element_type=F32) * scale
        e = jnp.exp(s - jnp.max(s, axis=-1, keepdims=True))
        p = e / jnp.sum(e, axis=-1, keepdims=True)
        heads.append(jnp.dot(p.astype(BF16), kv_ref[0, :, vs], preferred_element_type=F32))
    mem_out = jnp.concatenate(heads, axis=1).astype(BF16)
    y = y + jnp.dot(mem_out, w_ref[mix_w:, :], preferred_element_type=F32)
    z = DEEPNORM_ALPHA * res_ref[0] + y
    o_ref[0] = _layer_norm(z, g_ref[...], b_ref[...])


def _outproj_ln(mix, q_mem, kv_mem, w_out, resid, g, b, tm, name):
    bsz, t, mix_w = mix.shape
    n_mem = kv_mem.shape[1]
    return pl.pallas_call(
        _outproj_ln_body,
        grid=(bsz, t // tm),
        in_specs=[pl.BlockSpec((1, tm, mix_w), lambda b, i: (b, i, 0)),
                  pl.BlockSpec((1, tm, MEM_W), lambda b, i: (b, i, 0)),
                  pl.BlockSpec((1, n_mem, 2 * MEM_W), lambda b, i: (b, 0, 0)),
                  pl.BlockSpec(w_out.shape, lambda b, i: (0, 0)),
                  pl.BlockSpec((1, tm, D_MODEL), lambda b, i: (b, i, 0)),
                  pl.BlockSpec((1, D_MODEL), lambda b, i: (0, 0)),
                  pl.BlockSpec((1, D_MODEL), lambda b, i: (0, 0))],
        out_specs=pl.BlockSpec((1, tm, D_MODEL), lambda b, i: (b, i, 0)),
        out_shape=jax.ShapeDtypeStruct((bsz, t, D_MODEL), F32),
        compiler_params=_params(("parallel", "parallel")),
        name=name,
    )(mix, q_mem, kv_mem, w_out, resid, g.reshape(1, -1), b.reshape(1, -1))


def _ffn_ln_body(h_ref, w1_ref, w3_ref, w2_ref, g_ref, b_ref, o_ref, hb_ref, acc_ref):
    f = pl.program_id(1)

    @pl.when(f == 0)
    def _():
        hb_ref[...] = h_ref[...].astype(BF16)
        acc_ref[...] = jnp.zeros_like(acc_ref)

    hb = hb_ref[...]
    gate = jnp.dot(hb, w1_ref[...], preferred_element_type=F32)
    up = jnp.dot(hb, w3_ref[...], preferred_element_type=F32)
    act = (jax.nn.silu(gate) * up).astype(BF16)
    acc_ref[...] += jnp.dot(act, w2_ref[...], preferred_element_type=F32)

    @pl.when(f == pl.num_programs(1) - 1)
    def _():
        z = DEEPNORM_ALPHA * h_ref[...] + acc_ref[...]
        o_ref[...] = _layer_norm(z, g_ref[...], b_ref[...])


def _ffn_ln(h, w13, w2, g, b, tm):
    m, d = h.shape
    nf = FFN_DIM // FFN_TILE
    return pl.pallas_call(
        _ffn_ln_body,
        grid=(m // tm, nf),
        in_specs=[pl.BlockSpec((tm, d), lambda i, f: (i, 0)),
                  pl.BlockSpec((d, FFN_TILE), lambda i, f: (0, f)),
                  pl.BlockSpec((d, FFN_TILE), lambda i, f: (0, nf + f)),
                  pl.BlockSpec((FFN_TILE, d), lambda i, f: (f, 0)),
                  pl.BlockSpec((1, d), lambda i, f: (0, 0)),
                  pl.BlockSpec((1, d), lambda i, f: (0, 0))],
        out_specs=pl.BlockSpec((tm, d), lambda i, f: (i, 0)),
        out_shape=jax.ShapeDtypeStruct((m, d), F32),
        scratch_shapes=[pltpu.VMEM((tm, d), BF16), pltpu.VMEM((tm, d), F32)],
        compiler_params=_params(("parallel", "arbitrary")),
        name="ffn_ln",
    )(h, w13, w13, w2, g.reshape(1, -1), b.reshape(1, -1))


ATT_ONES_ROWS = 16
ATT_V_ROWS = DIFF_V_DIM + ATT_ONES_ROWS


def _diffattn_body(pi_ref, pj_ref, lamv_ref, sub_ref, coef_ref, q_ref, k_ref, vt_ref, o_ref,
                   kaug_ref, vaug_ref, qq_ref, mask_ref, s_ref, p_ref, c_ref, m_ref, acc_ref,
                   *, tq, n_pairs, unroll, lam_init):
    t = k_ref.shape[1]
    hd = 2 * DIFF_HEAD_DIM

    pos = lax.broadcasted_iota(jnp.int32, (t, LANES), 0)
    feat_lane = lax.broadcasted_iota(jnp.int32, (t, LANES), 1)
    feat = jnp.where(feat_lane < 3, pos // tq, jnp.where(feat_lane < 6, pos % tq, 0))
    kaug_ref[:, 0:hd] = k_ref[0]
    kaug_ref[:, hd:] = feat.astype(F32).astype(BF16)
    for kb in range(t // tq):
        vaug_ref[kb, 0:DIFF_V_DIM, :] = vt_ref[0, :, kb * tq:(kb + 1) * tq]
        vaug_ref[kb, DIFF_V_DIM:, :] = jnp.ones((ATT_ONES_ROWS, tq), BF16)
    lane = lax.broadcasted_iota(jnp.int32, (tq, hd), 1)
    coef_rows = jnp.broadcast_to(coef_ref[0], (2 * tq, LANES))
    for qb in range(t // tq):
        q = q_ref[0, qb * tq:(qb + 1) * tq, :]
        zero = jnp.zeros_like(q)
        qq_ref[qb * 2 * tq:qb * 2 * tq + tq, 0:hd] = jnp.where(lane < DIFF_HEAD_DIM, q, zero)
        qq_ref[qb * 2 * tq + tq:(qb + 1) * 2 * tq, 0:hd] = jnp.where(lane < DIFF_HEAD_DIM, zero, q)
        qq_ref[qb * 2 * tq:(qb + 1) * 2 * tq, hd:] = coef_rows
    k_row = lax.broadcasted_iota(jnp.int32, (tq, 2 * tq), 0)
    q_col = lax.broadcasted_iota(jnp.int32, (tq, 2 * tq), 1)
    q_col = jnp.where(q_col >= tq, q_col - tq, q_col)
    mask_ref[0] = jnp.zeros((tq, 2 * tq), F32)
    mask_ref[1] = jnp.where(k_row <= q_col, 0.0, NEG_BIG)
    m_ref[...] = jnp.full(m_ref.shape, NEG_BIG, F32)
    acc_ref[...] = jnp.zeros_like(acc_ref)

    def stage_qk(p, slot):
        q0 = pl.multiple_of(pi_ref[p] * (2 * tq), 2 * tq)
        k0 = pl.multiple_of(pj_ref[p] * tq, tq)
        s_ref[slot] = lax.dot_general(kaug_ref[pl.ds(k0, tq), :], qq_ref[pl.ds(q0, 2 * tq), :],
                                      (((1,), (1,)), ((), ())), preferred_element_type=F32)

    def stage_sm(p, slot):
        i = pi_ref[p]
        diag = (i == pj_ref[p]).astype(jnp.int32)
        for r in range(2 * tq // LANES):
            cols = slice(r * LANES, (r + 1) * LANES)
            s = s_ref[slot, :, cols] + mask_ref[diag, :, cols]
            m_prev = m_ref[i, :, cols]
            m_next = jnp.maximum(m_prev, jnp.max(s, axis=0, keepdims=True))
            p_blk = jnp.exp2(s - jnp.tile(m_next, (tq // SUBLANES, 1)))
            p_ref[slot, :, cols] = p_blk.astype(BF16)
            c_ref[slot, :, cols] = jnp.exp2(m_prev - m_next)
            m_ref[i, :, cols] = m_next

    def stage_pv(p, slot):
        i = pi_ref[p]
        pv = jnp.dot(vaug_ref[pj_ref[p]], p_ref[slot], preferred_element_type=F32)
        acc_ref[i] = acc_ref[i] * jnp.tile(c_ref[slot], (ATT_V_ROWS // SUBLANES, 1)) + pv

    stage_qk(0, 0)
    stage_qk(1, 1)
    stage_sm(0, 0)

    def pipeline_step(step, c):
        for u in range(unroll):
            p = unroll * step + u
            stage_qk(p + 2, u % 2)
            stage_sm(p + 1, (u + 1) % 2)
            stage_pv(p, u % 2)
        return c

    lax.fori_loop(0, n_pairs // unroll, pipeline_step, 0)

    lv = lamv_ref[...]
    lam = (jnp.exp(jnp.sum(lv[0:1] * lv[1:2], axis=-1, keepdims=True))
           - jnp.exp(jnp.sum(lv[2:3] * lv[3:4], axis=-1, keepdims=True)) + lam_init)

    def finalize(i, c):
        den = acc_ref[i, DIFF_V_DIM:DIFF_V_DIM + SUBLANES, :]
        o = acc_ref[i, 0:DIFF_V_DIM, :] / jnp.tile(den, (DIFF_V_DIM // SUBLANES, 1))
        o = o[:, :tq] - lam * o[:, tq:]
        o = o * lax.rsqrt(jnp.mean(o * o, axis=0, keepdims=True) + LN_EPS) * sub_ref[...]
        r0 = pl.multiple_of(i * tq, tq)
        o_ref[0, pl.ds(r0, tq), :] = (o * (1.0 - lam_init)).T.astype(o_ref.dtype)
        return c

    lax.fori_loop(0, t // tq, finalize, 0)


def _diffattn(q, k, v_t, lam_vecs, subln_g, lam_init, tq):
    bsz, t, _ = q.shape
    slopes = jnp.exp2(-8.0 * (jnp.arange(DIFF_HEADS, dtype=F32) + 1.0) / DIFF_HEADS) * LOG2E
    def pieces(c):
        c1 = c.astype(BF16)
        c2 = (c - c1.astype(F32)).astype(BF16)
        c3 = (c - c1.astype(F32) - c2.astype(F32)).astype(BF16)
        return [c1, c2, c3]

    coef = jnp.zeros((DIFF_HEADS, 1, LANES), BF16)
    coef = coef.at[:, 0, 0:6].set(jnp.stack(pieces(slopes * tq) + pieces(slopes), axis=1))
    hd = 2 * DIFF_HEAD_DIM
    nq = t // tq
    assert nq <= 256, "key block indices must be exact in bf16"
    pairs = [(i, j) for i in range(nq) for j in range(i + 1)]
    n_pairs = len(pairs)
    unroll = next(u for u in (8, 4, 2) if n_pairs % u == 0)
    assert n_pairs % unroll == 0
    pairs = pairs + [pairs[-1]] * 2
    pair_i = jnp.asarray([p[0] for p in pairs], jnp.int32)
    pair_j = jnp.asarray([p[1] for p in pairs], jnp.int32)
    head = lambda shape: pl.BlockSpec(shape, lambda b, h, pi, pj: (b, 0, h))
    return pl.pallas_call(
        functools.partial(_diffattn_body, tq=tq, n_pairs=n_pairs, unroll=unroll, lam_init=lam_init),
        grid_spec=pltpu.PrefetchScalarGridSpec(
            num_scalar_prefetch=2,
            grid=(bsz, DIFF_HEADS),
            in_specs=[pl.BlockSpec(lam_vecs.shape, lambda b, h, pi, pj: (0, 0)),
                      pl.BlockSpec((DIFF_V_DIM, 1), lambda b, h, pi, pj: (0, 0)),
                      pl.BlockSpec((1, 1, LANES), lambda b, h, pi, pj: (h, 0, 0)),
                      head((1, t, hd)), head((1, t, hd)),
                      pl.BlockSpec((1, DIFF_V_DIM, t), lambda b, h, pi, pj: (b, h, 0))],
            out_specs=head((1, t, DIFF_V_DIM)),
            scratch_shapes=[pltpu.VMEM((t, hd + LANES), BF16),
                            pltpu.VMEM((nq, ATT_V_ROWS, tq), BF16),
                            pltpu.VMEM((2 * t, hd + LANES), BF16),
                            pltpu.VMEM((2, tq, 2 * tq), F32),
                            pltpu.VMEM((2, tq, 2 * tq), F32),
                            pltpu.VMEM((2, tq, 2 * tq), BF16),
                            pltpu.VMEM((2, SUBLANES, 2 * tq), F32),
                            pltpu.VMEM((nq, SUBLANES, 2 * tq), F32),
                            pltpu.VMEM((nq, ATT_V_ROWS, 2 * tq), F32)]),
        out_shape=jax.ShapeDtypeStruct((bsz, t, DIFF_V_W), BF16),
        compiler_params=_params(("parallel", "parallel")),
        name="diffattn",
    )(pair_i, pair_j, lam_vecs, subln_g.reshape(-1, 1), coef, q, k, v_t)


def _router_body(h_ref, w_ref, o_ref, cnt_ref, carry_ref, *, tm):
    @pl.when(pl.program_id(0) == 0)
    def _():
        carry_ref[...] = jnp.zeros_like(carry_ref)

    logits = jnp.dot(h_ref[...], w_ref[...], preferred_element_type=F32,
                     precision=lax.Precision.HIGHEST)
    lane = lax.broadcasted_iota(jnp.int32, (tm, LANES), 1)
    lg = jnp.where(lane < N_EXPERTS, logits, -jnp.inf)
    v1 = jnp.max(lg, axis=-1, keepdims=True)
    i1 = jnp.min(jnp.where(lg == v1, lane, LANES), axis=-1, keepdims=True)
    oh1 = lane == i1
    lg2 = jnp.where(oh1, -jnp.inf, lg)
    v2 = jnp.max(lg2, axis=-1, keepdims=True)
    i2 = jnp.min(jnp.where(lg2 == v2, lane, LANES), axis=-1, keepdims=True)
    oh2 = lane == i2
    e2 = jnp.exp(v2 - v1)
    g1 = 1.0 / (1.0 + e2)
    g2 = e2 / (1.0 + e2)

    both = jnp.where(oh1, 1.0, jnp.where(oh2, 1.0, 0.0))
    tri = (lax.broadcasted_iota(jnp.int32, (tm, tm), 0)
           > lax.broadcasted_iota(jnp.int32, (tm, tm), 1))
    tri = jnp.where(tri, 1.0, 0.0).astype(BF16)
    before = jnp.dot(tri, both.astype(BF16), preferred_element_type=F32) + carry_ref[0:1, :]
    rank1 = jnp.sum(jnp.where(oh1, before, 0.0), axis=-1, keepdims=True)
    rank2 = jnp.sum(jnp.where(oh2, before, 0.0), axis=-1, keepdims=True)
    total = carry_ref[...] + jnp.sum(both, axis=0, keepdims=True)
    carry_ref[...] = total
    cnt_ref[...] = total

    out = jnp.where(lane == 0, i1.astype(F32), 0.0)
    out = jnp.where(lane == 1, i2.astype(F32), out)
    out = jnp.where(lane == 2, rank1, out)
    out = jnp.where(lane == 3, rank2, out)
    out = jnp.where(lane == 4, g1, out)
    out = jnp.where(lane == 5, g2, out)
    o_ref[...] = out


def _router(h, w_router, tm):
    m, d = h.shape
    w = jnp.zeros((d, LANES), F32).at[:, :N_EXPERTS].set(w_router)
    return pl.pallas_call(
        functools.partial(_router_body, tm=tm),
        grid=(m // tm,),
        in_specs=[pl.BlockSpec((tm, d), lambda i: (i, 0)),
                  pl.BlockSpec((d, LANES), lambda i: (0, 0))],
        out_specs=[pl.BlockSpec((tm, LANES), lambda i: (i, 0)),
                   pl.BlockSpec((SUBLANES, LANES), lambda i: (0, 0))],
        out_shape=[jax.ShapeDtypeStruct((m, LANES), F32),
                   jax.ShapeDtypeStruct((SUBLANES, LANES), F32)],
        scratch_shapes=[pltpu.VMEM((SUBLANES, LANES), F32)],
        compiler_params=_params(("arbitrary",)),
        name="router",
    )(h, w)


def _dispatch_body(dest_ref, meta_ref, h_ref, buf_ref, zero_ref, sem, *, tm, n_tok, n_rows):
    i = pl.program_id(0)
    base = i * tm

    def row_copy(r, d):
        return pltpu.make_async_copy(h_ref.at[pl.ds(r, 1), :], buf_ref.at[pl.ds(d, 1), :], sem)

    def issue(r, c):
        row_copy(r, dest_ref[base + r]).start()
        row_copy(r, dest_ref[n_tok + base + r]).start()
        return c

    lax.fori_loop(0, tm, issue, 0, unroll=8)
    all_rows = buf_ref.at[pl.ds(0, TOP_K * tm), :]
    pltpu.make_async_copy(all_rows, all_rows, sem).wait()

    @pl.when(i == pl.num_programs(0) - 1)
    def _():
        zero_ref[...] = jnp.zeros_like(zero_ref)

        def zero_row(r):
            return pltpu.make_async_copy(zero_ref.at[pl.ds(0, 1), :], buf_ref.at[pl.ds(r, 1), :], sem)

        def start_zero_row(r, c):
            zero_row(r).start()
            return c

        def wait_zero_row(r, c):
            zero_row(0).wait()
            return c

        for e in range(N_EXPERTS):
            lo = meta_ref[e] + meta_ref[N_EXPERTS + e]
            hi = meta_ref[e] + meta_ref[2 * N_EXPERTS + e]
            lax.fori_loop(lo, hi, start_zero_row, 0)
            lax.fori_loop(lo, hi, wait_zero_row, 0)

        def zero_block(b, c):
            r0 = pl.multiple_of(b * MOE_ROWS, MOE_ROWS)
            cp = pltpu.make_async_copy(zero_ref, buf_ref.at[pl.ds(r0, MOE_ROWS), :], sem)
            cp.start()
            cp.wait()
            return c

        used = meta_ref[N_EXPERTS - 1] + meta_ref[3 * N_EXPERTS - 1]
        lax.fori_loop(used // MOE_ROWS, n_rows // MOE_ROWS, zero_block, 0)


def _dispatch(dest, meta, h, n_rows, tm):
    m, d = h.shape
    return pl.pallas_call(
        functools.partial(_dispatch_body, tm=tm, n_tok=m, n_rows=n_rows),
        grid_spec=pltpu.PrefetchScalarGridSpec(
            num_scalar_prefetch=2,
            grid=(m // tm,),
            in_specs=[pl.BlockSpec((tm, d), lambda i, dest, meta: (i, 0))],
            out_specs=pl.BlockSpec(memory_space=pl.ANY),
            scratch_shapes=[pltpu.VMEM((MOE_ROWS, d), F32), pltpu.SemaphoreType.DMA(())]),
        out_shape=jax.ShapeDtypeStruct((n_rows, d), F32),
        compiler_params=_params(("arbitrary",)),
        name="moe_dispatch",
    )(dest, meta, h)


def _moe_body(be_ref, nv_ref, x_ref, w1_ref, w3_ref, w2_ref, o_ref, xb_ref, acc_ref):
    b = pl.program_id(0)
    f = pl.program_id(1)

    @pl.when(b < nv_ref[0])
    def _():
        @pl.when(f == 0)
        def _():
            xb_ref[...] = x_ref[...].astype(BF16)
            acc_ref[...] = jnp.zeros_like(acc_ref)

        xb = xb_ref[...]
        gate = jnp.dot(xb, w1_ref[0], preferred_element_type=F32)
        up = jnp.dot(xb, w3_ref[0], preferred_element_type=F32)
        act = (jax.nn.silu(gate) * up).astype(BF16)
        acc_ref[...] += jnp.dot(act, w2_ref[0], preferred_element_type=F32)

        @pl.when(f == pl.num_programs(1) - 1)
        def _():
            o_ref[...] = acc_ref[...]

    @pl.when(jnp.logical_and(b >= nv_ref[0], f == pl.num_programs(1) - 1))
    def _():
        o_ref[...] = jnp.zeros_like(o_ref)


def _moe(block_e, n_valid, buf, w13, w2):
    n_rows, d = buf.shape
    nb = n_rows // MOE_ROWS
    nf = FFN_DIM // FFN_TILE

    def blk(b, nv):
        return jnp.minimum(b, nv[0] - 1)

    def ftile(b, f, nv):
        return jnp.where(b < nv[0], f, nf - 1)

    return pl.pallas_call(
        _moe_body,
        grid_spec=pltpu.PrefetchScalarGridSpec(
            num_scalar_prefetch=2,
            grid=(nb, nf),
            in_specs=[pl.BlockSpec((MOE_ROWS, d), lambda b, f, be, nv: (blk(b, nv), 0)),
                      pl.BlockSpec((1, d, FFN_TILE),
                                   lambda b, f, be, nv: (be[blk(b, nv)], 0, ftile(b, f, nv))),
                      pl.BlockSpec((1, d, FFN_TILE),
                                   lambda b, f, be, nv: (be[blk(b, nv)], 0, nf + ftile(b, f, nv))),
                      pl.BlockSpec((1, FFN_TILE, d),
                                   lambda b, f, be, nv: (be[blk(b, nv)], ftile(b, f, nv), 0))],
            out_specs=pl.BlockSpec((MOE_ROWS, d), lambda b, f, be, nv: (b, 0)),
            scratch_shapes=[pltpu.VMEM((MOE_ROWS, d), BF16), pltpu.VMEM((MOE_ROWS, d), F32)]),
        out_shape=jax.ShapeDtypeStruct((n_rows, d), F32),
        compiler_params=_params(("arbitrary", "arbitrary")),
        name="moe_experts",
    )(block_e, n_valid, buf, w13, w13, w2)


def _combine_ln_body(dest_ref, h_ref, route_ref, g_ref, b_ref, yb_ref, o_ref, rows_ref, sem,
                     *, tm, n_tok):
    i = pl.program_id(0)

    def gather_tile(tile, buf):
        base = tile * tm

        def issue(r, c):
            for k in range(TOP_K):
                pltpu.make_async_copy(yb_ref.at[pl.ds(dest_ref[k * n_tok + base + r], 1), :],
                                      rows_ref.at[buf, k, pl.ds(r, 1), :], sem.at[buf]).start()
            return c

        lax.fori_loop(0, tm, issue, 0, unroll=8)

    @pl.when(i == 0)
    def _():
        gather_tile(0, 0)

    @pl.when(i + 1 < pl.num_programs(0))
    def _():
        gather_tile(i + 1, (i + 1) % 2)

    buf = i % 2
    pltpu.make_async_copy(rows_ref.at[buf], rows_ref.at[buf], sem.at[buf]).wait()
    route = route_ref[...]
    y = route[:, 4:5] * rows_ref[buf, 0] + route[:, 5:6] * rows_ref[buf, 1]
    z = DEEPNORM_ALPHA * h_ref[...] + y
    o_ref[...] = _layer_norm(z, g_ref[...], b_ref[...])


def _combine_ln(dest, h, route, yb, g, b, tm):
    m, d = h.shape
    return pl.pallas_call(
        functools.partial(_combine_ln_body, tm=tm, n_tok=m),
        grid_spec=pltpu.PrefetchScalarGridSpec(
            num_scalar_prefetch=1,
            grid=(m // tm,),
            in_specs=[pl.BlockSpec((tm, d), lambda i, dest: (i, 0)),
                      pl.BlockSpec((tm, LANES), lambda i, dest: (i, 0)),
                      pl.BlockSpec((1, d), lambda i, dest: (0, 0)),
                      pl.BlockSpec((1, d), lambda i, dest: (0, 0)),
                      pl.BlockSpec(memory_space=pl.ANY)],
            out_specs=pl.BlockSpec((tm, d), lambda i, dest: (i, 0)),
            scratch_shapes=[pltpu.VMEM((2, TOP_K, tm, d), F32), pltpu.SemaphoreType.DMA((2,))]),
        out_shape=jax.ShapeDtypeStruct((m, d), F32),
        compiler_params=_params(("arbitrary",)),
        name="moe_combine_ln",
    )(dest, h, route, g.reshape(1, -1), b.reshape(1, -1), yb)


def _moe_layer(h, w_router, w13, w2, g, b):
    n_tok, _ = h.shape
    route, cnt = _router(h, w_router, tm=512)
    counts = cnt[0, :N_EXPERTS].astype(jnp.int32)
    padded = (counts + MOE_ROWS - 1) // MOE_ROWS * MOE_ROWS
    pstart = jnp.cumsum(padded) - padded
    experts = route[:, 0:2].astype(jnp.int32)
    ranks = route[:, 2:4].astype(jnp.int32)
    dest = (pstart[experts] + ranks).T.reshape(-1)
    n_rows = n_tok * TOP_K + N_EXPERTS * MOE_ROWS
    n_blocks = n_rows // MOE_ROWS
    block_end = jnp.cumsum(padded // MOE_ROWS)
    block_e = jnp.sum(jnp.arange(n_blocks)[:, None] >= block_end[None, :], axis=1)
    block_e = jnp.minimum(block_e, N_EXPERTS - 1).astype(jnp.int32)
    n_valid = block_end[-1:].astype(jnp.int32)
    meta = jnp.concatenate([pstart, counts, padded]).astype(jnp.int32)
    buf = _dispatch(dest, meta, h, n_rows, tm=512)
    yb = _moe(block_e, n_valid, buf, w13, w2)
    return _combine_ln(dest, h, route, yb, g, b, tm=512)


def kernel(x, mem, a_w_in, a_conv_w, a_conv_b, a_w_rgate, a_b_rgate, a_w_igate, a_b_igate, a_lambda,
           a_w_out, w_kv_shared, b_w_q, b_lambda, b_subln_g, b_w_out, mem_w_kv, ffn_w13, ffn_w2,
           moe_router, moe_w13, moe_w2, ln_g, ln_b):
    bsz, t, d = x.shape
    n_tok = bsz * t
    n_mem = mem.shape[1]
    bf = lambda a: a.astype(BF16)

    kv_mem = _proj(mem.reshape(bsz * n_mem, d), bf(jnp.concatenate([mem_w_kv[0], mem_w_kv[1]], axis=1)),
                   [(0, 2 * MEM_W), (2 * MEM_W, 4 * MEM_W)], [None, None], [BF16, BF16],
                   tm=min(1024, bsz * n_mem), name="mem_kv")
    kv_mem = [a.reshape(bsz, n_mem, 2 * MEM_W) for a in kv_mem]

    x2 = x.reshape(n_tok, d)
    gate, u_pre, q_mem = _proj(
        x2, bf(a_w_in[0]), [(0, D_RNN), (D_RNN, 2 * D_RNN), (2 * D_RNN, 2 * D_RNN + MEM_W)],
        [jax.nn.gelu, None, None], [BF16, BF16, BF16], tm=512, name="proj_in")
    w_ri = bf(jnp.concatenate([a_w_rgate[0], a_w_igate[0]], axis=-1))
    rnn = _rglru(u_pre.reshape(bsz, t, D_RNN), gate.reshape(bsz, t, D_RNN), a_conv_w[0], a_conv_b[0],
                 w_ri, a_b_rgate[0], a_b_igate[0], a_lambda[0], tt=512)
    h = _outproj_ln(rnn, q_mem.reshape(bsz, t, MEM_W), kv_mem[0], bf(a_w_out[0]), x,
                    ln_g[0, 0], ln_b[0, 0], tm=512, name="outproj_ln_a")
    h = _ffn_ln(h.reshape(n_tok, d), bf(ffn_w13[0]), bf(ffn_w2[0]), ln_g[0, 1], ln_b[0, 1], tm=512)

    layer = 1
    lam_init = 0.8 - 0.6 * math.exp(-0.3 * layer)
    w_cat = bf(jnp.concatenate([w_kv_shared[:, :DIFF_QK_W], b_w_q[0]], axis=1))
    w_v_t = bf(w_kv_shared[:, DIFF_QK_W:].T)
    k_sh, v_t, q_diff, q_mem = _proj_kvq(h, w_cat, w_v_t, bsz, t, tm=512)
    attn = _diffattn(q_diff.reshape(bsz, t, DIFF_QK_W), k_sh.reshape(bsz, t, DIFF_QK_W), v_t,
                     b_lambda[0], b_subln_g[0], lam_init, tq=256)
    h = _outproj_ln(attn, q_mem.reshape(bsz, t, MEM_W), kv_mem[1], bf(b_w_out[0]),
                    h.reshape(bsz, t, d), ln_g[1, 0], ln_b[1, 0], tm=512, name="outproj_ln_b")
    out = _moe_layer(h.reshape(n_tok, d), moe_router[0], bf(moe_w13[0]), bf(moe_w2[0]),
                     ln_g[1, 1], ln_b[1, 1])
    return out.reshape(bsz, t, d)
```

```python
import functools
import math

import jax
import jax.numpy as jnp
from jax import lax
from jax.experimental import pallas as pl
from jax.experimental.pallas import tpu as pltpu

F32 = jnp.float32
BF16 = jnp.bfloat16

D_MODEL = 1024
DEPTH = 2
D_RNN = D_MODEL
RNN_BLOCKS = 8
RNN_BLOCK_W = D_RNN // RNN_BLOCKS
CONV_W = 4
LRU_C = 8.0
MEM_HEADS = 4
MEM_HEAD_DIM = D_MODEL // 8
MEM_W = MEM_HEADS * MEM_HEAD_DIM
DIFF_HEADS = 8
DIFF_HEAD_DIM = D_MODEL // 16
DIFF_V_DIM = 2 * DIFF_HEAD_DIM
DIFF_QK_W = DIFF_HEADS * 2 * DIFF_HEAD_DIM
DIFF_V_W = DIFF_HEADS * DIFF_V_DIM
FFN_DIM = (7 * D_MODEL) // 2
N_EXPERTS = 8
TOP_K = 2
LN_EPS = 1e-5
DEEPNORM_ALPHA = (2.0 * DEPTH) ** 0.25

LANES = 128
SUBLANES = 8
VMEM_LIMIT = 52 * 1024 * 1024

FFN_TILE = 1792
MOE_ROWS = 512
NEG_BIG = -1e30
LOG2E = math.log2(math.e)


def _params(semantics):
    return pltpu.CompilerParams(dimension_semantics=semantics, vmem_limit_bytes=VMEM_LIMIT)


def _layer_norm(z, g, b):
    mu = jnp.mean(z, axis=-1, keepdims=True)
    zc = z - mu
    var = jnp.mean(zc * zc, axis=-1, keepdims=True)
    return zc * lax.rsqrt(var + LN_EPS) * g + b


def _proj_body(x_ref, w_ref, *o_refs, splits, post):
    xb = x_ref[...].astype(BF16)
    for o_ref, (c0, c1), fn in zip(o_refs, splits, post):
        z = jnp.dot(xb, w_ref[:, c0:c1], preferred_element_type=F32)
        if fn is not None:
            z = fn(z)
        o_ref[...] = z.astype(o_ref.dtype)


def _proj(x, w, splits, post, out_dtypes, tm, name):
    m, k = x.shape
    return pl.pallas_call(
        functools.partial(_proj_body, splits=tuple(splits), post=tuple(post)),
        grid=(m // tm,),
        in_specs=[pl.BlockSpec((tm, k), lambda i: (i, 0)),
                  pl.BlockSpec(w.shape, lambda i: (0, 0))],
        out_specs=[pl.BlockSpec((tm, c1 - c0), lambda i: (i, 0)) for c0, c1 in splits],
        out_shape=[jax.ShapeDtypeStruct((m, c1 - c0), dt) for (c0, c1), dt in zip(splits, out_dtypes)],
        compiler_params=_params(("parallel",)),
        name=name,
    )(x, w)


def _proj_kvq_body(x_ref, w_ref, wkt_ref, v_ref, kt_ref, qd_ref, qm_ref):
    xb = x_ref[...].astype(BF16)
    q_scale = DIFF_HEAD_DIM ** -0.5 * LOG2E
    v_ref[...] = jnp.dot(xb, w_ref[:, 0:DIFF_V_W], preferred_element_type=F32).astype(BF16)
    kt_ref[0] = lax.dot_general(wkt_ref[...], xb, (((1,), (1,)), ((), ())),
                                preferred_element_type=F32).astype(BF16)
    qd = jnp.dot(xb, w_ref[:, DIFF_V_W:DIFF_V_W + DIFF_QK_W], preferred_element_type=F32)
    qd_ref[...] = (qd * q_scale).astype(BF16)
    qm_ref[...] = jnp.dot(xb, w_ref[:, DIFF_V_W + DIFF_QK_W:], preferred_element_type=F32).astype(BF16)


def _proj_kvq(h, w_cat, w_k_t, bsz, t, tm):
    m, d = h.shape
    nt = t // tm
    rows = lambda width: pl.BlockSpec((tm, width), lambda i: (i, 0))
    return pl.pallas_call(
        _proj_kvq_body,
        grid=(m // tm,),
        in_specs=[rows(d), pl.BlockSpec(w_cat.shape, lambda i: (0, 0)),
                  pl.BlockSpec(w_k_t.shape, lambda i: (0, 0))],
        out_specs=[rows(DIFF_V_W), pl.BlockSpec((1, DIFF_QK_W, tm), lambda i: (i // nt, 0, i % nt)),
                   rows(DIFF_QK_W), rows(MEM_W)],
        out_shape=[jax.ShapeDtypeStruct((m, DIFF_V_W), BF16),
                   jax.ShapeDtypeStruct((bsz, DIFF_QK_W, t), BF16),
                   jax.ShapeDtypeStruct((m, DIFF_QK_W), BF16),
                   jax.ShapeDtypeStruct((m, MEM_W), BF16)],
        compiler_params=_params(("parallel",)),
        name="proj_kvq",
    )(h, w_cat, w_k_t)


def _rglru_body(u_ref, g_ref, cw_ref, cb_ref, wri_ref, br_ref, bi_ref, lam_ref, o_ref,
                ext_ref, rec_ref, a_ref, b_ref, h_ref, *, tt):
    t = pl.program_id(1)

    @pl.when(t == 0)
    def _():
        ext_ref[0:SUBLANES, :] = jnp.zeros((SUBLANES, D_RNN), F32)
        h_ref[...] = jnp.zeros((SUBLANES, D_RNN), F32)

    ext_ref[SUBLANES:, :] = u_ref[0].astype(F32)
    rec = cb_ref[...] + cw_ref[CONV_W - 1:CONV_W, :] * ext_ref[SUBLANES:SUBLANES + tt, :]
    for j in range(CONV_W - 1):
        off = SUBLANES - (CONV_W - 1) + j
        rec = rec + cw_ref[j:j + 1, :] * ext_ref[off:off + tt, :]
    rec_ref[...] = rec
    ext_ref[0:SUBLANES, :] = ext_ref[tt:tt + SUBLANES, :]

    sp = jax.nn.softplus(-lam_ref[...])
    first_row = (lax.broadcasted_iota(jnp.int32, (tt, RNN_BLOCK_W), 0) + t * tt) == 0
    for n in range(RNN_BLOCKS):
        blk = slice(n * RNN_BLOCK_W, (n + 1) * RNN_BLOCK_W)
        u_n = rec_ref[:, blk]
        ri = jnp.dot(u_n.astype(BF16), wri_ref[n], preferred_element_type=F32)
        r = jax.nn.sigmoid(ri[:, :RNN_BLOCK_W] + br_ref[:, blk])
        i = jax.nn.sigmoid(ri[:, RNN_BLOCK_W:] + bi_ref[:, blk])
        log_a = -LRU_C * r * sp[:, blk]
        a = jnp.exp(log_a)
        th = jnp.tanh(-log_a)
        m2 = 2.0 * th / (1.0 + th)
        mult = jnp.where(m2 > 0.0, m2 * lax.rsqrt(m2), 0.0)
        mult = jnp.where(first_row, 1.0, mult)
        a_ref[:, blk] = a
        b_ref[:, blk] = mult * (i * u_n)

    row = lax.broadcasted_iota(jnp.int32, (SUBLANES, D_RNN), 0)

    def tile(i, h_prev):
        r0 = pl.multiple_of(i * SUBLANES, SUBLANES)
        a = a_ref[pl.ds(r0, SUBLANES), :]
        b = b_ref[pl.ds(r0, SUBLANES), :]
        for s in (1, 2, 4):
            a_s = pltpu.roll(a, s, 0)
            b_s = pltpu.roll(b, s, 0)
            keep = row >= s
            b = jnp.where(keep, a * b_s + b, b)
            a = jnp.where(keep, a * a_s, a)
        h = b + a * h_prev
        b_ref[pl.ds(r0, SUBLANES), :] = h
        return jnp.broadcast_to(h[SUBLANES - 1:SUBLANES, :], (SUBLANES, D_RNN))

    h_ref[...] = lax.fori_loop(0, tt // SUBLANES, tile, h_ref[...])
    o_ref[0] = (b_ref[...] * g_ref[0].astype(F32)).astype(o_ref.dtype)


def _rglru(u_pre, gate, conv_w, conv_b, w_ri, b_r, b_i, lam, tt):
    bsz, t, c = u_pre.shape
    row = lambda a: a.reshape(1, c)
    full = lambda shape: pl.BlockSpec(shape, lambda b, i: (0,) * len(shape))
    return pl.pallas_call(
        functools.partial(_rglru_body, tt=tt),
        grid=(bsz, t // tt),
        in_specs=[pl.BlockSpec((1, tt, c), lambda b, i: (b, i, 0)),
                  pl.BlockSpec((1, tt, c), lambda b, i: (b, i, 0)),
                  full((CONV_W, c)), full((1, c)), full(w_ri.shape),
                  full((1, c)), full((1, c)), full((1, c))],
        out_specs=pl.BlockSpec((1, tt, c), lambda b, i: (b, i, 0)),
        out_shape=jax.ShapeDtypeStruct((bsz, t, c), BF16),
        scratch_shapes=[pltpu.VMEM((tt + SUBLANES, c), F32), pltpu.VMEM((tt, c), F32),
                        pltpu.VMEM((tt, c), F32), pltpu.VMEM((tt, c), F32),
                        pltpu.VMEM((SUBLANES, c), F32)],
        compiler_params=_params(("parallel", "arbitrary")),
        name="rglru",
    )(u_pre, gate, conv_w, row(conv_b), w_ri, row(b_r), row(b_i), row(lam))


def _outproj_ln_body(a_ref, qm_ref, kv_ref, w_ref, res_ref, g_ref, b_ref, o_ref):
    mix_w = a_ref.shape[-1]
    y = jnp.dot(a_ref[0], w_ref[0:mix_w, :], preferred_element_type=F32)
    scale = MEM_HEAD_DIM ** -0.5
    heads = []
    for h in range(MEM_HEADS):
        hs = slice(h * MEM_HEAD_DIM, (h + 1) * MEM_HEAD_DIM)
        vs = slice(MEM_W + h * MEM_HEAD_DIM, MEM_W + (h + 1) * MEM_HEAD_DIM)
        s = lax.dot_general(qm_ref[0, :, hs], kv_ref[0, :, hs], (((1,), (1,)), ((), ())),
                            preferred_element_type=F32) * scale
        e = jnp.exp(s - jnp.max(s, axis=-1, keepdims=True))
        p = e / jnp.sum(e, axis=-1, keepdims=True)
        heads.append(jnp.dot(p.astype(BF16), kv_ref[0, :, vs], preferred_element_type=F32))
    mem_out = jnp.concatenate(heads, axis=1).astype(BF16)
    y = y + jnp.dot(mem_out, w_ref[mix_w:, :], preferred_element_type=F32)
    z = DEEPNORM_ALPHA * res_ref[0] + y
    o_ref[0] = _layer_norm(z, g_ref[...], b_ref[...])


def _outproj_ln(mix, q_mem, kv_mem, w_out, resid, g, b, tm, name):
    bsz, t, mix_w = mix.shape
    n_mem = kv_mem.shape[1]
    return pl.pallas_call(
        _outproj_ln_body,
        grid=(bsz, t // tm),
        in_specs=[pl.BlockSpec((1, tm, mix_w), lambda b, i: (b, i, 0)),
                  pl.BlockSpec((1, tm, MEM_W), lambda b, i: (b, i, 0)),
                  pl.BlockSpec((1, n_mem, 2 * MEM_W), lambda b, i: (b, 0, 0)),
                  pl.BlockSpec(w_out.shape, lambda b, i: (0, 0)),
                  pl.BlockSpec((1, tm, D_MODEL), lambda b, i: (b, i, 0)),
                  pl.BlockSpec((1, D_MODEL), lambda b, i: (0, 0)),
                  pl.BlockSpec((1, D_MODEL), lambda b, i: (0, 0))],
        out_specs=pl.BlockSpec((1, tm, D_MODEL), lambda b, i: (b, i, 0)),
        out_shape=jax.ShapeDtypeStruct((bsz, t, D_MODEL), F32),
        compiler_params=_params(("parallel", "parallel")),
        name=name,
    )(mix, q_mem, kv_mem, w_out, resid, g.reshape(1, -1), b.reshape(1, -1))


def _ffn_ln_body(h_ref, w1_ref, w3_ref, w2_ref, g_ref, b_ref, o_ref, hb_ref, acc_ref):
    f = pl.program_id(1)

    @pl.when(f == 0)
    def _():
        hb_ref[...] = h_ref[...].astype(BF16)
        acc_ref[...] = jnp.zeros_like(acc_ref)

    hb = hb_ref[...]
    gate = jnp.dot(hb, w1_ref[...], preferred_element_type=F32)
    up = jnp.dot(hb, w3_ref[...], preferred_element_type=F32)
    act = (jax.nn.silu(gate) * up).astype(BF16)
    acc_ref[...] += jnp.dot(act, w2_ref[...], preferred_element_type=F32)

    @pl.when(f == pl.num_programs(1) - 1)
    def _():
        z = DEEPNORM_ALPHA * h_ref[...] + acc_ref[...]
        o_ref[...] = _layer_norm(z, g_ref[...], b_ref[...])


def _ffn_ln(h, w13, w2, g, b, tm):
    m, d = h.shape
    nf = FFN_DIM // FFN_TILE
    return pl.pallas_call(
        _ffn_ln_body,
        grid=(m // tm, nf),
        in_specs=[pl.BlockSpec((tm, d), lambda i, f: (i, 0)),
                  pl.BlockSpec((d, FFN_TILE), lambda i, f: (0, f)),
                  pl.BlockSpec((d, FFN_TILE), lambda i, f: (0, nf + f)),
                  pl.BlockSpec((FFN_TILE, d), lambda i, f: (f, 0)),
                  pl.BlockSpec((1, d), lambda i, f: (0, 0)),
                  pl.BlockSpec((1, d), lambda i, f: (0, 0))],
        out_specs=pl.BlockSpec((tm, d), lambda i, f: (i, 0)),
        out_shape=jax.ShapeDtypeStruct((m, d), F32),
        scratch_shapes=[pltpu.VMEM((tm, d), BF16), pltpu.VMEM((tm, d), F32)],
        compiler_params=_params(("parallel", "arbitrary")),
        name="ffn_ln",
    )(h, w13, w13, w2, g.reshape(1, -1), b.reshape(1, -1))


ATT_STRIP = 128


def _diffattn_body(pi_ref, pj_ref, lamv_ref, sub_ref, coef_ref, q_ref, kt_ref, v_ref, o_ref,
                   kaug_ref, vaug_ref, qq_ref, mask_ref, s_ref, p_ref, c_ref, m_ref, acc_ref,
                   *, tq, n_pairs, unroll, lam_init):
    t = v_ref.shape[1]
    hd = 2 * DIFF_HEAD_DIM
    n_strips = 2 * tq // ATT_STRIP

    feat_row = lax.broadcasted_iota(jnp.int32, (LANES, tq), 0)
    in_block = lax.broadcasted_iota(jnp.int32, (LANES, tq), 1)
    for kb in range(t // tq):
        feat = jnp.where(feat_row < 3, kb, jnp.where(feat_row < 6, in_block, 0))
        kaug_ref[kb, 0:hd, :] = kt_ref[0, :, kb * tq:(kb + 1) * tq]
        kaug_ref[kb, hd:, :] = feat.astype(F32).astype(BF16)
    vaug_ref[:, 0:DIFF_V_DIM] = v_ref[0]
    vaug_ref[:, DIFF_V_DIM:] = jnp.ones((t, LANES), BF16)
    lane = lax.broadcasted_iota(jnp.int32, (tq, hd), 1)
    coef_rows = jnp.broadcast_to(coef_ref[0], (2 * tq, LANES))
    for qb in range(t // tq):
        q = q_ref[0, qb * tq:(qb + 1) * tq, :]
        zero = jnp.zeros_like(q)
        qq_ref[qb * 2 * tq:qb * 2 * tq + tq, 0:hd] = jnp.where(lane < DIFF_HEAD_DIM, q, zero)
        qq_ref[qb * 2 * tq + tq:(qb + 1) * 2 * tq, 0:hd] = jnp.where(lane < DIFF_HEAD_DIM, zero, q)
        qq_ref[qb * 2 * tq:(qb + 1) * 2 * tq, hd:] = coef_rows
    q_row = lax.broadcasted_iota(jnp.int32, (tq, tq), 0)
    k_col = lax.broadcasted_iota(jnp.int32, (tq, tq), 1)
    mask_ref[0] = jnp.zeros((tq, tq), F32)
    mask_ref[1] = jnp.where(k_col <= q_row, 0.0, NEG_BIG)
    m_ref[...] = jnp.full(m_ref.shape, NEG_BIG, F32)
    acc_ref[...] = jnp.zeros_like(acc_ref)

    def stage_qk(p, slot):
        q0 = pl.multiple_of(pi_ref[p] * (2 * tq), 2 * tq)
        s_ref[slot] = jnp.dot(qq_ref[pl.ds(q0, 2 * tq), :], kaug_ref[pj_ref[p]],
                              preferred_element_type=F32)

    def stage_sm(p, slot):
        i = pi_ref[p]
        diag = (i == pj_ref[p]).astype(jnp.int32)
        for r in range(n_strips):
            rows = slice(r * ATT_STRIP, (r + 1) * ATT_STRIP)
            mrows = slice((r * ATT_STRIP) % tq, (r * ATT_STRIP) % tq + ATT_STRIP)
            s = s_ref[slot, rows, :] + mask_ref[diag, mrows, :]
            m_prev = m_ref[i, rows, :]
            m_next = jnp.maximum(m_prev, jnp.max(s, axis=-1, keepdims=True))
            p_blk = jnp.exp2(s - jnp.tile(m_next, (1, tq // LANES)))
            p_ref[slot, rows, :] = p_blk.astype(BF16)
            c_ref[slot, rows, :] = jnp.exp2(m_prev - m_next)
            m_ref[i, rows, :] = m_next

    def stage_pv(p, slot):
        i = pi_ref[p]
        k0 = pl.multiple_of(pj_ref[p] * tq, tq)
        pv = jnp.dot(p_ref[slot], vaug_ref[pl.ds(k0, tq), :], preferred_element_type=F32)
        acc_ref[i] = acc_ref[i] * jnp.tile(c_ref[slot], (1, 2)) + pv

    stage_qk(0, 0)
    stage_qk(1, 1)
    stage_sm(0, 0)

    def pipeline_step(step, c):
        for u in range(unroll):
            p = unroll * step + u
            stage_qk(p + 2, u % 2)
            stage_sm(p + 1, (u + 1) % 2)
            stage_pv(p, u % 2)
        return c

    lax.fori_loop(0, n_pairs // unroll, pipeline_step, 0)

    lv = lamv_ref[...]
    lam = (jnp.exp(jnp.sum(lv[0:1] * lv[1:2], axis=-1, keepdims=True))
           - jnp.exp(jnp.sum(lv[2:3] * lv[3:4], axis=-1, keepdims=True)) + lam_init)

    def finalize(i, c):
        o = acc_ref[i, :, 0:DIFF_V_DIM] / acc_ref[i, :, DIFF_V_DIM:]
        o = o[:tq] - lam * o[tq:]
        o = o * lax.rsqrt(jnp.mean(o * o, axis=-1, keepdims=True) + LN_EPS) * sub_ref[...]
        r0 = pl.multiple_of(i * tq, tq)
        o_ref[0, pl.ds(r0, tq), :] = (o * (1.0 - lam_init)).astype(o_ref.dtype)
        return c

    lax.fori_loop(0, t // tq, finalize, 0)


def _diffattn(q, k_t, v, lam_vecs, subln_g, lam_init, tq):
    bsz, t, _ = q.shape
    slopes = jnp.exp2(-8.0 * (jnp.arange(DIFF_HEADS, dtype=F32) + 1.0) / DIFF_HEADS) * LOG2E
    def pieces(c):
        c1 = c.astype(BF16)
        c2 = (c - c1.astype(F32)).astype(BF16)
        c3 = (c - c1.astype(F32) - c2.astype(F32)).astype(BF16)
        return [c1, c2, c3]

    coef = jnp.zeros((DIFF_HEADS, 1, LANES), BF16)
    coef = coef.at[:, 0, 0:6].set(jnp.stack(pieces(slopes * tq) + pieces(slopes), axis=1))
    hd = 2 * DIFF_HEAD_DIM
    nq = t // tq
    assert nq <= 256, "key block indices must be exact in bf16"
    pairs = [(i, j) for i in range(nq) for j in range(i + 1)]
    n_pairs = len(pairs)
    unroll = next(u for u in (8, 4, 2) if n_pairs % u == 0)
    assert n_pairs % unroll == 0
    pairs = pairs + [pairs[-1]] * 2
    pair_i = jnp.asarray([p[0] for p in pairs], jnp.int32)
    pair_j = jnp.asarray([p[1] for p in pairs], jnp.int32)
    head = lambda shape: pl.BlockSpec(shape, lambda b, h, pi, pj: (b, 0, h))
    return pl.pallas_call(
        functools.partial(_diffattn_body, tq=tq, n_pairs=n_pairs, unroll=unroll, lam_init=lam_init),
        grid_spec=pltpu.PrefetchScalarGridSpec(
            num_scalar_prefetch=2,
            grid=(bsz, DIFF_HEADS),
            in_specs=[pl.BlockSpec(lam_vecs.shape, lambda b, h, pi, pj: (0, 0)),
                      pl.BlockSpec((1, DIFF_V_DIM), lambda b, h, pi, pj: (0, 0)),
                      pl.BlockSpec((1, 1, LANES), lambda b, h, pi, pj: (h, 0, 0)),
                      head((1, t, hd)),
                      pl.BlockSpec((1, hd, t), lambda b, h, pi, pj: (b, h, 0)),
                      head((1, t, DIFF_V_DIM))],
            out_specs=head((1, t, DIFF_V_DIM)),
            scratch_shapes=[pltpu.VMEM((nq, hd + LANES, tq), BF16),
                            pltpu.VMEM((t, DIFF_V_DIM + LANES), BF16),
                            pltpu.VMEM((2 * t, hd + LANES), BF16),
                            pltpu.VMEM((2, tq, tq), F32),
                            pltpu.VMEM((2, 2 * tq, tq), F32),
                            pltpu.VMEM((2, 2 * tq, tq), BF16),
                            pltpu.VMEM((2, 2 * tq, LANES), F32),
                            pltpu.VMEM((nq, 2 * tq, LANES), F32),
                            pltpu.VMEM((nq, 2 * tq, DIFF_V_DIM + LANES), F32)]),
        out_shape=jax.ShapeDtypeStruct((bsz, t, DIFF_V_W), BF16),
        compiler_params=_params(("parallel", "parallel")),
        name="diffattn",
    )(pair_i, pair_j, lam_vecs, subln_g.reshape(1, -1), coef, q, k_t, v)


def _router_body(h_ref, w_ref, o_ref, cnt_ref, carry_ref, *, tm):
    @pl.when(pl.program_id(0) == 0)
    def _():
        carry_ref[...] = jnp.zeros_like(carry_ref)

    h = h_ref[...]
    w = w_ref[...]
    h_hi = h.astype(BF16)
    h_lo = (h - h_hi.astype(F32)).astype(BF16)
    w_hi = w.astype(BF16)
    w_lo = (w - w_hi.astype(F32)).astype(BF16)
    logits = (jnp.dot(h_hi, w_hi, preferred_element_type=F32)
              + (jnp.dot(h_hi, w_lo, preferred_element_type=F32)
                 + jnp.dot(h_lo, w_hi, preferred_element_type=F32)))
    lane = lax.broadcasted_iota(jnp.int32, (tm, LANES), 1)
    lg = jnp.where(lane < N_EXPERTS, logits, -jnp.inf)
    v1 = jnp.max(lg, axis=-1, keepdims=True)
    i1 = jnp.min(jnp.where(lg == v1, lane, LANES), axis=-1, keepdims=True)
    oh1 = lane == i1
    lg2 = jnp.where(oh1, -jnp.inf, lg)
    v2 = jnp.max(lg2, axis=-1, keepdims=True)
    i2 = jnp.min(jnp.where(lg2 == v2, lane, LANES), axis=-1, keepdims=True)
    oh2 = lane == i2
    e2 = jnp.exp(v2 - v1)
    g1 = 1.0 / (1.0 + e2)
    g2 = e2 / (1.0 + e2)

    both = jnp.where(oh1, 1.0, jnp.where(oh2, 1.0, 0.0))
    tri = (lax.broadcasted_iota(jnp.int32, (tm, tm), 0)
           > lax.broadcasted_iota(jnp.int32, (tm, tm), 1))
    tri = jnp.where(tri, 1.0, 0.0).astype(BF16)
    before = jnp.dot(tri, both.astype(BF16), preferred_element_type=F32) + carry_ref[0:1, :]
    rank1 = jnp.sum(jnp.where(oh1, before, 0.0), axis=-1, keepdims=True)
    rank2 = jnp.sum(jnp.where(oh2, before, 0.0), axis=-1, keepdims=True)
    total = carry_ref[...] + jnp.sum(both, axis=0, keepdims=True)
    carry_ref[...] = total
    cnt_ref[...] = total

    out = jnp.where(lane == 0, i1.astype(F32), 0.0)
    out = jnp.where(lane == 1, i2.astype(F32), out)
    out = jnp.where(lane == 2, rank1, out)
    out = jnp.where(lane == 3, rank2, out)
    out = jnp.where(lane == 4, g1, out)
    out = jnp.where(lane == 5, g2, out)
    o_ref[...] = out


def _router(h, w_router, tm):
    m, d = h.shape
    w = jnp.zeros((d, LANES), F32).at[:, :N_EXPERTS].set(w_router)
    return pl.pallas_call(
        functools.partial(_router_body, tm=tm),
        grid=(m // tm,),
        in_specs=[pl.BlockSpec((tm, d), lambda i: (i, 0)),
                  pl.BlockSpec((d, LANES), lambda i: (0, 0))],
        out_specs=[pl.BlockSpec((tm, LANES), lambda i: (i, 0)),
                   pl.BlockSpec((SUBLANES, LANES), lambda i: (0, 0))],
        out_shape=[jax.ShapeDtypeStruct((m, LANES), F32),
                   jax.ShapeDtypeStruct((SUBLANES, LANES), F32)],
        scratch_shapes=[pltpu.VMEM((SUBLANES, LANES), F32)],
        compiler_params=_params(("arbitrary",)),
        name="router",
    )(h, w)


def _dispatch_body(dest_ref, meta_ref, h_ref, buf_ref, zero_ref, sem, *, tm, n_tok, n_rows):
    i = pl.program_id(0)
    base = i * tm

    def row_copy(r, d):
        return pltpu.make_async_copy(h_ref.at[pl.ds(r, 1), :], buf_ref.at[pl.ds(d, 1), :], sem)

    def issue(r, c):
        row_copy(r, dest_ref[base + r]).start()
        row_copy(r, dest_ref[n_tok + base + r]).start()
        return c

    lax.fori_loop(0, tm, issue, 0, unroll=8)
    all_rows = buf_ref.at[pl.ds(0, TOP_K * tm), :]
    pltpu.make_async_copy(all_rows, all_rows, sem).wait()

    @pl.when(i == pl.num_programs(0) - 1)
    def _():
        zero_ref[...] = jnp.zeros_like(zero_ref)

        def zero_row(r):
            return pltpu.make_async_copy(zero_ref.at[pl.ds(0, 1), :], buf_ref.at[pl.ds(r, 1), :], sem)

        def start_zero_row(r, c):
            zero_row(r).start()
            return c

        def wait_zero_row(r, c):
            zero_row(0).wait()
            return c

        for e in range(N_EXPERTS):
            lo = meta_ref[e] + meta_ref[N_EXPERTS + e]
            hi = meta_ref[e] + meta_ref[2 * N_EXPERTS + e]
            lax.fori_loop(lo, hi, start_zero_row, 0)
            lax.fori_loop(lo, hi, wait_zero_row, 0)

        def zero_block(b, c):
            r0 = pl.multiple_of(b * MOE_ROWS, MOE_ROWS)
            cp = pltpu.make_async_copy(zero_ref, buf_ref.at[pl.ds(r0, MOE_ROWS), :], sem)
            cp.start()
            cp.wait()
            return c

        used = meta_ref[N_EXPERTS - 1] + meta_ref[3 * N_EXPERTS - 1]
        lax.fori_loop(used // MOE_ROWS, n_rows // MOE_ROWS, zero_block, 0)


def _dispatch(dest, meta, h, n_rows, tm):
    m, d = h.shape
    return pl.pallas_call(
        functools.partial(_dispatch_body, tm=tm, n_tok=m, n_rows=n_rows),
        grid_spec=pltpu.PrefetchScalarGridSpec(
            num_scalar_prefetch=2,
            grid=(m // tm,),
            in_specs=[pl.BlockSpec((tm, d), lambda i, dest, meta: (i, 0))],
            out_specs=pl.BlockSpec(memory_space=pl.ANY),
            scratch_shapes=[pltpu.VMEM((MOE_ROWS, d), F32), pltpu.SemaphoreType.DMA(())]),
        out_shape=jax.ShapeDtypeStruct((n_rows, d), F32),
        compiler_params=_params(("arbitrary",)),
        name="moe_dispatch",
    )(dest, meta, h)


def _moe_body(be_ref, nv_ref, x_ref, w1_ref, w3_ref, w2_ref, o_ref, xb_ref, acc_ref):
    b = pl.program_id(0)
    f = pl.program_id(1)

    @pl.when(b < nv_ref[0])
    def _():
        @pl.when(f == 0)
        def _():
            xb_ref[...] = x_ref[...].astype(BF16)
            acc_ref[...] = jnp.zeros_like(acc_ref)

        xb = xb_ref[...]
        gate = jnp.dot(xb, w1_ref[0], preferred_element_type=F32)
        up = jnp.dot(xb, w3_ref[0], preferred_element_type=F32)
        act = (jax.nn.silu(gate) * up).astype(BF16)
        acc_ref[...] += jnp.dot(act, w2_ref[0], preferred_element_type=F32)

        @pl.when(f == pl.num_programs(1) - 1)
        def _():
            o_ref[...] = acc_ref[...]

    @pl.when(jnp.logical_and(b >= nv_ref[0], f == pl.num_programs(1) - 1))
    def _():
        o_ref[...] = jnp.zeros_like(o_ref)


def _moe(block_e, n_valid, buf, w13, w2):
    n_rows, d = buf.shape
    nb = n_rows // MOE_ROWS
    nf = FFN_DIM // FFN_TILE

    def blk(b, nv):
        return jnp.minimum(b, nv[0] - 1)

    def ftile(b, f, nv):
        return jnp.where(b < nv[0], f, nf - 1)

    return pl.pallas_call(
        _moe_body,
        grid_spec=pltpu.PrefetchScalarGridSpec(
            num_scalar_prefetch=2,
            grid=(nb, nf),
            in_specs=[pl.BlockSpec((MOE_ROWS, d), lambda b, f, be, nv: (blk(b, nv), 0)),
                      pl.BlockSpec((1, d, FFN_TILE),
                                   lambda b, f, be, nv: (be[blk(b, nv)], 0, ftile(b, f, nv))),
                      pl.BlockSpec((1, d, FFN_TILE),
                                   lambda b, f, be, nv: (be[blk(b, nv)], 0, nf + ftile(b, f, nv))),
                      pl.BlockSpec((1, FFN_TILE, d),
                                   lambda b, f, be, nv: (be[blk(b, nv)], ftile(b, f, nv), 0))],
            out_specs=pl.BlockSpec((MOE_ROWS, d), lambda b, f, be, nv: (b, 0)),
            scratch_shapes=[pltpu.VMEM((MOE_ROWS, d), BF16), pltpu.VMEM((MOE_ROWS, d), F32)]),
        out_shape=jax.ShapeDtypeStruct((n_rows, d), F32),
        compiler_params=_params(("arbitrary", "arbitrary")),
        name="moe_experts",
    )(block_e, n_valid, buf, w13, w13, w2)


def _combine_ln_body(dest_ref, h_ref, route_ref, g_ref, b_ref, yb_ref, o_ref, rows_ref, sem,
                     *, tm, n_tok):
    i = pl.program_id(0)

    def gather_tile(tile, buf):
        base = tile * tm

        def issue(r, c):
            for k in range(TOP_K):
                pltpu.make_async_copy(yb_ref.at[pl.ds(dest_ref[k * n_tok + base + r], 1), :],
                                      rows_ref.at[buf, k, pl.ds(r, 1), :], sem.at[buf]).start()
            return c

        lax.fori_loop(0, tm, issue, 0, unroll=8)

    @pl.when(i == 0)
    def _():
        gather_tile(0, 0)

    @pl.when(i + 1 < pl.num_programs(0))
    def _():
        gather_tile(i + 1, (i + 1) % 2)

    buf = i % 2
    pltpu.make_async_copy(rows_ref.at[buf], rows_ref.at[buf], sem.at[buf]).wait()
    route = route_ref[...]
    y = route[:, 4:5] * rows_ref[buf, 0] + route[:, 5:6] * rows_ref[buf, 1]
    z = DEEPNORM_ALPHA * h_ref[...] + y
    o_ref[...] = _layer_norm(z, g_ref[...], b_ref[...])


def _combine_ln(dest, h, route, yb, g, b, tm):
    m, d = h.shape
    return pl.pallas_call(
        functools.partial(_combine_ln_body, tm=tm, n_tok=m),
        grid_spec=pltpu.PrefetchScalarGridSpec(
            num_scalar_prefetch=1,
            grid=(m // tm,),
            in_specs=[pl.BlockSpec((tm, d), lambda i, dest: (i, 0)),
                      pl.BlockSpec((tm, LANES), lambda i, dest: (i, 0)),
                      pl.BlockSpec((1, d), lambda i, dest: (0, 0)),
                      pl.BlockSpec((1, d), lambda i, dest: (0, 0)),
                      pl.BlockSpec(memory_space=pl.ANY)],
            out_specs=pl.BlockSpec((tm, d), lambda i, dest: (i, 0)),
            scratch_shapes=[pltpu.VMEM((2, TOP_K, tm, d), F32), pltpu.SemaphoreType.DMA((2,))]),
        out_shape=jax.ShapeDtypeStruct((m, d), F32),
        compiler_params=_params(("arbitrary",)),
        name="moe_combine_ln",
    )(dest, h, route, g.reshape(1, -1), b.reshape(1, -1), yb)


def _moe_layer(h, w_router, w13, w2, g, b):
    n_tok, _ = h.shape
    route, cnt = _router(h, w_router, tm=512)
    counts = cnt[0, :N_EXPERTS].astype(jnp.int32)
    padded = (counts + MOE_ROWS - 1) // MOE_ROWS * MOE_ROWS
    pstart = jnp.cumsum(padded) - padded
    experts = route[:, 0:2].astype(jnp.int32)
    ranks = route[:, 2:4].astype(jnp.int32)
    dest = (pstart[experts] + ranks).T.reshape(-1)
    n_rows = n_tok * TOP_K + N_EXPERTS * MOE_ROWS
    n_blocks = n_rows // MOE_ROWS
    block_end = jnp.cumsum(padded // MOE_ROWS)
    block_e = jnp.sum(jnp.arange(n_blocks)[:, None] >= block_end[None, :], axis=1)
    block_e = jnp.minimum(block_e, N_EXPERTS - 1).astype(jnp.int32)
    n_valid = block_end[-1:].astype(jnp.int32)
    meta = jnp.concatenate([pstart, counts, padded]).astype(jnp.int32)
    buf = _dispatch(dest, meta, h, n_rows, tm=512)
    yb = _moe(block_e, n_valid, buf, w13, w2)
    return _combine_ln(dest, h, route, yb, g, b, tm=512)


def kernel(x, mem, a_w_in, a_conv_w, a_conv_b, a_w_rgate, a_b_rgate, a_w_igate, a_b_igate, a_lambda,
           a_w_out, w_kv_shared, b_w_q, b_lambda, b_subln_g, b_w_out, mem_w_kv, ffn_w13, ffn_w2,
           moe_router, moe_w13, moe_w2, ln_g, ln_b):
    bsz, t, d = x.shape
    n_tok = bsz * t
    n_mem = mem.shape[1]
    bf = lambda a: a.astype(BF16)

    kv_mem = _proj(mem.reshape(bsz * n_mem, d), bf(jnp.concatenate([mem_w_kv[0], mem_w_kv[1]], axis=1)),
                   [(0, 2 * MEM_W), (2 * MEM_W, 4 * MEM_W)], [None, None], [BF16, BF16],
                   tm=min(1024, bsz * n_mem), name="mem_kv")
    kv_mem = [a.reshape(bsz, n_mem, 2 * MEM_W) for a in kv_mem]

    x2 = x.reshape(n_tok, d)
    gate, u_pre, q_mem = _proj(
        x2, bf(a_w_in[0]), [(0, D_RNN), (D_RNN, 2 * D_RNN), (2 * D_RNN, 2 * D_RNN + MEM_W)],
        [jax.nn.gelu, None, None], [BF16, BF16, BF16], tm=512, name="proj_in")
    w_ri = bf(jnp.concatenate([a_w_rgate[0], a_w_igate[0]], axis=-1))
    rnn = _rglru(u_pre.reshape(bsz, t, D_RNN), gate.reshape(bsz, t, D_RNN), a_conv_w[0], a_conv_b[0],
                 w_ri, a_b_rgate[0], a_b_igate[0], a_lambda[0], tt=512)
    h = _outproj_ln(rnn, q_mem.reshape(bsz, t, MEM_W), kv_mem[0], bf(a_w_out[0]), x,
                    ln_g[0, 0], ln_b[0, 0], tm=512, name="outproj_ln_a")
    h = _ffn_ln(h.reshape(n_tok, d), bf(ffn_w13[0]), bf(ffn_w2[0]), ln_g[0, 1], ln_b[0, 1], tm=512)

    layer = 1
    lam_init = 0.8 - 0.6 * math.exp(-0.3 * layer)
    w_cat = bf(jnp.concatenate([w_kv_shared[:, DIFF_QK_W:], b_w_q[0]], axis=1))
    w_k_t = bf(w_kv_shared[:, :DIFF_QK_W].T)
    v_sh, k_t, q_diff, q_mem = _proj_kvq(h, w_cat, w_k_t, bsz, t, tm=512)
    attn = _diffattn(q_diff.reshape(bsz, t, DIFF_QK_W), k_t, v_sh.reshape(bsz, t, DIFF_V_W),
                     b_lambda[0], b_subln_g[0], lam_init, tq=256)
    h = _outproj_ln(attn, q_mem.reshape(bsz, t, MEM_W), kv_mem[1], bf(b_w_out[0]),
                    h.reshape(bsz, t, d), ln_g[1, 0], ln_b[1, 0], tm=512, name="outproj_ln_b")
    out = _moe_layer(h.reshape(n_tok, d), moe_router[0], bf(moe_w13[0]), bf(moe_w2[0]),
                     ln_g[1, 1], ln_b[1, 1])
    return out.reshape(bsz, t, d)
```

```python
import functools
import math

import jax
import jax.numpy as jnp
from jax import lax
from jax.experimental import pallas as pl
from jax.experimental.pallas import tpu as pltpu

F32 = jnp.float32
BF16 = jnp.bfloat16

D_MODEL = 1024
DEPTH = 2
D_RNN = D_MODEL
RNN_BLOCKS = 8
RNN_BLOCK_W = D_RNN // RNN_BLOCKS
CONV_W = 4
LRU_C = 8.0
MEM_HEADS = 4
MEM_HEAD_DIM = D_MODEL // 8
MEM_W = MEM_HEADS * MEM_HEAD_DIM
DIFF_HEADS = 8
DIFF_HEAD_DIM = D_MODEL // 16
DIFF_V_DIM = 2 * DIFF_HEAD_DIM
DIFF_QK_W = DIFF_HEADS * 2 * DIFF_HEAD_DIM
DIFF_V_W = DIFF_HEADS * DIFF_V_DIM
FFN_DIM = (7 * D_MODEL) // 2
N_EXPERTS = 8
TOP_K = 2
LN_EPS = 1e-5
DEEPNORM_ALPHA = (2.0 * DEPTH) ** 0.25

LANES = 128
SUBLANES = 8
VMEM_LIMIT = 52 * 1024 * 1024

FFN_TILE = 1792
MOE_ROWS = 512
NEG_BIG = -1e30
LOG2E = math.log2(math.e)


def _params(semantics):
    return pltpu.CompilerParams(dimension_semantics=semantics, vmem_limit_bytes=VMEM_LIMIT)


def _layer_norm(z, g, b):
    mu = jnp.mean(z, axis=-1, keepdims=True)
    zc = z - mu
    var = jnp.mean(zc * zc, axis=-1, keepdims=True)
    return zc * lax.rsqrt(var + LN_EPS) * g + b


def _proj_body(x_ref, w_ref, *o_refs, splits, post):
    xb = x_ref[...].astype(BF16)
    for o_ref, (c0, c1), fn in zip(o_refs, splits, post):
        z = jnp.dot(xb, w_ref[:, c0:c1], preferred_element_type=F32)
        if fn is not None:
            z = fn(z)
        o_ref[...] = z.astype(o_ref.dtype)


def _proj(x, w, splits, post, out_dtypes, tm, name):
    m, k = x.shape
    return pl.pallas_call(
        functools.partial(_proj_body, splits=tuple(splits), post=tuple(post)),
        grid=(m // tm,),
        in_specs=[pl.BlockSpec((tm, k), lambda i: (i, 0)),
                  pl.BlockSpec(w.shape, lambda i: (0, 0))],
        out_specs=[pl.BlockSpec((tm, c1 - c0), lambda i: (i, 0)) for c0, c1 in splits],
        out_shape=[jax.ShapeDtypeStruct((m, c1 - c0), dt) for (c0, c1), dt in zip(splits, out_dtypes)],
        compiler_params=_params(("parallel",)),
        name=name,
    )(x, w)


def _proj_kvq_body(x_ref, w_ref, wkt_ref, v_ref, kt_ref, qd_ref, qm_ref):
    xb = x_ref[...].astype(BF16)
    q_scale = DIFF_HEAD_DIM ** -0.5 * LOG2E
    v_ref[...] = jnp.dot(xb, w_ref[:, 0:DIFF_V_W], preferred_element_type=F32).astype(BF16)
    kt_ref[0] = lax.dot_general(wkt_ref[...], xb, (((1,), (1,)), ((), ())),
                                preferred_element_type=F32).astype(BF16)
    qd = jnp.dot(xb, w_ref[:, DIFF_V_W:DIFF_V_W + DIFF_QK_W], preferred_element_type=F32)
    qd_ref[...] = (qd * q_scale).astype(BF16)
    qm_ref[...] = jnp.dot(xb, w_ref[:, DIFF_V_W + DIFF_QK_W:], preferred_element_type=F32).astype(BF16)


def _proj_kvq(h, w_cat, w_k_t, bsz, t, tm):
    m, d = h.shape
    nt = t // tm
    rows = lambda width: pl.BlockSpec((tm, width), lambda i: (i, 0))
    return pl.pallas_call(
        _proj_kvq_body,
        grid=(m // tm,),
        in_specs=[rows(d), pl.BlockSpec(w_cat.shape, lambda i: (0, 0)),
                  pl.BlockSpec(w_k_t.shape, lambda i: (0, 0))],
        out_specs=[rows(DIFF_V_W), pl.BlockSpec((1, DIFF_QK_W, tm), lambda i: (i // nt, 0, i % nt)),
                   rows(DIFF_QK_W), rows(MEM_W)],
        out_shape=[jax.ShapeDtypeStruct((m, DIFF_V_W), BF16),
                   jax.ShapeDtypeStruct((bsz, DIFF_QK_W, t), BF16),
                   jax.ShapeDtypeStruct((m, DIFF_QK_W), BF16),
                   jax.ShapeDtypeStruct((m, MEM_W), BF16)],
        compiler_params=_params(("parallel",)),
        name="proj_kvq",
    )(h, w_cat, w_k_t)


def _rglru_body(u_ref, g_ref, cw_ref, cb_ref, wri_ref, br_ref, bi_ref, lam_ref, o_ref,
                ext_ref, rec_ref, a_ref, b_ref, h_ref, *, tt):
    t = pl.program_id(1)

    @pl.when(t == 0)
    def _():
        ext_ref[0:SUBLANES, :] = jnp.zeros((SUBLANES, D_RNN), F32)
        h_ref[...] = jnp.zeros((SUBLANES, D_RNN), F32)

    ext_ref[SUBLANES:, :] = u_ref[0].astype(F32)
    rec = cb_ref[...] + cw_ref[CONV_W - 1:CONV_W, :] * ext_ref[SUBLANES:SUBLANES + tt, :]
    for j in range(CONV_W - 1):
        off = SUBLANES - (CONV_W - 1) + j
        rec = rec + cw_ref[j:j + 1, :] * ext_ref[off:off + tt, :]
    rec_ref[...] = rec
    ext_ref[0:SUBLANES, :] = ext_ref[tt:tt + SUBLANES, :]

    sp = jax.nn.softplus(-lam_ref[...])
    first_row = (lax.broadcasted_iota(jnp.int32, (tt, RNN_BLOCK_W), 0) + t * tt) == 0
    for n in range(RNN_BLOCKS):
        blk = slice(n * RNN_BLOCK_W, (n + 1) * RNN_BLOCK_W)
        u_n = rec_ref[:, blk]
        ri = jnp.dot(u_n.astype(BF16), wri_ref[n], preferred_element_type=F32)
        r = jax.nn.sigmoid(ri[:, :RNN_BLOCK_W] + br_ref[:, blk])
        i = jax.nn.sigmoid(ri[:, RNN_BLOCK_W:] + bi_ref[:, blk])
        log_a = -LRU_C * r * sp[:, blk]
        a = jnp.exp(log_a)
        th = jnp.tanh(-log_a)
        m2 = 2.0 * th / (1.0 + th)
        mult = jnp.where(m2 > 0.0, m2 * lax.rsqrt(m2), 0.0)
        mult = jnp.where(first_row, 1.0, mult)
        a_ref[:, blk] = a
        b_ref[:, blk] = mult * (i * u_n)

    row = lax.broadcasted_iota(jnp.int32, (SUBLANES, D_RNN), 0)

    def tile(i, h_prev):
        r0 = pl.multiple_of(i * SUBLANES, SUBLANES)
        a = a_ref[pl.ds(r0, SUBLANES), :]
        b = b_ref[pl.ds(r0, SUBLANES), :]
        for s in (1, 2, 4):
            a_s = pltpu.roll(a, s, 0)
            b_s = pltpu.roll(b, s, 0)
            keep = row >= s
            b = jnp.where(keep, a * b_s + b, b)
            a = jnp.where(keep, a * a_s, a)
        h = b + a * h_prev
        b_ref[pl.ds(r0, SUBLANES), :] = h
        return jnp.broadcast_to(h[SUBLANES - 1:SUBLANES, :], (SUBLANES, D_RNN))

    h_ref[...] = lax.fori_loop(0, tt // SUBLANES, tile, h_ref[...])
    o_ref[0] = (b_ref[...] * g_ref[0].astype(F32)).astype(o_ref.dtype)


def _rglru(u_pre, gate, conv_w, conv_b, w_ri, b_r, b_i, lam, tt):
    bsz, t, c = u_pre.shape
    row = lambda a: a.reshape(1, c)
    full = lambda shape: pl.BlockSpec(shape, lambda b, i: (0,) * len(shape))
    return pl.pallas_call(
        functools.partial(_rglru_body, tt=tt),
        grid=(bsz, t // tt),
        in_specs=[pl.BlockSpec((1, tt, c), lambda b, i: (b, i, 0)),
                  pl.BlockSpec((1, tt, c), lambda b, i: (b, i, 0)),
                  full((CONV_W, c)), full((1, c)), full(w_ri.shape),
                  full((1, c)), full((1, c)), full((1, c))],
        out_specs=pl.BlockSpec((1, tt, c), lambda b, i: (b, i, 0)),
        out_shape=jax.ShapeDtypeStruct((bsz, t, c), BF16),
        scratch_shapes=[pltpu.VMEM((tt + SUBLANES, c), F32), pltpu.VMEM((tt, c), F32),
                        pltpu.VMEM((tt, c), F32), pltpu.VMEM((tt, c), F32),
                        pltpu.VMEM((SUBLANES, c), F32)],
        compiler_params=_params(("parallel", "arbitrary")),
        name="rglru",
    )(u_pre, gate, conv_w, row(conv_b), w_ri, row(b_r), row(b_i), row(lam))


def _outproj_ln_body(a_ref, qm_ref, kv_ref, w_ref, res_ref, g_ref, b_ref, o_ref):
    mix_w = a_ref.shape[-1]
    y = jnp.dot(a_ref[0], w_ref[0:mix_w, :], preferred_element_type=F32)
    scale = MEM_HEAD_DIM ** -0.5
    heads = []
    for h in range(MEM_HEADS):
        hs = slice(h * MEM_HEAD_DIM, (h + 1) * MEM_HEAD_DIM)
        vs = slice(MEM_W + h * MEM_HEAD_DIM, MEM_W + (h + 1) * MEM_HEAD_DIM)
        s = lax.dot_general(qm_ref[0, :, hs], kv_ref[0, :, hs], (((1,), (1,)), ((), ())),
                            preferred_element_type=F32) * scale
        e = jnp.exp(s - jnp.max(s, axis=-1, keepdims=True))
        p = e / jnp.sum(e, axis=-1, keepdims=True)
        heads.append(jnp.dot(p.astype(BF16), kv_ref[0, :, vs], preferred_element_type=F32))
    mem_out = jnp.concatenate(heads, axis=1).astype(BF16)
    y = y + jnp.dot(mem_out, w_ref[mix_w:, :], preferred_element_type=F32)
    z = DEEPNORM_ALPHA * res_ref[0] + y
    o_ref[0] = _layer_norm(z, g_ref[...], b_ref[...])


def _outproj_ln(mix, q_mem, kv_mem, w_out, resid, g, b, tm, name):
    bsz, t, mix_w = mix.shape
    n_mem = kv_mem.shape[1]
    return pl.pallas_call(
        _outproj_ln_body,
        grid=(bsz, t // tm),
        in_specs=[pl.BlockSpec((1, tm, mix_w), lambda b, i: (b, i, 0)),
                  pl.BlockSpec((1, tm, MEM_W), lambda b, i: (b, i, 0)),
                  pl.BlockSpec((1, n_mem, 2 * MEM_W), lambda b, i: (b, 0, 0)),
                  pl.BlockSpec(w_out.shape, lambda b, i: (0, 0)),
                  pl.BlockSpec((1, tm, D_MODEL), lambda b, i: (b, i, 0)),
                  pl.BlockSpec((1, D_MODEL), lambda b, i: (0, 0)),
                  pl.BlockSpec((1, D_MODEL), lambda b, i: (0, 0))],
        out_specs=pl.BlockSpec((1, tm, D_MODEL), lambda b, i: (b, i, 0)),
        out_shape=jax.ShapeDtypeStruct((bsz, t, D_MODEL), F32),
        compiler_params=_params(("parallel", "parallel")),
        name=name,
    )(mix, q_mem, kv_mem, w_out, resid, g.reshape(1, -1), b.reshape(1, -1))


def _ffn_ln_body(h_ref, w1_ref, w3_ref, w2_ref, g_ref, b_ref, o_ref, hb_ref, acc_ref):
    f = pl.program_id(1)

    @pl.when(f == 0)
    def _():
        hb_ref[...] = h_ref[...].astype(BF16)
        acc_ref[...] = jnp.zeros_like(acc_ref)

    hb = hb_ref[...]
    gate = jnp.dot(hb, w1_ref[...], preferred_element_type=F32)
    up = jnp.dot(hb, w3_ref[...], preferred_element_type=F32)
    act = (jax.nn.silu(gate) * up).astype(BF16)
    acc_ref[...] += jnp.dot(act, w2_ref[...], preferred_element_type=F32)

    @pl.when(f == pl.num_programs(1) - 1)
    def _():
        z = DEEPNORM_ALPHA * h_ref[...] + acc_ref[...]
        o_ref[...] = _layer_norm(z, g_ref[...], b_ref[...])


def _ffn_ln(h, w13, w2, g, b, tm):
    m, d = h.shape
    nf = FFN_DIM // FFN_TILE
    return pl.pallas_call(
        _ffn_ln_body,
        grid=(m // tm, nf),
        in_specs=[pl.BlockSpec((tm, d), lambda i, f: (i, 0)),
                  pl.BlockSpec((d, FFN_TILE), lambda i, f: (0, f)),
                  pl.BlockSpec((d, FFN_TILE), lambda i, f: (0, nf + f)),
                  pl.BlockSpec((FFN_TILE, d), lambda i, f: (f, 0)),
                  pl.BlockSpec((1, d), lambda i, f: (0, 0)),
                  pl.BlockSpec((1, d), lambda i, f: (0, 0))],
        out_specs=pl.BlockSpec((tm, d), lambda i, f: (i, 0)),
        out_shape=jax.ShapeDtypeStruct((m, d), F32),
        scratch_shapes=[pltpu.VMEM((tm, d), BF16), pltpu.VMEM((tm, d), F32)],
        compiler_params=_params(("parallel", "arbitrary")),
        name="ffn_ln",
    )(h, w13, w13, w2, g.reshape(1, -1), b.reshape(1, -1))


ATT_STRIP = 128


def _diffattn_body(pi_ref, pj_ref, lamv_ref, sub_ref, coef_ref, q_ref, kt_ref, v_ref, o_ref,
                   kaug_ref, vaug_ref, qq_ref, mask_ref, s_ref, p_ref, c_ref, m_ref, acc_ref,
                   *, tq, n_pairs, unroll, lam_init):
    t = v_ref.shape[1]
    hd = 2 * DIFF_HEAD_DIM
    n_strips = 2 * tq // ATT_STRIP

    feat_row = lax.broadcasted_iota(jnp.int32, (LANES, tq), 0)
    in_block = lax.broadcasted_iota(jnp.int32, (LANES, tq), 1)
    for kb in range(t // tq):
        feat = jnp.where(feat_row < 3, kb, jnp.where(feat_row < 6, in_block, 0))
        kaug_ref[kb, 0:hd, :] = kt_ref[0, :, kb * tq:(kb + 1) * tq]
        kaug_ref[kb, hd:, :] = feat.astype(F32).astype(BF16)
    vaug_ref[:, 0:DIFF_V_DIM] = v_ref[0]
    vaug_ref[:, DIFF_V_DIM:] = jnp.ones((t, LANES), BF16)
    lane = lax.broadcasted_iota(jnp.int32, (tq, hd), 1)
    coef_rows = jnp.broadcast_to(coef_ref[0], (2 * tq, LANES))
    for qb in range(t // tq):
        q = q_ref[0, qb * tq:(qb + 1) * tq, :]
        zero = jnp.zeros_like(q)
        qq_ref[qb * 2 * tq:qb * 2 * tq + tq, 0:hd] = jnp.where(lane < DIFF_HEAD_DIM, q, zero)
        qq_ref[qb * 2 * tq + tq:(qb + 1) * 2 * tq, 0:hd] = jnp.where(lane < DIFF_HEAD_DIM, zero, q)
        qq_ref[qb * 2 * tq:(qb + 1) * 2 * tq, hd:] = coef_rows
    q_row = lax.broadcasted_iota(jnp.int32, (tq, tq), 0)
    k_col = lax.broadcasted_iota(jnp.int32, (tq, tq), 1)
    mask_ref[0] = jnp.zeros((tq, tq), F32)
    mask_ref[1] = jnp.where(k_col <= q_row, 0.0, NEG_BIG)
    m_ref[...] = jnp.full(m_ref.shape, NEG_BIG, F32)
    acc_ref[...] = jnp.zeros_like(acc_ref)

    def stage_qk(p, slot):
        q0 = pl.multiple_of(pi_ref[p] * (2 * tq), 2 * tq)
        s_ref[slot] = jnp.dot(qq_ref[pl.ds(q0, 2 * tq), :], kaug_ref[pj_ref[p]],
                              preferred_element_type=F32)

    def stage_sm(p, slot):
        i = pi_ref[p]
        diag = (i == pj_ref[p]).astype(jnp.int32)
        for r in range(n_strips):
            rows = slice(r * ATT_STRIP, (r + 1) * ATT_STRIP)
            mrows = slice((r * ATT_STRIP) % tq, (r * ATT_STRIP) % tq + ATT_STRIP)
            s = s_ref[slot, rows, :] + mask_ref[diag, mrows, :]
            m_prev = m_ref[i, rows, :]
            m_next = jnp.maximum(m_prev, jnp.max(s, axis=-1, keepdims=True))
            p_blk = jnp.exp2(s - jnp.tile(m_next, (1, tq // LANES)))
            p_ref[slot, rows, :] = p_blk.astype(BF16)
            c_ref[slot, rows, :] = jnp.exp2(m_prev - m_next)
            m_ref[i, rows, :] = m_next

    def stage_pv(p, slot):
        i = pi_ref[p]
        k0 = pl.multiple_of(pj_ref[p] * tq, tq)
        pv = jnp.dot(p_ref[slot], vaug_ref[pl.ds(k0, tq), :], preferred_element_type=F32)
        acc_ref[i] = acc_ref[i] * jnp.tile(c_ref[slot], (1, 2)) + pv

    stage_qk(0, 0)
    stage_qk(1, 1)
    stage_sm(0, 0)

    def pipeline_step(step, c):
        for u in range(unroll):
            p = unroll * step + u
            stage_qk(p + 2, u % 2)
            stage_sm(p + 1, (u + 1) % 2)
            stage_pv(p, u % 2)
        return c

    lax.fori_loop(0, n_pairs // unroll, pipeline_step, 0)

    lv = lamv_ref[...]
    lam = (jnp.exp(jnp.sum(lv[0:1] * lv[1:2], axis=-1, keepdims=True))
           - jnp.exp(jnp.sum(lv[2:3] * lv[3:4], axis=-1, keepdims=True)) + lam_init)

    def finalize(i, c):
        o = acc_ref[i, :, 0:DIFF_V_DIM] / acc_ref[i, :, DIFF_V_DIM:]
        o = o[:tq] - lam * o[tq:]
        o = o * lax.rsqrt(jnp.mean(o * o, axis=-1, keepdims=True) + LN_EPS) * sub_ref[...]
        r0 = pl.multiple_of(i * tq, tq)
        o_ref[0, pl.ds(r0, tq), :] = (o * (1.0 - lam_init)).astype(o_ref.dtype)
        return c

    lax.fori_loop(0, t // tq, finalize, 0, unroll=2)


def _diffattn(q, k_t, v, lam_vecs, subln_g, lam_init, tq):
    bsz, t, _ = q.shape
    slopes = jnp.exp2(-8.0 * (jnp.arange(DIFF_HEADS, dtype=F32) + 1.0) / DIFF_HEADS) * LOG2E
    def pieces(c):
        c1 = c.astype(BF16)
        c2 = (c - c1.astype(F32)).astype(BF16)
        c3 = (c - c1.astype(F32) - c2.astype(F32)).astype(BF16)
        return [c1, c2, c3]

    coef = jnp.zeros((DIFF_HEADS, 1, LANES), BF16)
    coef = coef.at[:, 0, 0:6].set(jnp.stack(pieces(slopes * tq) + pieces(slopes), axis=1))
    hd = 2 * DIFF_HEAD_DIM
    nq = t // tq
    assert nq <= 256, "key block indices must be exact in bf16"
    pairs = [(i, j) for i in range(nq) for j in range(i + 1)]
    n_pairs = len(pairs)
    unroll = next(u for u in (8, 4, 2) if n_pairs % u == 0)
    assert n_pairs % unroll == 0
    pairs = pairs + [pairs[-1]] * 2
    pair_i = jnp.asarray([p[0] for p in pairs], jnp.int32)
    pair_j = jnp.asarray([p[1] for p in pairs], jnp.int32)
    head = lambda shape: pl.BlockSpec(shape, lambda b, h, pi, pj: (b, 0, h))
    return pl.pallas_call(
        functools.partial(_diffattn_body, tq=tq, n_pairs=n_pairs, unroll=unroll, lam_init=lam_init),
        grid_spec=pltpu.PrefetchScalarGridSpec(
            num_scalar_prefetch=2,
            grid=(bsz, DIFF_HEADS),
            in_specs=[pl.BlockSpec(lam_vecs.shape, lambda b, h, pi, pj: (0, 0)),
                      pl.BlockSpec((1, DIFF_V_DIM), lambda b, h, pi, pj: (0, 0)),
                      pl.BlockSpec((1, 1, LANES), lambda b, h, pi, pj: (h, 0, 0)),
                      head((1, t, hd)),
                      pl.BlockSpec((1, hd, t), lambda b, h, pi, pj: (b, h, 0)),
                      head((1, t, DIFF_V_DIM))],
            out_specs=head((1, t, DIFF_V_DIM)),
            scratch_shapes=[pltpu.VMEM((nq, hd + LANES, tq), BF16),
                            pltpu.VMEM((t, DIFF_V_DIM + LANES), BF16),
                            pltpu.VMEM((2 * t, hd + LANES), BF16),
                            pltpu.VMEM((2, tq, tq), F32),
                            pltpu.VMEM((2, 2 * tq, tq), F32),
                            pltpu.VMEM((2, 2 * tq, tq), BF16),
                            pltpu.VMEM((2, 2 * tq, LANES), F32),
                            pltpu.VMEM((nq, 2 * tq, LANES), F32),
                            pltpu.VMEM((nq, 2 * tq, DIFF_V_DIM + LANES), F32)]),
        out_shape=jax.ShapeDtypeStruct((bsz, t, DIFF_V_W), BF16),
        compiler_params=_params(("parallel", "parallel")),
        name="diffattn",
    )(pair_i, pair_j, lam_vecs, subln_g.reshape(1, -1), coef, q, k_t, v)


def _router_body(h_ref, w_ref, o_ref, cnt_ref, carry_ref, *, tm):
    @pl.when(pl.program_id(0) == 0)
    def _():
        carry_ref[...] = jnp.zeros_like(carry_ref)

    h = h_ref[...]
    w = w_ref[...]
    h_hi = h.astype(BF16)
    h_lo = (h - h_hi.astype(F32)).astype(BF16)
    w_hi = w.astype(BF16)
    w_lo = (w - w_hi.astype(F32)).astype(BF16)
    logits = (jnp.dot(h_hi, w_hi, preferred_element_type=F32)
              + (jnp.dot(h_hi, w_lo, preferred_element_type=F32)
                 + jnp.dot(h_lo, w_hi, preferred_element_type=F32)))
    lane = lax.broadcasted_iota(jnp.int32, (tm, LANES), 1)
    lg = jnp.where(lane < N_EXPERTS, logits, -jnp.inf)
    v1 = jnp.max(lg, axis=-1, keepdims=True)
    i1 = jnp.min(jnp.where(lg == v1, lane, LANES), axis=-1, keepdims=True)
    oh1 = lane == i1
    lg2 = jnp.where(oh1, -jnp.inf, lg)
    v2 = jnp.max(lg2, axis=-1, keepdims=True)
    i2 = jnp.min(jnp.where(lg2 == v2, lane, LANES), axis=-1, keepdims=True)
    oh2 = lane == i2
    e2 = jnp.exp(v2 - v1)
    g1 = 1.0 / (1.0 + e2)
    g2 = e2 / (1.0 + e2)

    both = jnp.where(oh1, 1.0, jnp.where(oh2, 1.0, 0.0))
    tri = (lax.broadcasted_iota(jnp.int32, (tm, tm), 0)
           > lax.broadcasted_iota(jnp.int32, (tm, tm), 1))
    tri = jnp.where(tri, 1.0, 0.0).astype(BF16)
    before = jnp.dot(tri, both.astype(BF16), preferred_element_type=F32) + carry_ref[0:1, :]
    rank1 = jnp.sum(jnp.where(oh1, before, 0.0), axis=-1, keepdims=True)
    rank2 = jnp.sum(jnp.where(oh2, before, 0.0), axis=-1, keepdims=True)
    total = carry_ref[...] + jnp.sum(both, axis=0, keepdims=True)
    carry_ref[...] = total
    cnt_ref[...] = total

    out = jnp.where(lane == 0, i1.astype(F32), 0.0)
    out = jnp.where(lane == 1, i2.astype(F32), out)
    out = jnp.where(lane == 2, rank1, out)
    out = jnp.where(lane == 3, rank2, out)
    out = jnp.where(lane == 4, g1, out)
    out = jnp.where(lane == 5, g2, out)
    o_ref[...] = out


def _router(h, w_router, tm):
    m, d = h.shape
    w = jnp.zeros((d, LANES), F32).at[:, :N_EXPERTS].set(w_router)
    return pl.pallas_call(
        functools.partial(_router_body, tm=tm),
        grid=(m // tm,),
        in_specs=[pl.BlockSpec((tm, d), lambda i: (i, 0)),
                  pl.BlockSpec((d, LANES), lambda i: (0, 0))],
        out_specs=[pl.BlockSpec((tm, LANES), lambda i: (i, 0)),
                   pl.BlockSpec((SUBLANES, LANES), lambda i: (0, 0))],
        out_shape=[jax.ShapeDtypeStruct((m, LANES), F32),
                   jax.ShapeDtypeStruct((SUBLANES, LANES), F32)],
        scratch_shapes=[pltpu.VMEM((SUBLANES, LANES), F32)],
        compiler_params=_params(("arbitrary",)),
        name="router",
    )(h, w)


ROW_TILE = D_MODEL // LANES


def _row_tile(ref, r):
    start = r * ROW_TILE if isinstance(r, int) else pl.multiple_of(r * ROW_TILE, ROW_TILE)
    return ref.at[pl.ds(start, ROW_TILE), :]


def _store_row_tiles(tile_ref, x):
    for j in range(ROW_TILE):
        tile_ref[pl.ds(j, x.shape[0], stride=ROW_TILE), :] = x[:, j * LANES:(j + 1) * LANES]


def _load_row_tiles(tile_ref, n):
    return jnp.concatenate([tile_ref[pl.ds(j, n, stride=ROW_TILE), :] for j in range(ROW_TILE)], axis=1)


def _dispatch_body(dest_ref, meta_ref, h_ref, buf_ref, rows_ref, zero_ref, sem, *, tm, n_tok, n_rows):
    i = pl.program_id(0)
    base = i * tm
    _store_row_tiles(rows_ref, h_ref[...])

    def row_copy(r, d):
        return pltpu.make_async_copy(_row_tile(rows_ref, r), _row_tile(buf_ref, d), sem)

    def issue(r, c):
        row_copy(r, dest_ref[base + r]).start()
        row_copy(r, dest_ref[n_tok + base + r]).start()
        return c

    lax.fori_loop(0, tm, issue, 0, unroll=16)
    all_rows = buf_ref.at[pl.ds(0, TOP_K * tm * ROW_TILE), :]
    pltpu.make_async_copy(all_rows, all_rows, sem).wait()

    @pl.when(i == pl.num_programs(0) - 1)
    def _():
        zero_ref[...] = jnp.zeros_like(zero_ref)

        def zero_row(r):
            return pltpu.make_async_copy(_row_tile(zero_ref, 0), _row_tile(buf_ref, r), sem)

        def start_zero_row(r, c):
            zero_row(r).start()
            return c

        def wait_zero_row(r, c):
            zero_row(0).wait()
            return c

        for e in range(N_EXPERTS):
            lo = meta_ref[e] + meta_ref[N_EXPERTS + e]
            hi = meta_ref[e] + meta_ref[2 * N_EXPERTS + e]
            lax.fori_loop(lo, hi, start_zero_row, 0)
            lax.fori_loop(lo, hi, wait_zero_row, 0)

        def zero_block(b, c):
            r0 = pl.multiple_of(b * (MOE_ROWS * ROW_TILE), MOE_ROWS * ROW_TILE)
            cp = pltpu.make_async_copy(zero_ref, buf_ref.at[pl.ds(r0, MOE_ROWS * ROW_TILE), :], sem)
            cp.start()
            cp.wait()
            return c

        used = meta_ref[N_EXPERTS - 1] + meta_ref[3 * N_EXPERTS - 1]
        lax.fori_loop(used // MOE_ROWS, n_rows // MOE_ROWS, zero_block, 0)


def _dispatch(dest, meta, h, n_rows, tm):
    m, d = h.shape
    return pl.pallas_call(
        functools.partial(_dispatch_body, tm=tm, n_tok=m, n_rows=n_rows),
        grid_spec=pltpu.PrefetchScalarGridSpec(
            num_scalar_prefetch=2,
            grid=(m // tm,),
            in_specs=[pl.BlockSpec((tm, d), lambda i, dest, meta: (i, 0))],
            out_specs=pl.BlockSpec(memory_space=pl.ANY),
            scratch_shapes=[pltpu.VMEM((tm * ROW_TILE, LANES), F32),
                            pltpu.VMEM((MOE_ROWS * ROW_TILE, LANES), F32),
                            pltpu.SemaphoreType.DMA(())]),
        out_shape=jax.ShapeDtypeStruct((n_rows * ROW_TILE, LANES), F32),
        compiler_params=_params(("arbitrary",)),
        name="moe_dispatch",
    )(dest, meta, h)


def _moe_body(be_ref, nv_ref, x_ref, w1_ref, w3_ref, w2_ref, o_ref, xb_ref, acc_ref):
    b = pl.program_id(0)
    f = pl.program_id(1)

    @pl.when(b < nv_ref[0])
    def _():
        @pl.when(f == 0)
        def _():
            xb_ref[...] = _load_row_tiles(x_ref, MOE_ROWS).astype(BF16)
            acc_ref[...] = jnp.zeros_like(acc_ref)

        xb = xb_ref[...]
        gate = jnp.dot(xb, w1_ref[0], preferred_element_type=F32)
        up = jnp.dot(xb, w3_ref[0], preferred_element_type=F32)
        act = (jax.nn.silu(gate) * up).astype(BF16)
        acc_ref[...] += jnp.dot(act, w2_ref[0], preferred_element_type=F32)

        @pl.when(f == pl.num_programs(1) - 1)
        def _():
            _store_row_tiles(o_ref, acc_ref[...])

    @pl.when(jnp.logical_and(b >= nv_ref[0], f == pl.num_programs(1) - 1))
    def _():
        o_ref[...] = jnp.zeros_like(o_ref)


def _moe(block_e, n_valid, buf, w13, w2):
    d = D_MODEL
    n_rows = buf.shape[0] // ROW_TILE
    nb = n_rows // MOE_ROWS
    nf = FFN_DIM // FFN_TILE
    row_block = (MOE_ROWS * ROW_TILE, LANES)

    def blk(b, nv):
        return jnp.minimum(b, nv[0] - 1)

    def ftile(b, f, nv):
        return jnp.where(b < nv[0], f, nf - 1)

    return pl.pallas_call(
        _moe_body,
        grid_spec=pltpu.PrefetchScalarGridSpec(
            num_scalar_prefetch=2,
            grid=(nb, nf),
            in_specs=[pl.BlockSpec(row_block, lambda b, f, be, nv: (blk(b, nv), 0)),
                      pl.BlockSpec((1, d, FFN_TILE),
                                   lambda b, f, be, nv: (be[blk(b, nv)], 0, ftile(b, f, nv))),
                      pl.BlockSpec((1, d, FFN_TILE),
                                   lambda b, f, be, nv: (be[blk(b, nv)], 0, nf + ftile(b, f, nv))),
                      pl.BlockSpec((1, FFN_TILE, d),
                                   lambda b, f, be, nv: (be[blk(b, nv)], ftile(b, f, nv), 0))],
            out_specs=pl.BlockSpec(row_block, lambda b, f, be, nv: (b, 0)),
            scratch_shapes=[pltpu.VMEM((MOE_ROWS, d), BF16), pltpu.VMEM((MOE_ROWS, d), F32)]),
        out_shape=jax.ShapeDtypeStruct((n_rows * ROW_TILE, LANES), F32),
        compiler_params=_params(("arbitrary", "arbitrary")),
        name="moe_experts",
    )(block_e, n_valid, buf, w13, w13, w2)


def _combine_ln_body(dest_ref, h_ref, route_ref, g_ref, b_ref, yb_ref, o_ref, rows_ref, sem,
                     *, tm, n_tok):
    i = pl.program_id(0)

    def gather_tile(tile, buf):
        base = tile * tm

        def issue(r, c):
            for k in range(TOP_K):
                pltpu.make_async_copy(_row_tile(yb_ref, dest_ref[k * n_tok + base + r]),
                                      _row_tile(rows_ref.at[buf, k], r), sem.at[buf]).start()
            return c

        lax.fori_loop(0, tm, issue, 0, unroll=16)

    @pl.when(i == 0)
    def _():
        gather_tile(0, 0)

    @pl.when(i + 1 < pl.num_programs(0))
    def _():
        gather_tile(i + 1, (i + 1) % 2)

    buf = i % 2
    pltpu.make_async_copy(rows_ref.at[buf], rows_ref.at[buf], sem.at[buf]).wait()
    route = route_ref[...]
    y = (route[:, 4:5] * _load_row_tiles(rows_ref.at[buf, 0], tm)
         + route[:, 5:6] * _load_row_tiles(rows_ref.at[buf, 1], tm))
    z = DEEPNORM_ALPHA * h_ref[...] + y
    o_ref[...] = _layer_norm(z, g_ref[...], b_ref[...])


def _combine_ln(dest, h, route, yb, g, b, tm):
    m, d = h.shape
    return pl.pallas_call(
        functools.partial(_combine_ln_body, tm=tm, n_tok=m),
        grid_spec=pltpu.PrefetchScalarGridSpec(
            num_scalar_prefetch=1,
            grid=(m // tm,),
            in_specs=[pl.BlockSpec((tm, d), lambda i, dest: (i, 0)),
                      pl.BlockSpec((tm, LANES), lambda i, dest: (i, 0)),
                      pl.BlockSpec((1, d), lambda i, dest: (0, 0)),
                      pl.BlockSpec((1, d), lambda i, dest: (0, 0)),
                      pl.BlockSpec(memory_space=pl.ANY)],
            out_specs=pl.BlockSpec((tm, d), lambda i, dest: (i, 0)),
            scratch_shapes=[pltpu.VMEM((2, TOP_K, tm * ROW_TILE, LANES), F32),
                            pltpu.SemaphoreType.DMA((2,))]),
        out_shape=jax.ShapeDtypeStruct((m, d), F32),
        compiler_params=_params(("arbitrary",)),
        name="moe_combine_ln",
    )(dest, h, route, g.reshape(1, -1), b.reshape(1, -1), yb)


def _moe_layer(h, w_router, w13, w2, g, b):
    n_tok, _ = h.shape
    route, cnt = _router(h, w_router, tm=512)
    counts = cnt[0, :N_EXPERTS].astype(jnp.int32)
    padded = (counts + MOE_ROWS - 1) // MOE_ROWS * MOE_ROWS
    pstart = jnp.cumsum(padded) - padded
    experts = route[:, 0:2].astype(jnp.int32)
    ranks = route[:, 2:4].astype(jnp.int32)
    dest = (pstart[experts] + ranks).T.reshape(-1)
    n_rows = n_tok * TOP_K + N_EXPERTS * MOE_ROWS
    n_blocks = n_rows // MOE_ROWS
    block_end = jnp.cumsum(padded // MOE_ROWS)
    block_e = jnp.sum(jnp.arange(n_blocks)[:, None] >= block_end[None, :], axis=1)
    block_e = jnp.minimum(block_e, N_EXPERTS - 1).astype(jnp.int32)
    n_valid = block_end[-1:].astype(jnp.int32)
    meta = jnp.concatenate([pstart, counts, padded]).astype(jnp.int32)
    buf = _dispatch(dest, meta, h, n_rows, tm=512)
    yb = _moe(block_e, n_valid, buf, w13, w2)
    return _combine_ln(dest, h, route, yb, g, b, tm=512)


def kernel(x, mem, a_w_in, a_conv_w, a_conv_b, a_w_rgate, a_b_rgate, a_w_igate, a_b_igate, a_lambda,
           a_w_out, w_kv_shared, b_w_q, b_lambda, b_subln_g, b_w_out, mem_w_kv, ffn_w13, ffn_w2,
           moe_router, moe_w13, moe_w2, ln_g, ln_b):
    bsz, t, d = x.shape
    n_tok = bsz * t
    n_mem = mem.shape[1]
    bf = lambda a: a.astype(BF16)

    kv_mem = _proj(mem.reshape(bsz * n_mem, d), bf(jnp.concatenate([mem_w_kv[0], mem_w_kv[1]], axis=1)),
                   [(0, 2 * MEM_W), (2 * MEM_W, 4 * MEM_W)], [None, None], [BF16, BF16],
                   tm=min(1024, bsz * n_mem), name="mem_kv")
    kv_mem = [a.reshape(bsz, n_mem, 2 * MEM_W) for a in kv_mem]

    x2 = x.reshape(n_tok, d)
    gate, u_pre, q_mem = _proj(
        x2, bf(a_w_in[0]), [(0, D_RNN), (D_RNN, 2 * D_RNN), (2 * D_RNN, 2 * D_RNN + MEM_W)],
        [jax.nn.gelu, None, None], [BF16, BF16, BF16], tm=512, name="proj_in")
    w_ri = bf(jnp.concatenate([a_w_rgate[0], a_w_igate[0]], axis=-1))
    rnn = _rglru(u_pre.reshape(bsz, t, D_RNN), gate.reshape(bsz, t, D_RNN), a_conv_w[0], a_conv_b[0],
                 w_ri, a_b_rgate[0], a_b_igate[0], a_lambda[0], tt=512)
    h = _outproj_ln(rnn, q_mem.reshape(bsz, t, MEM_W), kv_mem[0], bf(a_w_out[0]), x,
                    ln_g[0, 0], ln_b[0, 0], tm=512, name="outproj_ln_a")
    h = _ffn_ln(h.reshape(n_tok, d), bf(ffn_w13[0]), bf(ffn_w2[0]), ln_g[0, 1], ln_b[0, 1], tm=512)

    layer = 1
    lam_init = 0.8 - 0.6 * math.exp(-0.3 * layer)
    w_cat = bf(jnp.concatenate([w_kv_shared[:, DIFF_QK_W:], b_w_q[0]], axis=1))
    w_k_t = bf(w_kv_shared[:, :DIFF_QK_W].T)
    v_sh, k_t, q_diff, q_mem = _proj_kvq(h, w_cat, w_k_t, bsz, t, tm=512)
    attn = _diffattn(q_diff.reshape(bsz, t, DIFF_QK_W), k_t, v_sh.reshape(bsz, t, DIFF_V_W),
                     b_lambda[0], b_subln_g[0], lam_init, tq=256)
    h = _outproj_ln(attn, q_mem.reshape(bsz, t, MEM_W), kv_mem[1], bf(b_w_out[0]),
                    h.reshape(bsz, t, d), ln_g[1, 0], ln_b[1, 0], tm=512, name="outproj_ln_b")
    out = _moe_layer(h.reshape(n_tok, d), moe_router[0], bf(moe_w13[0]), bf(moe_w2[0]),
                     ln_g[1, 1], ln_b[1, 1])
    return out.reshape(bsz, t, d)
```

```python
import functools
import math

import jax
import jax.numpy as jnp
from jax import lax
from jax.experimental import pallas as pl
from jax.experimental.pallas import tpu as pltpu

F32 = jnp.float32
BF16 = jnp.bfloat16

D_MODEL = 1024
DEPTH = 2
D_RNN = D_MODEL
RNN_BLOCKS = 8
RNN_BLOCK_W = D_RNN // RNN_BLOCKS
CONV_W = 4
LRU_C = 8.0
MEM_HEADS = 4
MEM_HEAD_DIM = D_MODEL // 8
MEM_W = MEM_HEADS * MEM_HEAD_DIM
DIFF_HEADS = 8
DIFF_HEAD_DIM = D_MODEL // 16
DIFF_V_DIM = 2 * DIFF_HEAD_DIM
DIFF_QK_W = DIFF_HEADS * 2 * DIFF_HEAD_DIM
DIFF_V_W = DIFF_HEADS * DIFF_V_DIM
FFN_DIM = (7 * D_MODEL) // 2
N_EXPERTS = 8
TOP_K = 2
LN_EPS = 1e-5
DEEPNORM_ALPHA = (2.0 * DEPTH) ** 0.25

LANES = 128
SUBLANES = 8
VMEM_LIMIT = 52 * 1024 * 1024

FFN_TILE = 1792
MOE_ROWS = 512
NEG_BIG = -1e30
LOG2E = math.log2(math.e)


def _params(semantics):
    return pltpu.CompilerParams(dimension_semantics=semantics, vmem_limit_bytes=VMEM_LIMIT)


def _layer_norm(z, g, b):
    mu = jnp.mean(z, axis=-1, keepdims=True)
    zc = z - mu
    var = jnp.mean(zc * zc, axis=-1, keepdims=True)
    return zc * lax.rsqrt(var + LN_EPS) * g + b


def _proj_body(x_ref, w_ref, *o_refs, splits, post):
    xb = x_ref[...].astype(BF16)
    for o_ref, (c0, c1), fn in zip(o_refs, splits, post):
        z = jnp.dot(xb, w_ref[:, c0:c1], preferred_element_type=F32)
        if fn is not None:
            z = fn(z)
        o_ref[...] = z.astype(o_ref.dtype)


def _proj(x, w, splits, post, out_dtypes, tm, name):
    m, k = x.shape
    return pl.pallas_call(
        functools.partial(_proj_body, splits=tuple(splits), post=tuple(post)),
        grid=(m // tm,),
        in_specs=[pl.BlockSpec((tm, k), lambda i: (i, 0)),
                  pl.BlockSpec(w.shape, lambda i: (0, 0))],
        out_specs=[pl.BlockSpec((tm, c1 - c0), lambda i: (i, 0)) for c0, c1 in splits],
        out_shape=[jax.ShapeDtypeStruct((m, c1 - c0), dt) for (c0, c1), dt in zip(splits, out_dtypes)],
        compiler_params=_params(("parallel",)),
        name=name,
    )(x, w)


def _proj_kvq_body(x_ref, w_ref, wkt_ref, v_ref, kt_ref, qd_ref, qm_ref):
    xb = x_ref[...].astype(BF16)
    q_scale = DIFF_HEAD_DIM ** -0.5 * LOG2E
    v_ref[...] = jnp.dot(xb, w_ref[:, 0:DIFF_V_W], preferred_element_type=F32).astype(BF16)
    kt_ref[0] = lax.dot_general(wkt_ref[...], xb, (((1,), (1,)), ((), ())),
                                preferred_element_type=F32).astype(BF16)
    qd = jnp.dot(xb, w_ref[:, DIFF_V_W:DIFF_V_W + DIFF_QK_W], preferred_element_type=F32)
    qd_ref[...] = (qd * q_scale).astype(BF16)
    qm_ref[...] = jnp.dot(xb, w_ref[:, DIFF_V_W + DIFF_QK_W:], preferred_element_type=F32).astype(BF16)


def _proj_kvq(h, w_cat, w_k_t, bsz, t, tm):
    m, d = h.shape
    nt = t // tm
    rows = lambda width: pl.BlockSpec((tm, width), lambda i: (i, 0))
    return pl.pallas_call(
        _proj_kvq_body,
        grid=(m // tm,),
        in_specs=[rows(d), pl.BlockSpec(w_cat.shape, lambda i: (0, 0)),
                  pl.BlockSpec(w_k_t.shape, lambda i: (0, 0))],
        out_specs=[rows(DIFF_V_W), pl.BlockSpec((1, DIFF_QK_W, tm), lambda i: (i // nt, 0, i % nt)),
                   rows(DIFF_QK_W), rows(MEM_W)],
        out_shape=[jax.ShapeDtypeStruct((m, DIFF_V_W), BF16),
                   jax.ShapeDtypeStruct((bsz, DIFF_QK_W, t), BF16),
                   jax.ShapeDtypeStruct((m, DIFF_QK_W), BF16),
                   jax.ShapeDtypeStruct((m, MEM_W), BF16)],
        compiler_params=_params(("parallel",)),
        name="proj_kvq",
    )(h, w_cat, w_k_t)


def _rglru_body(u_ref, g_ref, cw_ref, cb_ref, wri_ref, br_ref, bi_ref, lam_ref, o_ref,
                ext_ref, rec_ref, a_ref, b_ref, h_ref, *, tt):
    t = pl.program_id(1)

    @pl.when(t == 0)
    def _():
        ext_ref[0:SUBLANES, :] = jnp.zeros((SUBLANES, D_RNN), F32)
        h_ref[...] = jnp.zeros((SUBLANES, D_RNN), F32)

    ext_ref[SUBLANES:, :] = u_ref[0].astype(F32)
    rec = cb_ref[...] + cw_ref[CONV_W - 1:CONV_W, :] * ext_ref[SUBLANES:SUBLANES + tt, :]
    for j in range(CONV_W - 1):
        off = SUBLANES - (CONV_W - 1) + j
        rec = rec + cw_ref[j:j + 1, :] * ext_ref[off:off + tt, :]
    rec_ref[...] = rec
    ext_ref[0:SUBLANES, :] = ext_ref[tt:tt + SUBLANES, :]

    sp = jax.nn.softplus(-lam_ref[...])
    first_row = (lax.broadcasted_iota(jnp.int32, (tt, RNN_BLOCK_W), 0) + t * tt) == 0
    for n in range(RNN_BLOCKS):
        blk = slice(n * RNN_BLOCK_W, (n + 1) * RNN_BLOCK_W)
        u_n = rec_ref[:, blk]
        ri = jnp.dot(u_n.astype(BF16), wri_ref[n], preferred_element_type=F32)
        r = jax.nn.sigmoid(ri[:, :RNN_BLOCK_W] + br_ref[:, blk])
        i = jax.nn.sigmoid(ri[:, RNN_BLOCK_W:] + bi_ref[:, blk])
        log_a = -LRU_C * r * sp[:, blk]
        a = jnp.exp(log_a)
        th = jnp.tanh(-log_a)
        m2 = 2.0 * th / (1.0 + th)
        mult = jnp.where(m2 > 0.0, m2 * lax.rsqrt(m2), 0.0)
        mult = jnp.where(first_row, 1.0, mult)
        a_ref[:, blk] = a
        b_ref[:, blk] = mult * (i * u_n)

    row = lax.broadcasted_iota(jnp.int32, (SUBLANES, D_RNN), 0)

    def tile(i, h_prev):
        r0 = pl.multiple_of(i * SUBLANES, SUBLANES)
        a = a_ref[pl.ds(r0, SUBLANES), :]
        b = b_ref[pl.ds(r0, SUBLANES), :]
        for s in (1, 2, 4):
            a_s = pltpu.roll(a, s, 0)
            b_s = pltpu.roll(b, s, 0)
            keep = row >= s
            b = jnp.where(keep, a * b_s + b, b)
            a = jnp.where(keep, a * a_s, a)
        h = b + a * h_prev
        b_ref[pl.ds(r0, SUBLANES), :] = h
        return jnp.broadcast_to(h[SUBLANES - 1:SUBLANES, :], (SUBLANES, D_RNN))

    h_ref[...] = lax.fori_loop(0, tt // SUBLANES, tile, h_ref[...])
    o_ref[0] = (b_ref[...] * g_ref[0].astype(F32)).astype(o_ref.dtype)


def _rglru(u_pre, gate, conv_w, conv_b, w_ri, b_r, b_i, lam, tt):
    bsz, t, c = u_pre.shape
    row = lambda a: a.reshape(1, c)
    full = lambda shape: pl.BlockSpec(shape, lambda b, i: (0,) * len(shape))
    return pl.pallas_call(
        functools.partial(_rglru_body, tt=tt),
        grid=(bsz, t // tt),
        in_specs=[pl.BlockSpec((1, tt, c), lambda b, i: (b, i, 0)),
                  pl.BlockSpec((1, tt, c), lambda b, i: (b, i, 0)),
                  full((CONV_W, c)), full((1, c)), full(w_ri.shape),
                  full((1, c)), full((1, c)), full((1, c))],
        out_specs=pl.BlockSpec((1, tt, c), lambda b, i: (b, i, 0)),
        out_shape=jax.ShapeDtypeStruct((bsz, t, c), BF16),
        scratch_shapes=[pltpu.VMEM((tt + SUBLANES, c), F32), pltpu.VMEM((tt, c), F32),
                        pltpu.VMEM((tt, c), F32), pltpu.VMEM((tt, c), F32),
                        pltpu.VMEM((SUBLANES, c), F32)],
        compiler_params=_params(("parallel", "arbitrary")),
        name="rglru",
    )(u_pre, gate, conv_w, row(conv_b), w_ri, row(b_r), row(b_i), row(lam))


def _outproj_ln_body(a_ref, qm_ref, kv_ref, w_ref, res_ref, g_ref, b_ref, o_ref):
    mix_w = a_ref.shape[-1]
    y = jnp.dot(a_ref[0], w_ref[0:mix_w, :], preferred_element_type=F32)
    scale = MEM_HEAD_DIM ** -0.5
    heads = []
    for h in range(MEM_HEADS):
        hs = slice(h * MEM_HEAD_DIM, (h + 1) * MEM_HEAD_DIM)
        vs = slice(MEM_W + h * MEM_HEAD_DIM, MEM_W + (h + 1) * MEM_HEAD_DIM)
        s = lax.dot_general(qm_ref[0, :, hs], kv_ref[0, :, hs], (((1,), (1,)), ((), ())),
                            preferred_element_type=F32) * scale
        e = jnp.exp(s - jnp.max(s, axis=-1, keepdims=True))
        p = e / jnp.sum(e, axis=-1, keepdims=True)
        heads.append(jnp.dot(p.astype(BF16), kv_ref[0, :, vs], preferred_element_type=F32))
    mem_out = jnp.concatenate(heads, axis=1).astype(BF16)
    y = y + jnp.dot(mem_out, w_ref[mix_w:, :], preferred_element_type=F32)
    z = DEEPNORM_ALPHA * res_ref[0] + y
    o_ref[0] = _layer_norm(z, g_ref[...], b_ref[...])


def _outproj_ln(mix, q_mem, kv_mem, w_out, resid, g, b, tm, name):
    bsz, t, mix_w = mix.shape
    n_mem = kv_mem.shape[1]
    return pl.pallas_call(
        _outproj_ln_body,
        grid=(bsz, t // tm),
        in_specs=[pl.BlockSpec((1, tm, mix_w), lambda b, i: (b, i, 0)),
                  pl.BlockSpec((1, tm, MEM_W), lambda b, i: (b, i, 0)),
                  pl.BlockSpec((1, n_mem, 2 * MEM_W), lambda b, i: (b, 0, 0)),
                  pl.BlockSpec(w_out.shape, lambda b, i: (0, 0)),
                  pl.BlockSpec((1, tm, D_MODEL), lambda b, i: (b, i, 0)),
                  pl.BlockSpec((1, D_MODEL), lambda b, i: (0, 0)),
                  pl.BlockSpec((1, D_MODEL), lambda b, i: (0, 0))],
        out_specs=pl.BlockSpec((1, tm, D_MODEL), lambda b, i: (b, i, 0)),
        out_shape=jax.ShapeDtypeStruct((bsz, t, D_MODEL), F32),
        compiler_params=_params(("parallel", "parallel")),
        name=name,
    )(mix, q_mem, kv_mem, w_out, resid, g.reshape(1, -1), b.reshape(1, -1))


def _ffn_ln_body(h_ref, w1_ref, w3_ref, w2_ref, g_ref, b_ref, o_ref, acc_ref):
    f = pl.program_id(1)

    @pl.when(jnp.logical_and(pl.program_id(0) == 0, f == 0))
    def _():
        acc_ref[...] = jnp.zeros_like(acc_ref)

    h = h_ref[...]
    hb = h.astype(BF16)
    gate = jnp.dot(hb, w1_ref[...], preferred_element_type=F32)
    up = jnp.dot(hb, w3_ref[...], preferred_element_type=F32)
    act = (jax.nn.silu(gate) * up).astype(BF16)
    acc = jnp.where(f == 0, 0.0, acc_ref[...]) + jnp.dot(act, w2_ref[...], preferred_element_type=F32)
    acc_ref[...] = acc
    o_ref[...] = _layer_norm(DEEPNORM_ALPHA * h + acc, g_ref[...], b_ref[...])


def _ffn_ln(h, w13, w2, g, b, tm):
    m, d = h.shape
    nf = FFN_DIM // FFN_TILE
    return pl.pallas_call(
        _ffn_ln_body,
        grid=(m // tm, nf),
        in_specs=[pl.BlockSpec((tm, d), lambda i, f: (i, 0)),
                  pl.BlockSpec((d, FFN_TILE), lambda i, f: (0, f)),
                  pl.BlockSpec((d, FFN_TILE), lambda i, f: (0, nf + f)),
                  pl.BlockSpec((FFN_TILE, d), lambda i, f: (f, 0)),
                  pl.BlockSpec((1, d), lambda i, f: (0, 0)),
                  pl.BlockSpec((1, d), lambda i, f: (0, 0))],
        out_specs=pl.BlockSpec((tm, d), lambda i, f: (i, 0)),
        out_shape=jax.ShapeDtypeStruct((m, d), F32),
        scratch_shapes=[pltpu.VMEM((tm, d), F32)],
        compiler_params=_params(("arbitrary", "arbitrary")),
        name="ffn_ln",
    )(h, w13, w13, w2, g.reshape(1, -1), b.reshape(1, -1))


ATT_STRIP = 128


def _diffattn_body(pi_ref, pj_ref, lamv_ref, sub_ref, coef_ref, q_ref, kt_ref, v_ref, o_ref,
                   kaug_ref, vaug_ref, qq_ref, mask_ref, s_ref, p_ref, c_ref, m_ref, acc_ref,
                   *, tq, n_pairs, unroll, lam_init):
    t = v_ref.shape[1]
    hd = 2 * DIFF_HEAD_DIM
    n_strips = 2 * tq // ATT_STRIP

    feat_row = lax.broadcasted_iota(jnp.int32, (LANES, tq), 0)
    in_block = lax.broadcasted_iota(jnp.int32, (LANES, tq), 1)
    for kb in range(t // tq):
        feat = jnp.where(feat_row < 3, kb, jnp.where(feat_row < 6, in_block, 0))
        kaug_ref[kb, 0:hd, :] = kt_ref[0, :, kb * tq:(kb + 1) * tq]
        kaug_ref[kb, hd:, :] = feat.astype(F32).astype(BF16)
    vaug_ref[:, 0:DIFF_V_DIM] = v_ref[0]
    vaug_ref[:, DIFF_V_DIM:] = jnp.ones((t, LANES), BF16)
    lane = lax.broadcasted_iota(jnp.int32, (tq, hd), 1)
    coef_rows = jnp.broadcast_to(coef_ref[0], (2 * tq, LANES))
    for qb in range(t // tq):
        q = q_ref[0, qb * tq:(qb + 1) * tq, :]
        zero = jnp.zeros_like(q)
        qq_ref[qb * 2 * tq:qb * 2 * tq + tq, 0:hd] = jnp.where(lane < DIFF_HEAD_DIM, q, zero)
        qq_ref[qb * 2 * tq + tq:(qb + 1) * 2 * tq, 0:hd] = jnp.where(lane < DIFF_HEAD_DIM, zero, q)
        qq_ref[qb * 2 * tq:(qb + 1) * 2 * tq, hd:] = coef_rows
    q_row = lax.broadcasted_iota(jnp.int32, (tq, tq), 0)
    k_col = lax.broadcasted_iota(jnp.int32, (tq, tq), 1)
    mask_ref[0] = jnp.zeros((tq, tq), F32)
    mask_ref[1] = jnp.where(k_col <= q_row, 0.0, NEG_BIG)
    m_ref[...] = jnp.full(m_ref.shape, NEG_BIG, F32)
    acc_ref[...] = jnp.zeros_like(acc_ref)

    def stage_qk(p, slot):
        q0 = pl.multiple_of(pi_ref[p] * (2 * tq), 2 * tq)
        s_ref[slot] = jnp.dot(qq_ref[pl.ds(q0, 2 * tq), :], kaug_ref[pj_ref[p]],
                              preferred_element_type=F32)

    def stage_sm(p, slot):
        i = pi_ref[p]
        diag = (i == pj_ref[p]).astype(jnp.int32)
        for r in range(n_strips):
            rows = slice(r * ATT_STRIP, (r + 1) * ATT_STRIP)
            mrows = slice((r * ATT_STRIP) % tq, (r * ATT_STRIP) % tq + ATT_STRIP)
            s = s_ref[slot, rows, :] + mask_ref[diag, mrows, :]
            m_prev = m_ref[i, rows, :]
            m_next = jnp.maximum(m_prev, jnp.max(s, axis=-1, keepdims=True))
            p_blk = jnp.exp2(s - jnp.tile(m_next, (1, tq // LANES)))
            p_ref[slot, rows, :] = p_blk.astype(BF16)
            c_ref[slot, rows, :] = jnp.exp2(m_prev - m_next)
            m_ref[i, rows, :] = m_next

    def stage_pv(p, slot):
        i = pi_ref[p]
        k0 = pl.multiple_of(pj_ref[p] * tq, tq)
        pv = jnp.dot(p_ref[slot], vaug_ref[pl.ds(k0, tq), :], preferred_element_type=F32)
        acc_ref[i] = acc_ref[i] * jnp.tile(c_ref[slot], (1, 2)) + pv

    stage_qk(0, 0)
    stage_qk(1, 1)
    stage_sm(0, 0)

    def pipeline_step(step, c):
        for u in range(unroll):
            p = unroll * step + u
            stage_qk(p + 2, u % 2)
            stage_sm(p + 1, (u + 1) % 2)
            stage_pv(p, u % 2)
        return c

    lax.fori_loop(0, n_pairs // unroll, pipeline_step, 0)

    lv = lamv_ref[...]
    lam = (jnp.exp(jnp.sum(lv[0:1] * lv[1:2], axis=-1, keepdims=True))
           - jnp.exp(jnp.sum(lv[2:3] * lv[3:4], axis=-1, keepdims=True)) + lam_init)

    def finalize(i, c):
        o = acc_ref[i, :, 0:DIFF_V_DIM] / acc_ref[i, :, DIFF_V_DIM:]
        o = o[:tq] - lam * o[tq:]
        o = o * lax.rsqrt(jnp.mean(o * o, axis=-1, keepdims=True) + LN_EPS) * sub_ref[...]
        r0 = pl.multiple_of(i * tq, tq)
        o_ref[0, pl.ds(r0, tq), :] = (o * (1.0 - lam_init)).astype(o_ref.dtype)
        return c

    lax.fori_loop(0, t // tq, finalize, 0, unroll=2)


def _diffattn(q, k_t, v, lam_vecs, subln_g, lam_init, tq):
    bsz, t, _ = q.shape
    slopes = jnp.exp2(-8.0 * (jnp.arange(DIFF_HEADS, dtype=F32) + 1.0) / DIFF_HEADS) * LOG2E
    def pieces(c):
        c1 = c.astype(BF16)
        c2 = (c - c1.astype(F32)).astype(BF16)
        c3 = (c - c1.astype(F32) - c2.astype(F32)).astype(BF16)
        return [c1, c2, c3]

    coef = jnp.zeros((DIFF_HEADS, 1, LANES), BF16)
    coef = coef.at[:, 0, 0:6].set(jnp.stack(pieces(slopes * tq) + pieces(slopes), axis=1))
    hd = 2 * DIFF_HEAD_DIM
    nq = t // tq
    assert nq <= 256, "key block indices must be exact in bf16"
    pairs = [(i, j) for i in range(nq) for j in range(i + 1)]
    n_pairs = len(pairs)
    unroll = next(u for u in (8, 4, 2) if n_pairs % u == 0)
    assert n_pairs % unroll == 0
    pairs = pairs + [pairs[-1]] * 2
    pair_i = jnp.asarray([p[0] for p in pairs], jnp.int32)
    pair_j = jnp.asarray([p[1] for p in pairs], jnp.int32)
    head = lambda shape: pl.BlockSpec(shape, lambda b, h, pi, pj: (b, 0, h))
    return pl.pallas_call(
        functools.partial(_diffattn_body, tq=tq, n_pairs=n_pairs, unroll=unroll, lam_init=lam_init),
        grid_spec=pltpu.PrefetchScalarGridSpec(
            num_scalar_prefetch=2,
            grid=(bsz, DIFF_HEADS),
            in_specs=[pl.BlockSpec(lam_vecs.shape, lambda b, h, pi, pj: (0, 0)),
                      pl.BlockSpec((1, DIFF_V_DIM), lambda b, h, pi, pj: (0, 0)),
                      pl.BlockSpec((1, 1, LANES), lambda b, h, pi, pj: (h, 0, 0)),
                      head((1, t, hd)),
                      pl.BlockSpec((1, hd, t), lambda b, h, pi, pj: (b, h, 0)),
                      head((1, t, DIFF_V_DIM))],
            out_specs=head((1, t, DIFF_V_DIM)),
            scratch_shapes=[pltpu.VMEM((nq, hd + LANES, tq), BF16),
                            pltpu.VMEM((t, DIFF_V_DIM + LANES), BF16),
                            pltpu.VMEM((2 * t, hd + LANES), BF16),
                            pltpu.VMEM((2, tq, tq), F32),
                            pltpu.VMEM((2, 2 * tq, tq), F32),
                            pltpu.VMEM((2, 2 * tq, tq), BF16),
                            pltpu.VMEM((2, 2 * tq, LANES), F32),
                            pltpu.VMEM((nq, 2 * tq, LANES), F32),
                            pltpu.VMEM((nq, 2 * tq, DIFF_V_DIM + LANES), F32)]),
        out_shape=jax.ShapeDtypeStruct((bsz, t, DIFF_V_W), BF16),
        compiler_params=_params(("parallel", "parallel")),
        name="diffattn",
    )(pair_i, pair_j, lam_vecs, subln_g.reshape(1, -1), coef, q, k_t, v)


def _router_body(h_ref, w_ref, o_ref, cnt_ref, carry_ref, *, tm):
    @pl.when(pl.program_id(0) == 0)
    def _():
        carry_ref[...] = jnp.zeros_like(carry_ref)

    h = h_ref[...]
    w = w_ref[...]
    h_hi = h.astype(BF16)
    h_lo = (h - h_hi.astype(F32)).astype(BF16)
    w_hi = w.astype(BF16)
    w_lo = (w - w_hi.astype(F32)).astype(BF16)
    logits = (jnp.dot(h_hi, w_hi, preferred_element_type=F32)
              + (jnp.dot(h_hi, w_lo, preferred_element_type=F32)
                 + jnp.dot(h_lo, w_hi, preferred_element_type=F32)))
    lane = lax.broadcasted_iota(jnp.int32, (tm, LANES), 1)
    lg = jnp.where(lane < N_EXPERTS, logits, -jnp.inf)
    v1 = jnp.max(lg, axis=-1, keepdims=True)
    i1 = jnp.min(jnp.where(lg == v1, lane, LANES), axis=-1, keepdims=True)
    oh1 = lane == i1
    lg2 = jnp.where(oh1, -jnp.inf, lg)
    v2 = jnp.max(lg2, axis=-1, keepdims=True)
    i2 = jnp.min(jnp.where(lg2 == v2, lane, LANES), axis=-1, keepdims=True)
    oh2 = lane == i2
    e2 = jnp.exp(v2 - v1)
    g1 = 1.0 / (1.0 + e2)
    g2 = e2 / (1.0 + e2)

    both = jnp.where(oh1, 1.0, jnp.where(oh2, 1.0, 0.0))
    tri = (lax.broadcasted_iota(jnp.int32, (tm, tm), 0)
           > lax.broadcasted_iota(jnp.int32, (tm, tm), 1))
    tri = jnp.where(tri, 1.0, 0.0).astype(BF16)
    before = jnp.dot(tri, both.astype(BF16), preferred_element_type=F32) + carry_ref[0:1, :]
    rank1 = jnp.sum(jnp.where(oh1, before, 0.0), axis=-1, keepdims=True)
    rank2 = jnp.sum(jnp.where(oh2, before, 0.0), axis=-1, keepdims=True)
    total = carry_ref[...] + jnp.sum(both, axis=0, keepdims=True)
    carry_ref[...] = total
    cnt_ref[...] = total

    out = jnp.where(lane == 0, i1.astype(F32), 0.0)
    out = jnp.where(lane == 1, i2.astype(F32), out)
    out = jnp.where(lane == 2, rank1, out)
    out = jnp.where(lane == 3, rank2, out)
    out = jnp.where(lane == 4, g1, out)
    out = jnp.where(lane == 5, g2, out)
    o_ref[...] = out


def _router(h, w_router, tm):
    m, d = h.shape
    w = jnp.zeros((d, LANES), F32).at[:, :N_EXPERTS].set(w_router)
    return pl.pallas_call(
        functools.partial(_router_body, tm=tm),
        grid=(m // tm,),
        in_specs=[pl.BlockSpec((tm, d), lambda i: (i, 0)),
                  pl.BlockSpec((d, LANES), lambda i: (0, 0))],
        out_specs=[pl.BlockSpec((tm, LANES), lambda i: (i, 0)),
                   pl.BlockSpec((SUBLANES, LANES), lambda i: (0, 0))],
        out_shape=[jax.ShapeDtypeStruct((m, LANES), F32),
                   jax.ShapeDtypeStruct((SUBLANES, LANES), F32)],
        scratch_shapes=[pltpu.VMEM((SUBLANES, LANES), F32)],
        compiler_params=_params(("arbitrary",)),
        name="router",
    )(h, w)


ROW_TILE = D_MODEL // LANES


def _row_tile(ref, r):
    start = r * ROW_TILE if isinstance(r, int) else pl.multiple_of(r * ROW_TILE, ROW_TILE)
    return ref.at[pl.ds(start, ROW_TILE), :]


def _store_row_tiles(tile_ref, x):
    for j in range(ROW_TILE):
        tile_ref[pl.ds(j, x.shape[0], stride=ROW_TILE), :] = x[:, j * LANES:(j + 1) * LANES]


def _load_row_tiles(tile_ref, n):
    return jnp.concatenate([tile_ref[pl.ds(j, n, stride=ROW_TILE), :] for j in range(ROW_TILE)], axis=1)


def _dispatch_body(dest_ref, meta_ref, h_ref, buf_ref, rows_ref, zero_ref, sem, *, tm, n_tok, n_rows):
    i = pl.program_id(0)
    base = i * tm
    _store_row_tiles(rows_ref, h_ref[...])

    def row_copy(r, d):
        return pltpu.make_async_copy(_row_tile(rows_ref, r), _row_tile(buf_ref, d), sem)

    def issue(r, c):
        row_copy(r, dest_ref[base + r]).start(priority=0)
        row_copy(r, dest_ref[n_tok + base + r]).start(priority=1)
        return c

    lax.fori_loop(0, tm, issue, 0, unroll=16)
    all_rows = buf_ref.at[pl.ds(0, TOP_K * tm * ROW_TILE), :]
    pltpu.make_async_copy(all_rows, all_rows, sem).wait()

    @pl.when(i == pl.num_programs(0) - 1)
    def _():
        zero_ref[...] = jnp.zeros_like(zero_ref)

        def zero_row(r):
            return pltpu.make_async_copy(_row_tile(zero_ref, 0), _row_tile(buf_ref, r), sem)

        def start_zero_row(r, c):
            zero_row(r).start()
            return c

        def wait_zero_row(r, c):
            zero_row(0).wait()
            return c

        for e in range(N_EXPERTS):
            lo = meta_ref[e] + meta_ref[N_EXPERTS + e]
            hi = meta_ref[e] + meta_ref[2 * N_EXPERTS + e]
            lax.fori_loop(lo, hi, start_zero_row, 0)
            lax.fori_loop(lo, hi, wait_zero_row, 0)

        def zero_block(b, c):
            r0 = pl.multiple_of(b * (MOE_ROWS * ROW_TILE), MOE_ROWS * ROW_TILE)
            cp = pltpu.make_async_copy(zero_ref, buf_ref.at[pl.ds(r0, MOE_ROWS * ROW_TILE), :], sem)
            cp.start()
            cp.wait()
            return c

        used = meta_ref[N_EXPERTS - 1] + meta_ref[3 * N_EXPERTS - 1]
        lax.fori_loop(used // MOE_ROWS, n_rows // MOE_ROWS, zero_block, 0)


def _dispatch(dest, meta, h, n_rows, tm):
    m, d = h.shape
    return pl.pallas_call(
        functools.partial(_dispatch_body, tm=tm, n_tok=m, n_rows=n_rows),
        grid_spec=pltpu.PrefetchScalarGridSpec(
            num_scalar_prefetch=2,
            grid=(m // tm,),
            in_specs=[pl.BlockSpec((tm, d), lambda i, dest, meta: (i, 0))],
            out_specs=pl.BlockSpec(memory_space=pl.ANY),
            scratch_shapes=[pltpu.VMEM((tm * ROW_TILE, LANES), F32),
                            pltpu.VMEM((MOE_ROWS * ROW_TILE, LANES), F32),
                            pltpu.SemaphoreType.DMA(())]),
        out_shape=jax.ShapeDtypeStruct((n_rows * ROW_TILE, LANES), F32),
        compiler_params=_params(("arbitrary",)),
        name="moe_dispatch",
    )(dest, meta, h)


def _moe_body(be_ref, nv_ref, x_ref, w1_ref, w3_ref, w2_ref, o_ref, acc_ref):
    b = pl.program_id(0)
    f = pl.program_id(1)

    @pl.when(jnp.logical_and(b == 0, f == 0))
    def _():
        acc_ref[...] = jnp.zeros_like(acc_ref)

    @pl.when(b < nv_ref[0])
    def _():
        xb = _load_row_tiles(x_ref, MOE_ROWS).astype(BF16)
        gate = jnp.dot(xb, w1_ref[0], preferred_element_type=F32)
        up = jnp.dot(xb, w3_ref[0], preferred_element_type=F32)
        act = (jax.nn.silu(gate) * up).astype(BF16)
        acc = jnp.where(f == 0, 0.0, acc_ref[...]) + jnp.dot(act, w2_ref[0], preferred_element_type=F32)
        acc_ref[...] = acc
        _store_row_tiles(o_ref, acc)

    @pl.when(jnp.logical_and(b >= nv_ref[0], f == pl.num_programs(1) - 1))
    def _():
        o_ref[...] = jnp.zeros_like(o_ref)


def _moe(block_e, n_valid, buf, w13, w2):
    d = D_MODEL
    n_rows = buf.shape[0] // ROW_TILE
    nb = n_rows // MOE_ROWS
    nf = FFN_DIM // FFN_TILE
    row_block = (MOE_ROWS * ROW_TILE, LANES)

    def blk(b, nv):
        return jnp.minimum(b, nv[0] - 1)

    def ftile(b, f, nv):
        return jnp.where(b < nv[0], f, nf - 1)

    return pl.pallas_call(
        _moe_body,
        grid_spec=pltpu.PrefetchScalarGridSpec(
            num_scalar_prefetch=2,
            grid=(nb, nf),
            in_specs=[pl.BlockSpec(row_block, lambda b, f, be, nv: (blk(b, nv), 0)),
                      pl.BlockSpec((1, d, FFN_TILE),
                                   lambda b, f, be, nv: (be[blk(b, nv)], 0, ftile(b, f, nv))),
                      pl.BlockSpec((1, d, FFN_TILE),
                                   lambda b, f, be, nv: (be[blk(b, nv)], 0, nf + ftile(b, f, nv))),
                      pl.BlockSpec((1, FFN_TILE, d),
                                   lambda b, f, be, nv: (be[blk(b, nv)], ftile(b, f, nv), 0))],
            out_specs=pl.BlockSpec(row_block, lambda b, f, be, nv: (b, 0)),
            scratch_shapes=[pltpu.VMEM((MOE_ROWS, d), F32)]),
        out_shape=jax.ShapeDtypeStruct((n_rows * ROW_TILE, LANES), F32),
        compiler_params=_params(("arbitrary", "arbitrary")),
        name="moe_experts",
    )(block_e, n_valid, buf, w13, w13, w2)


def _combine_ln_body(dest_ref, h_ref, route_ref, g_ref, b_ref, yb_ref, o_ref, rows_ref, sem,
                     *, tm, n_tok):
    i = pl.program_id(0)

    def gather_tile(tile, buf):
        base = tile * tm

        def issue(r, c):
            for k in range(TOP_K):
                pltpu.make_async_copy(_row_tile(yb_ref, dest_ref[k * n_tok + base + r]),
                                      _row_tile(rows_ref.at[buf, k], r), sem.at[buf]).start(priority=k)
            return c

        lax.fori_loop(0, tm, issue, 0, unroll=16)

    @pl.when(i == 0)
    def _():
        gather_tile(0, 0)

    @pl.when(i + 1 < pl.num_programs(0))
    def _():
        gather_tile(i + 1, (i + 1) % 2)

    buf = i % 2
    pltpu.make_async_copy(rows_ref.at[buf], rows_ref.at[buf], sem.at[buf]).wait()
    route = route_ref[...]
    y = (route[:, 4:5] * _load_row_tiles(rows_ref.at[buf, 0], tm)
         + route[:, 5:6] * _load_row_tiles(rows_ref.at[buf, 1], tm))
    z = DEEPNORM_ALPHA * h_ref[...] + y
    o_ref[...] = _layer_norm(z, g_ref[...], b_ref[...])


def _combine_ln(dest, h, route, yb, g, b, tm):
    m, d = h.shape
    return pl.pallas_call(
        functools.partial(_combine_ln_body, tm=tm, n_tok=m),
        grid_spec=pltpu.PrefetchScalarGridSpec(
            num_scalar_prefetch=1,
            grid=(m // tm,),
            in_specs=[pl.BlockSpec((tm, d), lambda i, dest: (i, 0)),
                      pl.BlockSpec((tm, LANES), lambda i, dest: (i, 0)),
                      pl.BlockSpec((1, d), lambda i, dest: (0, 0)),
                      pl.BlockSpec((1, d), lambda i, dest: (0, 0)),
                      pl.BlockSpec(memory_space=pl.ANY)],
            out_specs=pl.BlockSpec((tm, d), lambda i, dest: (i, 0)),
            scratch_shapes=[pltpu.VMEM((2, TOP_K, tm * ROW_TILE, LANES), F32),
                            pltpu.SemaphoreType.DMA((2,))]),
        out_shape=jax.ShapeDtypeStruct((m, d), F32),
        compiler_params=_params(("arbitrary",)),
        name="moe_combine_ln",
    )(dest, h, route, g.reshape(1, -1), b.reshape(1, -1), yb)


def _moe_layer(h, w_router, w13, w2, g, b):
    n_tok, _ = h.shape
    route, cnt = _router(h, w_router, tm=512)
    counts = cnt[0, :N_EXPERTS].astype(jnp.int32)
    padded = (counts + MOE_ROWS - 1) // MOE_ROWS * MOE_ROWS
    pstart = jnp.cumsum(padded) - padded
    experts = route[:, 0:2].astype(jnp.int32)
    ranks = route[:, 2:4].astype(jnp.int32)
    dest = (pstart[experts] + ranks).T.reshape(-1)
    n_rows = n_tok * TOP_K + N_EXPERTS * MOE_ROWS
    n_blocks = n_rows // MOE_ROWS
    block_end = jnp.cumsum(padded // MOE_ROWS)
    block_e = jnp.sum(jnp.arange(n_blocks)[:, None] >= block_end[None, :], axis=1)
    block_e = jnp.minimum(block_e, N_EXPERTS - 1).astype(jnp.int32)
    n_valid = block_end[-1:].astype(jnp.int32)
    meta = jnp.concatenate([pstart, counts, padded]).astype(jnp.int32)
    buf = _dispatch(dest, meta, h, n_rows, tm=512)
    yb = _moe(block_e, n_valid, buf, w13, w2)
    return _combine_ln(dest, h, route, yb, g, b, tm=512)


def kernel(x, mem, a_w_in, a_conv_w, a_conv_b, a_w_rgate, a_b_rgate, a_w_igate, a_b_igate, a_lambda,
           a_w_out, w_kv_shared, b_w_q, b_lambda, b_subln_g, b_w_out, mem_w_kv, ffn_w13, ffn_w2,
           moe_router, moe_w13, moe_w2, ln_g, ln_b):
    bsz, t, d = x.shape
    n_tok = bsz * t
    n_mem = mem.shape[1]
    bf = lambda a: a.astype(BF16)

    kv_mem = _proj(mem.reshape(bsz * n_mem, d), bf(jnp.concatenate([mem_w_kv[0], mem_w_kv[1]], axis=1)),
                   [(0, 2 * MEM_W), (2 * MEM_W, 4 * MEM_W)], [None, None], [BF16, BF16],
                   tm=min(1024, bsz * n_mem), name="mem_kv")
    kv_mem = [a.reshape(bsz, n_mem, 2 * MEM_W) for a in kv_mem]

    x2 = x.reshape(n_tok, d)
    gate, u_pre, q_mem = _proj(
        x2, bf(a_w_in[0]), [(0, D_RNN), (D_RNN, 2 * D_RNN), (2 * D_RNN, 2 * D_RNN + MEM_W)],
        [jax.nn.gelu, None, None], [BF16, BF16, BF16], tm=512, name="proj_in")
    w_ri = bf(jnp.concatenate([a_w_rgate[0], a_w_igate[0]], axis=-1))
    rnn = _rglru(u_pre.reshape(bsz, t, D_RNN), gate.reshape(bsz, t, D_RNN), a_conv_w[0], a_conv_b[0],
                 w_ri, a_b_rgate[0], a_b_igate[0], a_lambda[0], tt=512)
    h = _outproj_ln(rnn, q_mem.reshape(bsz, t, MEM_W), kv_mem[0], bf(a_w_out[0]), x,
                    ln_g[0, 0], ln_b[0, 0], tm=512, name="outproj_ln_a")
    h = _ffn_ln(h.reshape(n_tok, d), bf(ffn_w13[0]), bf(ffn_w2[0]), ln_g[0, 1], ln_b[0, 1], tm=512)

    layer = 1
    lam_init = 0.8 - 0.6 * math.exp(-0.3 * layer)
    w_cat = bf(jnp.concatenate([w_kv_shared[:, DIFF_QK_W:], b_w_q[0]], axis=1))
    w_k_t = bf(w_kv_shared[:, :DIFF_QK_W].T)
    v_sh, k_t, q_diff, q_mem = _proj_kvq(h, w_cat, w_k_t, bsz, t, tm=512)
    attn = _diffattn(q_diff.reshape(bsz, t, DIFF_QK_W), k_t, v_sh.reshape(bsz, t, DIFF_V_W),
                     b_lambda[0], b_subln_g[0], lam_init, tq=256)
    h = _outproj_ln(attn, q_mem.reshape(bsz, t, MEM_W), kv_mem[1], bf(b_w_out[0]),
                    h.reshape(bsz, t, d), ln_g[1, 0], ln_b[1, 0], tm=512, name="outproj_ln_b")
    out = _moe_layer(h.reshape(n_tok, d), moe_router[0], bf(moe_w13[0]), bf(moe_w2[0]),
                     ln_g[1, 1], ln_b[1, 1])
    return out.reshape(bsz, t, d)
```

```python
import functools
import math

import jax
import jax.numpy as jnp
from jax import lax
from jax.experimental import pallas as pl
from jax.experimental.pallas import tpu as pltpu

F32 = jnp.float32
BF16 = jnp.bfloat16

D_MODEL = 1024
DEPTH = 2
D_RNN = D_MODEL
RNN_BLOCKS = 8
RNN_BLOCK_W = D_RNN // RNN_BLOCKS
CONV_W = 4
LRU_C = 8.0
MEM_HEADS = 4
MEM_HEAD_DIM = D_MODEL // 8
MEM_W = MEM_HEADS * MEM_HEAD_DIM
DIFF_HEADS = 8
DIFF_HEAD_DIM = D_MODEL // 16
DIFF_V_DIM = 2 * DIFF_HEAD_DIM
DIFF_QK_W = DIFF_HEADS * 2 * DIFF_HEAD_DIM
DIFF_V_W = DIFF_HEADS * DIFF_V_DIM
FFN_DIM = (7 * D_MODEL) // 2
N_EXPERTS = 8
TOP_K = 2
LN_EPS = 1e-5
DEEPNORM_ALPHA = (2.0 * DEPTH) ** 0.25

LANES = 128
SUBLANES = 8
VMEM_LIMIT = 52 * 1024 * 1024

FFN_TILE = 1792
MOE_ROWS = 512
NEG_BIG = -1e30
LOG2E = math.log2(math.e)


def _params(semantics):
    return pltpu.CompilerParams(dimension_semantics=semantics, vmem_limit_bytes=VMEM_LIMIT)


def _layer_norm(z, g, b):
    mu = jnp.mean(z, axis=-1, keepdims=True)
    zc = z - mu
    var = jnp.mean(zc * zc, axis=-1, keepdims=True)
    return zc * lax.rsqrt(var + LN_EPS) * g + b


def _proj_body(x_ref, w_ref, *o_refs, splits, post):
    xb = x_ref[...].astype(BF16)
    for o_ref, (c0, c1), fn in zip(o_refs, splits, post):
        z = jnp.dot(xb, w_ref[:, c0:c1], preferred_element_type=F32)
        if fn is not None:
            z = fn(z)
        o_ref[...] = z.astype(o_ref.dtype)


def _proj(x, w, splits, post, out_dtypes, tm, name):
    m, k = x.shape
    return pl.pallas_call(
        functools.partial(_proj_body, splits=tuple(splits), post=tuple(post)),
        grid=(m // tm,),
        in_specs=[pl.BlockSpec((tm, k), lambda i: (i, 0)),
                  pl.BlockSpec(w.shape, lambda i: (0, 0))],
        out_specs=[pl.BlockSpec((tm, c1 - c0), lambda i: (i, 0)) for c0, c1 in splits],
        out_shape=[jax.ShapeDtypeStruct((m, c1 - c0), dt) for (c0, c1), dt in zip(splits, out_dtypes)],
        compiler_params=_params(("parallel",)),
        name=name,
    )(x, w)


def _proj_kvq_body(x_ref, w_ref, wkt_ref, v_ref, kt_ref, qd_ref, qm_ref):
    xb = x_ref[...].astype(BF16)
    q_scale = DIFF_HEAD_DIM ** -0.5 * LOG2E
    v_ref[...] = jnp.dot(xb, w_ref[:, 0:DIFF_V_W], preferred_element_type=F32).astype(BF16)
    kt_ref[0] = lax.dot_general(wkt_ref[...], xb, (((1,), (1,)), ((), ())),
                                preferred_element_type=F32).astype(BF16)
    qd = jnp.dot(xb, w_ref[:, DIFF_V_W:DIFF_V_W + DIFF_QK_W], preferred_element_type=F32)
    qd_ref[...] = (qd * q_scale).astype(BF16)
    qm_ref[...] = jnp.dot(xb, w_ref[:, DIFF_V_W + DIFF_QK_W:], preferred_element_type=F32).astype(BF16)


def _proj_kvq(h, w_cat, w_k_t, bsz, t, tm):
    m, d = h.shape
    nt = t // tm
    rows = lambda width: pl.BlockSpec((tm, width), lambda i: (i, 0))
    return pl.pallas_call(
        _proj_kvq_body,
        grid=(m // tm,),
        in_specs=[rows(d), pl.BlockSpec(w_cat.shape, lambda i: (0, 0)),
                  pl.BlockSpec(w_k_t.shape, lambda i: (0, 0))],
        out_specs=[rows(DIFF_V_W), pl.BlockSpec((1, DIFF_QK_W, tm), lambda i: (i // nt, 0, i % nt)),
                   rows(DIFF_QK_W), rows(MEM_W)],
        out_shape=[jax.ShapeDtypeStruct((m, DIFF_V_W), BF16),
                   jax.ShapeDtypeStruct((bsz, DIFF_QK_W, t), BF16),
                   jax.ShapeDtypeStruct((m, DIFF_QK_W), BF16),
                   jax.ShapeDtypeStruct((m, MEM_W), BF16)],
        compiler_params=_params(("parallel",)),
        name="proj_kvq",
    )(h, w_cat, w_k_t)


def _rglru_body(u_ref, g_ref, cw_ref, cb_ref, wri_ref, br_ref, bi_ref, lam_ref, o_ref,
                ext_ref, rec_ref, a_ref, b_ref, h_ref, *, tt):
    t = pl.program_id(1)

    @pl.when(t == 0)
    def _():
        ext_ref[0:SUBLANES, :] = jnp.zeros((SUBLANES, D_RNN), F32)
        h_ref[...] = jnp.zeros((SUBLANES, D_RNN), F32)

    ext_ref[SUBLANES:, :] = u_ref[0].astype(F32)
    rec = cb_ref[...] + cw_ref[CONV_W - 1:CONV_W, :] * ext_ref[SUBLANES:SUBLANES + tt, :]
    for j in range(CONV_W - 1):
        off = SUBLANES - (CONV_W - 1) + j
        rec = rec + cw_ref[j:j + 1, :] * ext_ref[off:off + tt, :]
    rec_ref[...] = rec
    ext_ref[0:SUBLANES, :] = ext_ref[tt:tt + SUBLANES, :]

    sp = jax.nn.softplus(-lam_ref[...])
    first_row = (lax.broadcasted_iota(jnp.int32, (tt, RNN_BLOCK_W), 0) + t * tt) == 0
    for n in range(RNN_BLOCKS):
        blk = slice(n * RNN_BLOCK_W, (n + 1) * RNN_BLOCK_W)
        u_n = rec_ref[:, blk]
        ri = jnp.dot(u_n.astype(BF16), wri_ref[n], preferred_element_type=F32)
        r = jax.nn.sigmoid(ri[:, :RNN_BLOCK_W] + br_ref[:, blk])
        i = jax.nn.sigmoid(ri[:, RNN_BLOCK_W:] + bi_ref[:, blk])
        log_a = -LRU_C * r * sp[:, blk]
        a = jnp.exp(log_a)
        th = jnp.tanh(-log_a)
        m2 = 2.0 * th / (1.0 + th)
        mult = jnp.where(m2 > 0.0, m2 * lax.rsqrt(m2), 0.0)
        mult = jnp.where(first_row, 1.0, mult)
        a_ref[:, blk] = a
        b_ref[:, blk] = mult * (i * u_n)

    row = lax.broadcasted_iota(jnp.int32, (SUBLANES, D_RNN), 0)

    def tile(i, h_prev):
        r0 = pl.multiple_of(i * SUBLANES, SUBLANES)
        a = a_ref[pl.ds(r0, SUBLANES), :]
        b = b_ref[pl.ds(r0, SUBLANES), :]
        for s in (1, 2, 4):
            a_s = pltpu.roll(a, s, 0)
            b_s = pltpu.roll(b, s, 0)
            keep = row >= s
            b = jnp.where(keep, a * b_s + b, b)
            a = jnp.where(keep, a * a_s, a)
        h = b + a * h_prev
        b_ref[pl.ds(r0, SUBLANES), :] = h
        return jnp.broadcast_to(h[SUBLANES - 1:SUBLANES, :], (SUBLANES, D_RNN))

    h_ref[...] = lax.fori_loop(0, tt // SUBLANES, tile, h_ref[...])
    o_ref[0] = (b_ref[...] * g_ref[0].astype(F32)).astype(o_ref.dtype)


def _rglru(u_pre, gate, conv_w, conv_b, w_ri, b_r, b_i, lam, tt):
    bsz, t, c = u_pre.shape
    row = lambda a: a.reshape(1, c)
    full = lambda shape: pl.BlockSpec(shape, lambda b, i: (0,) * len(shape))
    return pl.pallas_call(
        functools.partial(_rglru_body, tt=tt),
        grid=(bsz, t // tt),
        in_specs=[pl.BlockSpec((1, tt, c), lambda b, i: (b, i, 0)),
                  pl.BlockSpec((1, tt, c), lambda b, i: (b, i, 0)),
                  full((CONV_W, c)), full((1, c)), full(w_ri.shape),
                  full((1, c)), full((1, c)), full((1, c))],
        out_specs=pl.BlockSpec((1, tt, c), lambda b, i: (b, i, 0)),
        out_shape=jax.ShapeDtypeStruct((bsz, t, c), BF16),
        scratch_shapes=[pltpu.VMEM((tt + SUBLANES, c), F32), pltpu.VMEM((tt, c), F32),
                        pltpu.VMEM((tt, c), F32), pltpu.VMEM((tt, c), F32),
                        pltpu.VMEM((SUBLANES, c), F32)],
        compiler_params=_params(("parallel", "arbitrary")),
        name="rglru",
    )(u_pre, gate, conv_w, row(conv_b), w_ri, row(b_r), row(b_i), row(lam))


def _outproj_ln_body(a_ref, qm_ref, kv_ref, w_ref, res_ref, g_ref, b_ref, o_ref):
    mix_w = a_ref.shape[-1]
    y = jnp.dot(a_ref[0], w_ref[0:mix_w, :], preferred_element_type=F32)
    scale = MEM_HEAD_DIM ** -0.5 * LOG2E
    heads = []
    for h in range(MEM_HEADS):
        hs = slice(h * MEM_HEAD_DIM, (h + 1) * MEM_HEAD_DIM)
        vs = slice(MEM_W + h * MEM_HEAD_DIM, MEM_W + (h + 1) * MEM_HEAD_DIM)
        s = lax.dot_general(qm_ref[0, :, hs], kv_ref[0, :, hs], (((1,), (1,)), ((), ())),
                            preferred_element_type=F32)
        e = jnp.exp2((s - jnp.max(s, axis=-1, keepdims=True)) * scale)
        pv = jnp.dot(e.astype(BF16), kv_ref[0, :, vs], preferred_element_type=F32)
        heads.append(pv / jnp.sum(e, axis=-1, keepdims=True))
    mem_out = jnp.concatenate(heads, axis=1).astype(BF16)
    y = y + jnp.dot(mem_out, w_ref[mix_w:, :], preferred_element_type=F32)
    z = DEEPNORM_ALPHA * res_ref[0] + y
    o_ref[0] = _layer_norm(z, g_ref[...], b_ref[...])


def _outproj_ln(mix, q_mem, kv_mem, w_out, resid, g, b, tm, name):
    bsz, t, mix_w = mix.shape
    n_mem = kv_mem.shape[1]
    return pl.pallas_call(
        _outproj_ln_body,
        grid=(bsz, t // tm),
        in_specs=[pl.BlockSpec((1, tm, mix_w), lambda b, i: (b, i, 0)),
                  pl.BlockSpec((1, tm, MEM_W), lambda b, i: (b, i, 0)),
                  pl.BlockSpec((1, n_mem, 2 * MEM_W), lambda b, i: (b, 0, 0)),
                  pl.BlockSpec(w_out.shape, lambda b, i: (0, 0)),
                  pl.BlockSpec((1, tm, D_MODEL), lambda b, i: (b, i, 0)),
                  pl.BlockSpec((1, D_MODEL), lambda b, i: (0, 0)),
                  pl.BlockSpec((1, D_MODEL), lambda b, i: (0, 0))],
        out_specs=pl.BlockSpec((1, tm, D_MODEL), lambda b, i: (b, i, 0)),
        out_shape=jax.ShapeDtypeStruct((bsz, t, D_MODEL), F32),
        compiler_params=_params(("parallel", "parallel")),
        name=name,
    )(mix, q_mem, kv_mem, w_out, resid, g.reshape(1, -1), b.reshape(1, -1))


def _ffn_ln_body(h_ref, w13_ref, w2_ref, g_ref, b_ref, o_ref):
    h = h_ref[...]
    hb = h.astype(BF16)
    acc = None
    for c in range(FFN_DIM // FFN_TILE):
        cols = slice(c * FFN_TILE, (c + 1) * FFN_TILE)
        up_cols = slice(FFN_DIM + c * FFN_TILE, FFN_DIM + (c + 1) * FFN_TILE)
        gate = jnp.dot(hb, w13_ref[:, cols], preferred_element_type=F32)
        up = jnp.dot(hb, w13_ref[:, up_cols], preferred_element_type=F32)
        act = (jax.nn.silu(gate) * up).astype(BF16)
        part = jnp.dot(act, w2_ref[cols, :], preferred_element_type=F32)
        acc = part if acc is None else acc + part
    o_ref[...] = _layer_norm(DEEPNORM_ALPHA * h + acc, g_ref[...], b_ref[...])


def _ffn_ln(h, w13, w2, g, b, tm):
    m, d = h.shape
    resident = lambda shape: pl.BlockSpec(shape, lambda i: (0, 0), pipeline_mode=pl.Buffered(1))
    return pl.pallas_call(
        _ffn_ln_body,
        grid=(m // tm,),
        in_specs=[pl.BlockSpec((tm, d), lambda i: (i, 0)),
                  resident(w13.shape), resident(w2.shape),
                  pl.BlockSpec((1, d), lambda i: (0, 0)),
                  pl.BlockSpec((1, d), lambda i: (0, 0))],
        out_specs=pl.BlockSpec((tm, d), lambda i: (i, 0)),
        out_shape=jax.ShapeDtypeStruct((m, d), F32),
        compiler_params=_params(("parallel",)),
        name="ffn_ln",
    )(h, w13, w2, g.reshape(1, -1), b.reshape(1, -1))


ATT_STRIP = 128


def _diffattn_body(pi_ref, pj_ref, lamv_ref, sub_ref, coef_ref, q_ref, kt_ref, v_ref, o_ref,
                   kaug_ref, vaug_ref, qq_ref, mask_ref, s_ref, p_ref, c_ref, m_ref, acc_ref,
                   *, tq, n_pairs, unroll, lam_init):
    t = v_ref.shape[1]
    hd = 2 * DIFF_HEAD_DIM
    n_strips = 2 * tq // ATT_STRIP

    feat_row = lax.broadcasted_iota(jnp.int32, (LANES, tq), 0)
    in_block = lax.broadcasted_iota(jnp.int32, (LANES, tq), 1)
    for kb in range(t // tq):
        feat = jnp.where(feat_row < 3, kb, jnp.where(feat_row < 6, in_block, 0))
        kaug_ref[kb, 0:hd, :] = kt_ref[0, :, kb * tq:(kb + 1) * tq]
        kaug_ref[kb, hd:, :] = feat.astype(F32).astype(BF16)
    vaug_ref[:, 0:DIFF_V_DIM] = v_ref[0]
    vaug_ref[:, DIFF_V_DIM:] = jnp.ones((t, LANES), BF16)
    lane = lax.broadcasted_iota(jnp.int32, (tq, hd), 1)
    coef_rows = jnp.broadcast_to(coef_ref[0], (2 * tq, LANES))
    for qb in range(t // tq):
        q = q_ref[0, qb * tq:(qb + 1) * tq, :]
        zero = jnp.zeros_like(q)
        qq_ref[qb * 2 * tq:qb * 2 * tq + tq, 0:hd] = jnp.where(lane < DIFF_HEAD_DIM, q, zero)
        qq_ref[qb * 2 * tq + tq:(qb + 1) * 2 * tq, 0:hd] = jnp.where(lane < DIFF_HEAD_DIM, zero, q)
        qq_ref[qb * 2 * tq:(qb + 1) * 2 * tq, hd:] = coef_rows
    q_row = lax.broadcasted_iota(jnp.int32, (tq, tq), 0)
    k_col = lax.broadcasted_iota(jnp.int32, (tq, tq), 1)
    mask_ref[0] = jnp.zeros((tq, tq), F32)
    mask_ref[1] = jnp.where(k_col <= q_row, 0.0, NEG_BIG)
    m_ref[...] = jnp.full(m_ref.shape, NEG_BIG, F32)
    acc_ref[...] = jnp.zeros_like(acc_ref)

    def stage_qk(p, slot):
        q0 = pl.multiple_of(pi_ref[p] * (2 * tq), 2 * tq)
        s_ref[slot] = jnp.dot(qq_ref[pl.ds(q0, 2 * tq), :], kaug_ref[pj_ref[p]],
                              preferred_element_type=F32)

    def stage_sm(p, slot):
        i = pi_ref[p]
        diag = (i == pj_ref[p]).astype(jnp.int32)
        for r in range(n_strips):
            rows = slice(r * ATT_STRIP, (r + 1) * ATT_STRIP)
            mrows = slice((r * ATT_STRIP) % tq, (r * ATT_STRIP) % tq + ATT_STRIP)
            s = s_ref[slot, rows, :] + mask_ref[diag, mrows, :]
            m_prev = m_ref[i, rows, :]
            m_next = jnp.maximum(m_prev, jnp.max(s, axis=-1, keepdims=True))
            p_blk = jnp.exp2(s - jnp.tile(m_next, (1, tq // LANES)))
            p_ref[slot, rows, :] = p_blk.astype(BF16)
            c_ref[slot, rows, :] = jnp.exp2(m_prev - m_next)
            m_ref[i, rows, :] = m_next

    def stage_pv(p, slot):
        i = pi_ref[p]
        k0 = pl.multiple_of(pj_ref[p] * tq, tq)
        pv = jnp.dot(p_ref[slot], vaug_ref[pl.ds(k0, tq), :], preferred_element_type=F32)
        acc_ref[i] = acc_ref[i] * jnp.tile(c_ref[slot], (1, 2)) + pv

    stage_qk(0, 0)
    stage_qk(1, 1)
    stage_sm(0, 0)

    def pipeline_step(step, c):
        for u in range(unroll):
            p = unroll * step + u
            stage_qk(p + 2, u % 2)
            stage_sm(p + 1, (u + 1) % 2)
            stage_pv(p, u % 2)
        return c

    lax.fori_loop(0, n_pairs // unroll, pipeline_step, 0)

    lv = lamv_ref[...]
    lam = (jnp.exp(jnp.sum(lv[0:1] * lv[1:2], axis=-1, keepdims=True))
           - jnp.exp(jnp.sum(lv[2:3] * lv[3:4], axis=-1, keepdims=True)) + lam_init)

    def finalize(i, c):
        o = acc_ref[i, :, 0:DIFF_V_DIM] / acc_ref[i, :, DIFF_V_DIM:]
        o = o[:tq] - lam * o[tq:]
        o = o * lax.rsqrt(jnp.mean(o * o, axis=-1, keepdims=True) + LN_EPS) * sub_ref[...]
        r0 = pl.multiple_of(i * tq, tq)
        o_ref[0, pl.ds(r0, tq), :] = (o * (1.0 - lam_init)).astype(o_ref.dtype)
        return c

    lax.fori_loop(0, t // tq, finalize, 0, unroll=2)


def _diffattn(q, k_t, v, lam_vecs, subln_g, lam_init, tq):
    bsz, t, _ = q.shape
    slopes = jnp.exp2(-8.0 * (jnp.arange(DIFF_HEADS, dtype=F32) + 1.0) / DIFF_HEADS) * LOG2E
    def pieces(c):
        c1 = c.astype(BF16)
        c2 = (c - c1.astype(F32)).astype(BF16)
        c3 = (c - c1.astype(F32) - c2.astype(F32)).astype(BF16)
        return [c1, c2, c3]

    coef = jnp.zeros((DIFF_HEADS, 1, LANES), BF16)
    coef = coef.at[:, 0, 0:6].set(jnp.stack(pieces(slopes * tq) + pieces(slopes), axis=1))
    hd = 2 * DIFF_HEAD_DIM
    nq = t // tq
    assert nq <= 256, "key block indices must be exact in bf16"
    pairs = [(i, j) for i in range(nq) for j in range(i + 1)]
    n_pairs = len(pairs)
    unroll = next(u for u in (8, 4, 2) if n_pairs % u == 0)
    assert n_pairs % unroll == 0
    pairs = pairs + [pairs[-1]] * 2
    pair_i = jnp.asarray([p[0] for p in pairs], jnp.int32)
    pair_j = jnp.asarray([p[1] for p in pairs], jnp.int32)
    head = lambda shape: pl.BlockSpec(shape, lambda b, h, pi, pj: (b, 0, h))
    return pl.pallas_call(
        functools.partial(_diffattn_body, tq=tq, n_pairs=n_pairs, unroll=unroll, lam_init=lam_init),
        grid_spec=pltpu.PrefetchScalarGridSpec(
            num_scalar_prefetch=2,
            grid=(bsz, DIFF_HEADS),
            in_specs=[pl.BlockSpec(lam_vecs.shape, lambda b, h, pi, pj: (0, 0)),
                      pl.BlockSpec((1, DIFF_V_DIM), lambda b, h, pi, pj: (0, 0)),
                      pl.BlockSpec((1, 1, LANES), lambda b, h, pi, pj: (h, 0, 0)),
                      head((1, t, hd)),
                      pl.BlockSpec((1, hd, t), lambda b, h, pi, pj: (b, h, 0)),
                      head((1, t, DIFF_V_DIM))],
            out_specs=head((1, t, DIFF_V_DIM)),
            scratch_shapes=[pltpu.VMEM((nq, hd + LANES, tq), BF16),
                            pltpu.VMEM((t, DIFF_V_DIM + LANES), BF16),
                            pltpu.VMEM((2 * t, hd + LANES), BF16),
                            pltpu.VMEM((2, tq, tq), F32),
                            pltpu.VMEM((2, 2 * tq, tq), F32),
                            pltpu.VMEM((2, 2 * tq, tq), BF16),
                            pltpu.VMEM((2, 2 * tq, LANES), F32),
                            pltpu.VMEM((nq, 2 * tq, LANES), F32),
                            pltpu.VMEM((nq, 2 * tq, DIFF_V_DIM + LANES), F32)]),
        out_shape=jax.ShapeDtypeStruct((bsz, t, DIFF_V_W), BF16),
        compiler_params=_params(("parallel", "parallel")),
        name="diffattn",
    )(pair_i, pair_j, lam_vecs, subln_g.reshape(1, -1), coef, q, k_t, v)


def _router_body(h_ref, w_ref, o_ref, cnt_ref, carry_ref, *, tm):
    @pl.when(pl.program_id(0) == 0)
    def _():
        carry_ref[...] = jnp.zeros_like(carry_ref)

    h = h_ref[...]
    w = w_ref[...]
    h_hi = h.astype(BF16)
    h_lo = (h - h_hi.astype(F32)).astype(BF16)
    w_hi = w.astype(BF16)
    w_lo = (w - w_hi.astype(F32)).astype(BF16)
    logits = (jnp.dot(h_hi, w_hi, preferred_element_type=F32)
              + (jnp.dot(h_hi, w_lo, preferred_element_type=F32)
                 + jnp.dot(h_lo, w_hi, preferred_element_type=F32)))
    lane = lax.broadcasted_iota(jnp.int32, (tm, LANES), 1)
    lg = jnp.where(lane < N_EXPERTS, logits, -jnp.inf)
    v1 = jnp.max(lg, axis=-1, keepdims=True)
    i1 = jnp.min(jnp.where(lg == v1, lane, LANES), axis=-1, keepdims=True)
    oh1 = lane == i1
    lg2 = jnp.where(oh1, -jnp.inf, lg)
    v2 = jnp.max(lg2, axis=-1, keepdims=True)
    i2 = jnp.min(jnp.where(lg2 == v2, lane, LANES), axis=-1, keepdims=True)
    oh2 = lane == i2
    e2 = jnp.exp(v2 - v1)
    g1 = 1.0 / (1.0 + e2)
    g2 = e2 / (1.0 + e2)

    both = jnp.where(oh1, 1.0, jnp.where(oh2, 1.0, 0.0))
    tri = (lax.broadcasted_iota(jnp.int32, (tm, tm), 0)
           > lax.broadcasted_iota(jnp.int32, (tm, tm), 1))
    tri = jnp.where(tri, 1.0, 0.0).astype(BF16)
    before = jnp.dot(tri, both.astype(BF16), preferred_element_type=F32) + carry_ref[0:1, :]
    rank1 = jnp.sum(jnp.where(oh1, before, 0.0), axis=-1, keepdims=True)
    rank2 = jnp.sum(jnp.where(oh2, before, 0.0), axis=-1, keepdims=True)
    total = carry_ref[...] + jnp.sum(both, axis=0, keepdims=True)
    carry_ref[...] = total
    cnt_ref[...] = total

    out = jnp.where(lane == 0, i1.astype(F32), 0.0)
    out = jnp.where(lane == 1, i2.astype(F32), out)
    out = jnp.where(lane == 2, rank1, out)
    out = jnp.where(lane == 3, rank2, out)
    out = jnp.where(lane == 4, g1, out)
    out = jnp.where(lane == 5, g2, out)
    o_ref[...] = out


def _router(h, w_router, tm):
    m, d = h.shape
    w = jnp.zeros((d, LANES), F32).at[:, :N_EXPERTS].set(w_router)
    return pl.pallas_call(
        functools.partial(_router_body, tm=tm),
        grid=(m // tm,),
        in_specs=[pl.BlockSpec((tm, d), lambda i: (i, 0)),
                  pl.BlockSpec((d, LANES), lambda i: (0, 0))],
        out_specs=[pl.BlockSpec((tm, LANES), lambda i: (i, 0)),
                   pl.BlockSpec((SUBLANES, LANES), lambda i: (0, 0))],
        out_shape=[jax.ShapeDtypeStruct((m, LANES), F32),
                   jax.ShapeDtypeStruct((SUBLANES, LANES), F32)],
        scratch_shapes=[pltpu.VMEM((SUBLANES, LANES), F32)],
        compiler_params=_params(("arbitrary",)),
        name="router",
    )(h, w)


ROW_TILE = D_MODEL // LANES


def _row_tile(ref, r):
    start = r * ROW_TILE if isinstance(r, int) else pl.multiple_of(r * ROW_TILE, ROW_TILE)
    return ref.at[pl.ds(start, ROW_TILE), :]


def _store_row_tiles(tile_ref, x):
    for j in range(ROW_TILE):
        tile_ref[pl.ds(j, x.shape[0], stride=ROW_TILE), :] = x[:, j * LANES:(j + 1) * LANES]


def _load_row_tiles(tile_ref, n):
    return jnp.concatenate([tile_ref[pl.ds(j, n, stride=ROW_TILE), :] for j in range(ROW_TILE)], axis=1)


def _dispatch_body(dest_ref, meta_ref, h_ref, buf_ref, rows_ref, zero_ref, sem, *, tm, n_tok, n_rows):
    i = pl.program_id(0)
    base = i * tm
    _store_row_tiles(rows_ref, h_ref[...])

    def row_copy(r, d):
        return pltpu.make_async_copy(_row_tile(rows_ref, r), _row_tile(buf_ref, d), sem)

    def issue(r, c):
        row_copy(r, dest_ref[base + r]).start(priority=0)
        row_copy(r, dest_ref[n_tok + base + r]).start(priority=1)
        return c

    lax.fori_loop(0, tm, issue, 0, unroll=16)
    all_rows = buf_ref.at[pl.ds(0, TOP_K * tm * ROW_TILE), :]
    pltpu.make_async_copy(all_rows, all_rows, sem).wait()

    @pl.when(i == pl.num_programs(0) - 1)
    def _():
        zero_ref[...] = jnp.zeros_like(zero_ref)

        def zero_row(r):
            return pltpu.make_async_copy(_row_tile(zero_ref, 0), _row_tile(buf_ref, r), sem)

        def start_zero_row(r, c):
            zero_row(r).start()
            return c

        def wait_zero_row(r, c):
            zero_row(0).wait()
            return c

        for e in range(N_EXPERTS):
            lo = meta_ref[e] + meta_ref[N_EXPERTS + e]
            hi = meta_ref[e] + meta_ref[2 * N_EXPERTS + e]
            lax.fori_loop(lo, hi, start_zero_row, 0)
            lax.fori_loop(lo, hi, wait_zero_row, 0)

        def zero_block(b, c):
            r0 = pl.multiple_of(b * (MOE_ROWS * ROW_TILE), MOE_ROWS * ROW_TILE)
            cp = pltpu.make_async_copy(zero_ref, buf_ref.at[pl.ds(r0, MOE_ROWS * ROW_TILE), :], sem)
            cp.start()
            cp.wait()
            return c

        used = meta_ref[N_EXPERTS - 1] + meta_ref[3 * N_EXPERTS - 1]
        lax.fori_loop(used // MOE_ROWS, n_rows // MOE_ROWS, zero_block, 0)


def _dispatch(dest, meta, h, n_rows, tm):
    m, d = h.shape
    return pl.pallas_call(
        functools.partial(_dispatch_body, tm=tm, n_tok=m, n_rows=n_rows),
        grid_spec=pltpu.PrefetchScalarGridSpec(
            num_scalar_prefetch=2,
            grid=(m // tm,),
            in_specs=[pl.BlockSpec((tm, d), lambda i, dest, meta: (i, 0))],
            out_specs=pl.BlockSpec(memory_space=pl.ANY),
            scratch_shapes=[pltpu.VMEM((tm * ROW_TILE, LANES), F32),
                            pltpu.VMEM((MOE_ROWS * ROW_TILE, LANES), F32),
                            pltpu.SemaphoreType.DMA(())]),
        out_shape=jax.ShapeDtypeStruct((n_rows * ROW_TILE, LANES), F32),
        compiler_params=_params(("arbitrary",)),
        name="moe_dispatch",
    )(dest, meta, h)


def _moe_body(be_ref, nv_ref, x_ref, w1_ref, w3_ref, w2_ref, o_ref, acc_ref):
    b = pl.program_id(0)
    f = pl.program_id(1)

    @pl.when(jnp.logical_and(b == 0, f == 0))
    def _():
        acc_ref[...] = jnp.zeros_like(acc_ref)

    @pl.when(b < nv_ref[0])
    def _():
        xb = _load_row_tiles(x_ref, MOE_ROWS).astype(BF16)
        gate = jnp.dot(xb, w1_ref[0], preferred_element_type=F32)
        up = jnp.dot(xb, w3_ref[0], preferred_element_type=F32)
        act = (jax.nn.silu(gate) * up).astype(BF16)
        acc = jnp.where(f == 0, 0.0, acc_ref[...]) + jnp.dot(act, w2_ref[0], preferred_element_type=F32)
        acc_ref[...] = acc
        _store_row_tiles(o_ref, acc)

    @pl.when(jnp.logical_and(b >= nv_ref[0], f == pl.num_programs(1) - 1))
    def _():
        o_ref[...] = jnp.zeros_like(o_ref)


def _moe(block_e, n_valid, buf, w13, w2):
    d = D_MODEL
    n_rows = buf.shape[0] // ROW_TILE
    nb = n_rows // MOE_ROWS
    nf = FFN_DIM // FFN_TILE
    row_block = (MOE_ROWS * ROW_TILE, LANES)

    def blk(b, nv):
        return jnp.minimum(b, nv[0] - 1)

    def ftile(b, f, nv):
        return jnp.where(b < nv[0], f, nf - 1)

    return pl.pallas_call(
        _moe_body,
        grid_spec=pltpu.PrefetchScalarGridSpec(
            num_scalar_prefetch=2,
            grid=(nb, nf),
            in_specs=[pl.BlockSpec(row_block, lambda b, f, be, nv: (blk(b, nv), 0)),
                      pl.BlockSpec((1, d, FFN_TILE),
                                   lambda b, f, be, nv: (be[blk(b, nv)], 0, ftile(b, f, nv))),
                      pl.BlockSpec((1, d, FFN_TILE),
                                   lambda b, f, be, nv: (be[blk(b, nv)], 0, nf + ftile(b, f, nv))),
                      pl.BlockSpec((1, FFN_TILE, d),
                                   lambda b, f, be, nv: (be[blk(b, nv)], ftile(b, f, nv), 0))],
            out_specs=pl.BlockSpec(row_block, lambda b, f, be, nv: (b, 0)),
            scratch_shapes=[pltpu.VMEM((MOE_ROWS, d), F32)]),
        out_shape=jax.ShapeDtypeStruct((n_rows * ROW_TILE, LANES), F32),
        compiler_params=_params(("arbitrary", "arbitrary")),
        name="moe_experts",
    )(block_e, n_valid, buf, w13, w13, w2)


def _combine_ln_body(dest_ref, h_ref, route_ref, g_ref, b_ref, yb_ref, o_ref, rows_ref, sem,
                     *, tm, n_tok):
    i = pl.program_id(0)

    def gather_tile(tile, buf):
        base = tile * tm

        def issue(r, c):
            for k in range(TOP_K):
                pltpu.make_async_copy(_row_tile(yb_ref, dest_ref[k * n_tok + base + r]),
                                      _row_tile(rows_ref.at[buf, k], r), sem.at[buf]).start(priority=k)
            return c

        lax.fori_loop(0, tm, issue, 0, unroll=16)

    @pl.when(i == 0)
    def _():
        gather_tile(0, 0)

    @pl.when(i + 1 < pl.num_programs(0))
    def _():
        gather_tile(i + 1, (i + 1) % 2)

    buf = i % 2
    pltpu.make_async_copy(rows_ref.at[buf], rows_ref.at[buf], sem.at[buf]).wait()
    route = route_ref[...]
    y = (route[:, 4:5] * _load_row_tiles(rows_ref.at[buf, 0], tm)
         + route[:, 5:6] * _load_row_tiles(rows_ref.at[buf, 1], tm))
    z = DEEPNORM_ALPHA * h_ref[...] + y
    o_ref[...] = _layer_norm(z, g_ref[...], b_ref[...])


def _combine_ln(dest, h, route, yb, g, b, tm):
    m, d = h.shape
    return pl.pallas_call(
        functools.partial(_combine_ln_body, tm=tm, n_tok=m),
        grid_spec=pltpu.PrefetchScalarGridSpec(
            num_scalar_prefetch=1,
            grid=(m // tm,),
            in_specs=[pl.BlockSpec((tm, d), lambda i, dest: (i, 0)),
                      pl.BlockSpec((tm, LANES), lambda i, dest: (i, 0)),
                      pl.BlockSpec((1, d), lambda i, dest: (0, 0)),
                      pl.BlockSpec((1, d), lambda i, dest: (0, 0)),
                      pl.BlockSpec(memory_space=pl.ANY)],
            out_specs=pl.BlockSpec((tm, d), lambda i, dest: (i, 0)),
            scratch_shapes=[pltpu.VMEM((2, TOP_K, tm * ROW_TILE, LANES), F32),
                            pltpu.SemaphoreType.DMA((2,))]),
        out_shape=jax.ShapeDtypeStruct((m, d), F32),
        compiler_params=_params(("arbitrary",)),
        name="moe_combine_ln",
    )(dest, h, route, g.reshape(1, -1), b.reshape(1, -1), yb)


def _moe_layer(h, w_router, w13, w2, g, b):
    n_tok, _ = h.shape
    route, cnt = _router(h, w_router, tm=512)
    counts = cnt[0, :N_EXPERTS].astype(jnp.int32)
    padded = (counts + MOE_ROWS - 1) // MOE_ROWS * MOE_ROWS
    pstart = jnp.cumsum(padded) - padded
    experts = route[:, 0:2].astype(jnp.int32)
    ranks = route[:, 2:4].astype(jnp.int32)
    dest = (pstart[experts] + ranks).T.reshape(-1)
    n_rows = n_tok * TOP_K + N_EXPERTS * MOE_ROWS
    n_blocks = n_rows // MOE_ROWS
    block_end = jnp.cumsum(padded // MOE_ROWS)
    block_e = jnp.sum(jnp.arange(n_blocks)[:, None] >= block_end[None, :], axis=1)
    block_e = jnp.minimum(block_e, N_EXPERTS - 1).astype(jnp.int32)
    n_valid = block_end[-1:].astype(jnp.int32)
    meta = jnp.concatenate([pstart, counts, padded]).astype(jnp.int32)
    buf = _dispatch(dest, meta, h, n_rows, tm=512)
    yb = _moe(block_e, n_valid, buf, w13, w2)
    return _combine_ln(dest, h, route, yb, g, b, tm=512)


def kernel(x, mem, a_w_in, a_conv_w, a_conv_b, a_w_rgate, a_b_rgate, a_w_igate, a_b_igate, a_lambda,
           a_w_out, w_kv_shared, b_w_q, b_lambda, b_subln_g, b_w_out, mem_w_kv, ffn_w13, ffn_w2,
           moe_router, moe_w13, moe_w2, ln_g, ln_b):
    bsz, t, d = x.shape
    n_tok = bsz * t
    n_mem = mem.shape[1]
    bf = lambda a: a.astype(BF16)

    kv_mem = _proj(mem.reshape(bsz * n_mem, d), bf(jnp.concatenate([mem_w_kv[0], mem_w_kv[1]], axis=1)),
                   [(0, 2 * MEM_W), (2 * MEM_W, 4 * MEM_W)], [None, None], [BF16, BF16],
                   tm=min(1024, bsz * n_mem), name="mem_kv")
    kv_mem = [a.reshape(bsz, n_mem, 2 * MEM_W) for a in kv_mem]

    x2 = x.reshape(n_tok, d)
    gate, u_pre, q_mem = _proj(
        x2, bf(a_w_in[0]), [(0, D_RNN), (D_RNN, 2 * D_RNN), (2 * D_RNN, 2 * D_RNN + MEM_W)],
        [jax.nn.gelu, None, None], [BF16, BF16, BF16], tm=512, name="proj_in")
    w_ri = bf(jnp.concatenate([a_w_rgate[0], a_w_igate[0]], axis=-1))
    rnn = _rglru(u_pre.reshape(bsz, t, D_RNN), gate.reshape(bsz, t, D_RNN), a_conv_w[0], a_conv_b[0],
                 w_ri, a_b_rgate[0], a_b_igate[0], a_lambda[0], tt=512)
    h = _outproj_ln(rnn, q_mem.reshape(bsz, t, MEM_W), kv_mem[0], bf(a_w_out[0]), x,
                    ln_g[0, 0], ln_b[0, 0], tm=512, name="outproj_ln_a")
    h = _ffn_ln(h.reshape(n_tok, d), bf(ffn_w13[0]), bf(ffn_w2[0]), ln_g[0, 1], ln_b[0, 1], tm=512)

    layer = 1
    lam_init = 0.8 - 0.6 * math.exp(-0.3 * layer)
    w_cat = bf(jnp.concatenate([w_kv_shared[:, DIFF_QK_W:], b_w_q[0]], axis=1))
    w_k_t = bf(w_kv_shared[:, :DIFF_QK_W].T)
    v_sh, k_t, q_diff, q_mem = _proj_kvq(h, w_cat, w_k_t, bsz, t, tm=512)
    attn = _diffattn(q_diff.reshape(bsz, t, DIFF_QK_W), k_t, v_sh.reshape(bsz, t, DIFF_V_W),
                     b_lambda[0], b_subln_g[0], lam_init, tq=256)
    h = _outproj_ln(attn, q_mem.reshape(bsz, t, MEM_W), kv_mem[1], bf(b_w_out[0]),
                    h.reshape(bsz, t, d), ln_g[1, 0], ln_b[1, 0], tm=512, name="outproj_ln_b")
    out = _moe_layer(h.reshape(n_tok, d), moe_router[0], bf(moe_w13[0]), bf(moe_w2[0]),
                     ln_g[1, 1], ln_b[1, 1])
    return out.reshape(bsz, t, d)
```

```python
import functools
import math

import jax
import jax.numpy as jnp
from jax import lax
from jax.experimental import pallas as pl
from jax.experimental.pallas import tpu as pltpu

F32 = jnp.float32
BF16 = jnp.bfloat16

D_MODEL = 1024
DEPTH = 2
D_RNN = D_MODEL
RNN_BLOCKS = 8
RNN_BLOCK_W = D_RNN // RNN_BLOCKS
CONV_W = 4
LRU_C = 8.0
MEM_HEADS = 4
MEM_HEAD_DIM = D_MODEL // 8
MEM_W = MEM_HEADS * MEM_HEAD_DIM
DIFF_HEADS = 8
DIFF_HEAD_DIM = D_MODEL // 16
DIFF_V_DIM = 2 * DIFF_HEAD_DIM
DIFF_QK_W = DIFF_HEADS * 2 * DIFF_HEAD_DIM
DIFF_V_W = DIFF_HEADS * DIFF_V_DIM
FFN_DIM = (7 * D_MODEL) // 2
N_EXPERTS = 8
TOP_K = 2
LN_EPS = 1e-5
DEEPNORM_ALPHA = (2.0 * DEPTH) ** 0.25

LANES = 128
SUBLANES = 8
VMEM_LIMIT = 52 * 1024 * 1024

FFN_TILE = 1792
MOE_ROWS = 512
NEG_BIG = -1e30
LOG2E = math.log2(math.e)


def _params(semantics):
    return pltpu.CompilerParams(dimension_semantics=semantics, vmem_limit_bytes=VMEM_LIMIT)


def _layer_norm(z, g, b):
    mu = jnp.mean(z, axis=-1, keepdims=True)
    zc = z - mu
    var = jnp.mean(zc * zc, axis=-1, keepdims=True)
    return zc * lax.rsqrt(var + LN_EPS) * g + b


def _proj_body(x_ref, w_ref, *o_refs, splits, post):
    xb = x_ref[...].astype(BF16)
    for o_ref, (c0, c1), fn in zip(o_refs, splits, post):
        z = jnp.dot(xb, w_ref[:, c0:c1], preferred_element_type=F32)
        if fn is not None:
            z = fn(z)
        o_ref[...] = z.astype(o_ref.dtype)


def _proj(x, w, splits, post, out_dtypes, tm, name):
    m, k = x.shape
    return pl.pallas_call(
        functools.partial(_proj_body, splits=tuple(splits), post=tuple(post)),
        grid=(m // tm,),
        in_specs=[pl.BlockSpec((tm, k), lambda i: (i, 0)),
                  pl.BlockSpec(w.shape, lambda i: (0, 0))],
        out_specs=[pl.BlockSpec((tm, c1 - c0), lambda i: (i, 0)) for c0, c1 in splits],
        out_shape=[jax.ShapeDtypeStruct((m, c1 - c0), dt) for (c0, c1), dt in zip(splits, out_dtypes)],
        compiler_params=_params(("parallel",)),
        name=name,
    )(x, w)


def _proj_kvq_body(x_ref, w_ref, wkt_ref, v_ref, kt_ref, qd_ref, qm_ref):
    xb = x_ref[...].astype(BF16)
    q_scale = DIFF_HEAD_DIM ** -0.5 * LOG2E
    v_ref[...] = jnp.dot(xb, w_ref[:, 0:DIFF_V_W], preferred_element_type=F32).astype(BF16)
    kt_ref[0] = lax.dot_general(wkt_ref[...], xb, (((1,), (1,)), ((), ())),
                                preferred_element_type=F32).astype(BF16)
    qd = jnp.dot(xb, w_ref[:, DIFF_V_W:DIFF_V_W + DIFF_QK_W], preferred_element_type=F32)
    qd_ref[...] = (qd * q_scale).astype(BF16)
    qm_ref[...] = jnp.dot(xb, w_ref[:, DIFF_V_W + DIFF_QK_W:], preferred_element_type=F32).astype(BF16)


def _proj_kvq(h, w_cat, w_k_t, bsz, t, tm):
    m, d = h.shape
    nt = t // tm
    rows = lambda width: pl.BlockSpec((tm, width), lambda i: (i, 0))
    return pl.pallas_call(
        _proj_kvq_body,
        grid=(m // tm,),
        in_specs=[rows(d), pl.BlockSpec(w_cat.shape, lambda i: (0, 0)),
                  pl.BlockSpec(w_k_t.shape, lambda i: (0, 0))],
        out_specs=[rows(DIFF_V_W), pl.BlockSpec((1, DIFF_QK_W, tm), lambda i: (i // nt, 0, i % nt)),
                   rows(DIFF_QK_W), rows(MEM_W)],
        out_shape=[jax.ShapeDtypeStruct((m, DIFF_V_W), BF16),
                   jax.ShapeDtypeStruct((bsz, DIFF_QK_W, t), BF16),
                   jax.ShapeDtypeStruct((m, DIFF_QK_W), BF16),
                   jax.ShapeDtypeStruct((m, MEM_W), BF16)],
        compiler_params=_params(("parallel",)),
        name="proj_kvq",
    )(h, w_cat, w_k_t)


def _rglru_body(u_ref, g_ref, cw_ref, cb_ref, wri_ref, br_ref, bi_ref, lam_ref, o_ref,
                ext_ref, rec_ref, a_ref, b_ref, h_ref, *, tt):
    t = pl.program_id(1)

    @pl.when(t == 0)
    def _():
        ext_ref[0:SUBLANES, :] = jnp.zeros((SUBLANES, D_RNN), F32)
        h_ref[...] = jnp.zeros((SUBLANES, D_RNN), F32)

    ext_ref[SUBLANES:, :] = u_ref[0].astype(F32)
    rec = cb_ref[...] + cw_ref[CONV_W - 1:CONV_W, :] * ext_ref[SUBLANES:SUBLANES + tt, :]
    for j in range(CONV_W - 1):
        off = SUBLANES - (CONV_W - 1) + j
        rec = rec + cw_ref[j:j + 1, :] * ext_ref[off:off + tt, :]
    rec_ref[...] = rec
    ext_ref[0:SUBLANES, :] = ext_ref[tt:tt + SUBLANES, :]

    sp = jax.nn.softplus(-lam_ref[...])
    first_row = (lax.broadcasted_iota(jnp.int32, (SUBLANES, RNN_BLOCK_W), 0) + t * tt) == 0
    for n in range(RNN_BLOCKS):
        blk = slice(n * RNN_BLOCK_W, (n + 1) * RNN_BLOCK_W)
        u_n = rec_ref[:, blk]
        ri = jnp.dot(u_n.astype(BF16), wri_ref[n], preferred_element_type=F32)
        r_t = jnp.tanh(0.5 * (ri[:, :RNN_BLOCK_W] + br_ref[:, blk]))
        i = 0.5 * jnp.tanh(0.5 * (ri[:, RNN_BLOCK_W:] + bi_ref[:, blk])) + 0.5
        half = (-0.5 * LRU_C) * sp[:, blk]
        log_a = half * r_t + half
        a = jnp.exp(log_a)
        th = jnp.tanh(-log_a)
        m2 = 2.0 * th / (1.0 + th)
        mult = jnp.where(m2 > 0.0, m2 * lax.rsqrt(m2), 0.0)
        mult = jnp.concatenate([jnp.where(first_row, 1.0, mult[:SUBLANES]), mult[SUBLANES:]], axis=0)
        a_ref[:, blk] = a
        b_ref[:, blk] = mult * (i * u_n)

    row = lax.broadcasted_iota(jnp.int32, (SUBLANES, D_RNN), 0)

    def tile(i, h_prev):
        r0 = pl.multiple_of(i * SUBLANES, SUBLANES)
        a = a_ref[pl.ds(r0, SUBLANES), :]
        b = b_ref[pl.ds(r0, SUBLANES), :]
        for s in (1, 2, 4):
            a_s = pltpu.roll(a, s, 0)
            b_s = pltpu.roll(b, s, 0)
            keep = row >= s
            b = jnp.where(keep, a * b_s + b, b)
            a = jnp.where(keep, a * a_s, a)
        h = b + a * h_prev
        b_ref[pl.ds(r0, SUBLANES), :] = h
        return jnp.broadcast_to(h[SUBLANES - 1:SUBLANES, :], (SUBLANES, D_RNN))

    h_ref[...] = lax.fori_loop(0, tt // SUBLANES, tile, h_ref[...])
    o_ref[0] = (b_ref[...] * g_ref[0].astype(F32)).astype(o_ref.dtype)


def _rglru(u_pre, gate, conv_w, conv_b, w_ri, b_r, b_i, lam, tt):
    bsz, t, c = u_pre.shape
    row = lambda a: a.reshape(1, c)
    full = lambda shape: pl.BlockSpec(shape, lambda b, i: (0,) * len(shape))
    return pl.pallas_call(
        functools.partial(_rglru_body, tt=tt),
        grid=(bsz, t // tt),
        in_specs=[pl.BlockSpec((1, tt, c), lambda b, i: (b, i, 0)),
                  pl.BlockSpec((1, tt, c), lambda b, i: (b, i, 0)),
                  full((CONV_W, c)), full((1, c)), full(w_ri.shape),
                  full((1, c)), full((1, c)), full((1, c))],
        out_specs=pl.BlockSpec((1, tt, c), lambda b, i: (b, i, 0)),
        out_shape=jax.ShapeDtypeStruct((bsz, t, c), BF16),
        scratch_shapes=[pltpu.VMEM((tt + SUBLANES, c), F32), pltpu.VMEM((tt, c), F32),
                        pltpu.VMEM((tt, c), F32), pltpu.VMEM((tt, c), F32),
                        pltpu.VMEM((SUBLANES, c), F32)],
        compiler_params=_params(("parallel", "arbitrary")),
        name="rglru",
    )(u_pre, gate, conv_w, row(conv_b), w_ri, row(b_r), row(b_i), row(lam))


def _outproj_ln_body(a_ref, qm_ref, kv_ref, w_ref, res_ref, g_ref, b_ref, o_ref):
    mix_w = a_ref.shape[-1]
    y = jnp.dot(a_ref[0], w_ref[0:mix_w, :], preferred_element_type=F32)
    scale = MEM_HEAD_DIM ** -0.5 * LOG2E
    heads = []
    for h in range(MEM_HEADS):
        hs = slice(h * MEM_HEAD_DIM, (h + 1) * MEM_HEAD_DIM)
        vs = slice(MEM_W + h * MEM_HEAD_DIM, MEM_W + (h + 1) * MEM_HEAD_DIM)
        s = lax.dot_general(qm_ref[0, :, hs], kv_ref[0, :, hs], (((1,), (1,)), ((), ())),
                            preferred_element_type=F32)
        e = jnp.exp2((s - jnp.max(s, axis=-1, keepdims=True)) * scale)
        pv = jnp.dot(e.astype(BF16), kv_ref[0, :, vs], preferred_element_type=F32)
        heads.append(pv / jnp.sum(e, axis=-1, keepdims=True))
    mem_out = jnp.concatenate(heads, axis=1).astype(BF16)
    y = y + jnp.dot(mem_out, w_ref[mix_w:, :], preferred_element_type=F32)
    z = DEEPNORM_ALPHA * res_ref[0] + y
    o_ref[0] = _layer_norm(z, g_ref[...], b_ref[...])


def _outproj_ln(mix, q_mem, kv_mem, w_out, resid, g, b, tm, name):
    bsz, t, mix_w = mix.shape
    n_mem = kv_mem.shape[1]
    return pl.pallas_call(
        _outproj_ln_body,
        grid=(bsz, t // tm),
        in_specs=[pl.BlockSpec((1, tm, mix_w), lambda b, i: (b, i, 0)),
                  pl.BlockSpec((1, tm, MEM_W), lambda b, i: (b, i, 0)),
                  pl.BlockSpec((1, n_mem, 2 * MEM_W), lambda b, i: (b, 0, 0)),
                  pl.BlockSpec(w_out.shape, lambda b, i: (0, 0)),
                  pl.BlockSpec((1, tm, D_MODEL), lambda b, i: (b, i, 0)),
                  pl.BlockSpec((1, D_MODEL), lambda b, i: (0, 0)),
                  pl.BlockSpec((1, D_MODEL), lambda b, i: (0, 0))],
        out_specs=pl.BlockSpec((1, tm, D_MODEL), lambda b, i: (b, i, 0)),
        out_shape=jax.ShapeDtypeStruct((bsz, t, D_MODEL), F32),
        compiler_params=_params(("parallel", "parallel")),
        name=name,
    )(mix, q_mem, kv_mem, w_out, resid, g.reshape(1, -1), b.reshape(1, -1))


def _ffn_ln_body(h_ref, w13_ref, w2_ref, g_ref, b_ref, o_ref):
    h = h_ref[...]
    hb = h.astype(BF16)
    acc = None
    for c in range(FFN_DIM // FFN_TILE):
        cols = slice(c * FFN_TILE, (c + 1) * FFN_TILE)
        up_cols = slice(FFN_DIM + c * FFN_TILE, FFN_DIM + (c + 1) * FFN_TILE)
        gate = jnp.dot(hb, w13_ref[:, cols], preferred_element_type=F32)
        up = jnp.dot(hb, w13_ref[:, up_cols], preferred_element_type=F32)
        act = (jax.nn.silu(gate) * up).astype(BF16)
        part = jnp.dot(act, w2_ref[cols, :], preferred_element_type=F32)
        acc = part if acc is None else acc + part
    o_ref[...] = _layer_norm(DEEPNORM_ALPHA * h + acc, g_ref[...], b_ref[...])


def _ffn_ln(h, w13, w2, g, b, tm):
    m, d = h.shape
    resident = lambda shape: pl.BlockSpec(shape, lambda i: (0, 0), pipeline_mode=pl.Buffered(1))
    return pl.pallas_call(
        _ffn_ln_body,
        grid=(m // tm,),
        in_specs=[pl.BlockSpec((tm, d), lambda i: (i, 0)),
                  resident(w13.shape), resident(w2.shape),
                  pl.BlockSpec((1, d), lambda i: (0, 0)),
                  pl.BlockSpec((1, d), lambda i: (0, 0))],
        out_specs=pl.BlockSpec((tm, d), lambda i: (i, 0)),
        out_shape=jax.ShapeDtypeStruct((m, d), F32),
        compiler_params=_params(("parallel",)),
        name="ffn_ln",
    )(h, w13, w2, g.reshape(1, -1), b.reshape(1, -1))


ATT_STRIP = 128


def _diffattn_body(pi_ref, pj_ref, lamv_ref, sub_ref, coef_ref, q_ref, kt_ref, v_ref, o_ref,
                   kaug_ref, vaug_ref, qq_ref, mask_ref, s_ref, p_ref, c_ref, m_ref, acc_ref,
                   *, tq, tk, n_pairs, unroll, lam_init):
    t = v_ref.shape[1]
    hd = 2 * DIFF_HEAD_DIM
    n_strips = 2 * tq // ATT_STRIP
    ratio = tq // tk

    feat_row = lax.broadcasted_iota(jnp.int32, (LANES, tk), 0)
    in_block = lax.broadcasted_iota(jnp.int32, (LANES, tk), 1)
    for kb in range(t // tk):
        feat = jnp.where(feat_row < 3, kb, jnp.where(feat_row < 6, in_block, 0))
        kaug_ref[kb, 0:hd, :] = kt_ref[0, :, kb * tk:(kb + 1) * tk]
        kaug_ref[kb, hd:, :] = feat.astype(F32).astype(BF16)
    vaug_ref[:, 0:DIFF_V_DIM] = v_ref[0]
    vaug_ref[:, DIFF_V_DIM:] = jnp.ones((t, LANES), BF16)
    lane = lax.broadcasted_iota(jnp.int32, (tq, hd), 1)
    coef_rows = jnp.broadcast_to(coef_ref[0], (2 * tq, LANES))
    for qb in range(t // tq):
        q = q_ref[0, qb * tq:(qb + 1) * tq, :]
        zero = jnp.zeros_like(q)
        qq_ref[qb * 2 * tq:qb * 2 * tq + tq, 0:hd] = jnp.where(lane < DIFF_HEAD_DIM, q, zero)
        qq_ref[qb * 2 * tq + tq:(qb + 1) * 2 * tq, 0:hd] = jnp.where(lane < DIFF_HEAD_DIM, zero, q)
        qq_ref[qb * 2 * tq:(qb + 1) * 2 * tq, hd:] = coef_rows
    q_row = lax.broadcasted_iota(jnp.int32, (tq, tk), 0)
    k_col = lax.broadcasted_iota(jnp.int32, (tq, tk), 1)
    mask_ref[0] = jnp.zeros((tq, tk), F32)
    for v in range(1, ratio + 1):
        mask_ref[v] = jnp.where(k_col + (v - 1) * tk <= q_row, 0.0, NEG_BIG)
    m_ref[...] = jnp.full(m_ref.shape, NEG_BIG, F32)
    acc_ref[...] = jnp.zeros_like(acc_ref)

    def stage_qk(p, slot):
        q0 = pl.multiple_of(pi_ref[p] * (2 * tq), 2 * tq)
        s_ref[slot] = jnp.dot(qq_ref[pl.ds(q0, 2 * tq), :], kaug_ref[pj_ref[p]],
                              preferred_element_type=F32)

    def stage_sm(p, slot):
        i = pi_ref[p]
        diag = jnp.maximum(pj_ref[p] - ratio * i + 1, 0)
        for r in range(n_strips):
            rows = slice(r * ATT_STRIP, (r + 1) * ATT_STRIP)
            mrows = slice((r * ATT_STRIP) % tq, (r * ATT_STRIP) % tq + ATT_STRIP)
            s = s_ref[slot, rows, :] + mask_ref[diag, mrows, :]
            m_prev = m_ref[i, rows, :]
            m_next = jnp.maximum(m_prev, jnp.max(s, axis=-1, keepdims=True))
            p_blk = jnp.exp2(s - jnp.tile(m_next, (1, tk // LANES)))
            p_ref[slot, rows, :] = p_blk.astype(BF16)
            c_ref[slot, rows, :] = jnp.exp2(m_prev - m_next)
            m_ref[i, rows, :] = m_next

    def stage_pv(p, slot):
        i = pi_ref[p]
        k0 = pl.multiple_of(pj_ref[p] * tk, tk)
        pv = jnp.dot(p_ref[slot], vaug_ref[pl.ds(k0, tk), :], preferred_element_type=F32)
        acc_ref[i] = acc_ref[i] * jnp.tile(c_ref[slot], (1, 2)) + pv

    stage_qk(0, 0)
    stage_qk(1, 1)
    stage_sm(0, 0)

    def pipeline_step(step, c):
        for u in range(unroll):
            p = unroll * step + u
            stage_qk(p + 2, u % 2)
            stage_sm(p + 1, (u + 1) % 2)
            stage_pv(p, u % 2)
        return c

    lax.fori_loop(0, n_pairs // unroll, pipeline_step, 0)

    lv = lamv_ref[...]
    lam = (jnp.exp(jnp.sum(lv[0:1] * lv[1:2], axis=-1, keepdims=True))
           - jnp.exp(jnp.sum(lv[2:3] * lv[3:4], axis=-1, keepdims=True)) + lam_init)

    def finalize(i, c):
        o = acc_ref[i, :, 0:DIFF_V_DIM] / acc_ref[i, :, DIFF_V_DIM:]
        o = o[:tq] - lam * o[tq:]
        o = o * lax.rsqrt(jnp.mean(o * o, axis=-1, keepdims=True) + LN_EPS) * sub_ref[...]
        r0 = pl.multiple_of(i * tq, tq)
        o_ref[0, pl.ds(r0, tq), :] = (o * (1.0 - lam_init)).astype(o_ref.dtype)
        return c

    lax.fori_loop(0, t // tq, finalize, 0, unroll=2)


def _diffattn(q, k_t, v, lam_vecs, subln_g, lam_init, tq, tk):
    bsz, t, _ = q.shape
    slopes = jnp.exp2(-8.0 * (jnp.arange(DIFF_HEADS, dtype=F32) + 1.0) / DIFF_HEADS) * LOG2E
    def pieces(c):
        c1 = c.astype(BF16)
        c2 = (c - c1.astype(F32)).astype(BF16)
        c3 = (c - c1.astype(F32) - c2.astype(F32)).astype(BF16)
        return [c1, c2, c3]

    coef = jnp.zeros((DIFF_HEADS, 1, LANES), BF16)
    coef = coef.at[:, 0, 0:6].set(jnp.stack(pieces(slopes * tk) + pieces(slopes), axis=1))
    hd = 2 * DIFF_HEAD_DIM
    nq = t // tq
    nk = t // tk
    ratio = tq // tk
    assert nk <= 256 and tk <= 256, "key block index and in-block position must be exact in bf16"
    pairs = [(i, j) for i in range(nq) for j in range(ratio * (i + 1))]
    n_pairs = len(pairs)
    unroll = next(u for u in (34, 8, 4, 2) if n_pairs % u == 0)
    assert n_pairs % unroll == 0
    pairs = pairs + [pairs[-1]] * 2
    pair_i = jnp.asarray([p[0] for p in pairs], jnp.int32)
    pair_j = jnp.asarray([p[1] for p in pairs], jnp.int32)
    head = lambda shape: pl.BlockSpec(shape, lambda b, h, pi, pj: (b, 0, h))
    return pl.pallas_call(
        functools.partial(_diffattn_body, tq=tq, tk=tk, n_pairs=n_pairs, unroll=unroll, lam_init=lam_init),
        grid_spec=pltpu.PrefetchScalarGridSpec(
            num_scalar_prefetch=2,
            grid=(bsz, DIFF_HEADS),
            in_specs=[pl.BlockSpec(lam_vecs.shape, lambda b, h, pi, pj: (0, 0)),
                      pl.BlockSpec((1, DIFF_V_DIM), lambda b, h, pi, pj: (0, 0)),
                      pl.BlockSpec((1, 1, LANES), lambda b, h, pi, pj: (h, 0, 0)),
                      head((1, t, hd)),
                      pl.BlockSpec((1, hd, t), lambda b, h, pi, pj: (b, h, 0)),
                      head((1, t, DIFF_V_DIM))],
            out_specs=head((1, t, DIFF_V_DIM)),
            scratch_shapes=[pltpu.VMEM((nk, hd + LANES, tk), BF16),
                            pltpu.VMEM((t, DIFF_V_DIM + LANES), BF16),
                            pltpu.VMEM((2 * t, hd + LANES), BF16),
                            pltpu.VMEM((ratio + 1, tq, tk), F32),
                            pltpu.VMEM((2, 2 * tq, tk), F32),
                            pltpu.VMEM((2, 2 * tq, tk), BF16),
                            pltpu.VMEM((2, 2 * tq, LANES), F32),
                            pltpu.VMEM((nq, 2 * tq, LANES), F32),
                            pltpu.VMEM((nq, 2 * tq, DIFF_V_DIM + LANES), F32)]),
        out_shape=jax.ShapeDtypeStruct((bsz, t, DIFF_V_W), BF16),
        compiler_params=_params(("parallel", "parallel")),
        name="diffattn",
    )(pair_i, pair_j, lam_vecs, subln_g.reshape(1, -1), coef, q, k_t, v)


def _router_body(h_ref, w_ref, o_ref, cnt_ref, carry_ref, *, tm):
    @pl.when(pl.program_id(0) == 0)
    def _():
        carry_ref[...] = jnp.zeros_like(carry_ref)

    h = h_ref[...]
    w = w_ref[...]
    h_hi = h.astype(BF16)
    h_lo = (h - h_hi.astype(F32)).astype(BF16)
    w_hi = w.astype(BF16)
    w_lo = (w - w_hi.astype(F32)).astype(BF16)
    logits = (jnp.dot(h_hi, w_hi, preferred_element_type=F32)
              + (jnp.dot(h_hi, w_lo, preferred_element_type=F32)
                 + jnp.dot(h_lo, w_hi, preferred_element_type=F32)))
    lane = lax.broadcasted_iota(jnp.int32, (tm, LANES), 1)
    lg = jnp.where(lane < N_EXPERTS, logits, -jnp.inf)
    v1 = jnp.max(lg, axis=-1, keepdims=True)
    i1 = jnp.min(jnp.where(lg == v1, lane, LANES), axis=-1, keepdims=True)
    oh1 = lane == i1
    lg2 = jnp.where(oh1, -jnp.inf, lg)
    v2 = jnp.max(lg2, axis=-1, keepdims=True)
    i2 = jnp.min(jnp.where(lg2 == v2, lane, LANES), axis=-1, keepdims=True)
    oh2 = lane == i2
    e2 = jnp.exp(v2 - v1)
    g1 = 1.0 / (1.0 + e2)
    g2 = e2 / (1.0 + e2)

    both = jnp.where(oh1, 1.0, jnp.where(oh2, 1.0, 0.0))
    tri = (lax.broadcasted_iota(jnp.int32, (tm, tm), 0)
           > lax.broadcasted_iota(jnp.int32, (tm, tm), 1))
    tri = jnp.where(tri, 1.0, 0.0).astype(BF16)
    before = jnp.dot(tri, both.astype(BF16), preferred_element_type=F32) + carry_ref[0:1, :]
    rank1 = jnp.sum(jnp.where(oh1, before, 0.0), axis=-1, keepdims=True)
    rank2 = jnp.sum(jnp.where(oh2, before, 0.0), axis=-1, keepdims=True)
    total = carry_ref[...] + jnp.sum(both, axis=0, keepdims=True)
    carry_ref[...] = total
    cnt_ref[...] = total

    out = jnp.where(lane == 0, i1.astype(F32), 0.0)
    out = jnp.where(lane == 1, i2.astype(F32), out)
    out = jnp.where(lane == 2, rank1, out)
    out = jnp.where(lane == 3, rank2, out)
    out = jnp.where(lane == 4, g1, out)
    out = jnp.where(lane == 5, g2, out)
    o_ref[...] = out


def _router(h, w_router, tm):
    m, d = h.shape
    w = jnp.zeros((d, LANES), F32).at[:, :N_EXPERTS].set(w_router)
    return pl.pallas_call(
        functools.partial(_router_body, tm=tm),
        grid=(m // tm,),
        in_specs=[pl.BlockSpec((tm, d), lambda i: (i, 0)),
                  pl.BlockSpec((d, LANES), lambda i: (0, 0))],
        out_specs=[pl.BlockSpec((tm, LANES), lambda i: (i, 0)),
                   pl.BlockSpec((SUBLANES, LANES), lambda i: (0, 0))],
        out_shape=[jax.ShapeDtypeStruct((m, LANES), F32),
                   jax.ShapeDtypeStruct((SUBLANES, LANES), F32)],
        scratch_shapes=[pltpu.VMEM((SUBLANES, LANES), F32)],
        compiler_params=_params(("arbitrary",)),
        name="router",
    )(h, w)


ROW_TILE = D_MODEL // LANES


def _row_tile(ref, r):
    start = r * ROW_TILE if isinstance(r, int) else pl.multiple_of(r * ROW_TILE, ROW_TILE)
    return ref.at[pl.ds(start, ROW_TILE), :]


def _store_row_tiles(tile_ref, x):
    for j in range(ROW_TILE):
        tile_ref[pl.ds(j, x.shape[0], stride=ROW_TILE), :] = x[:, j * LANES:(j + 1) * LANES]


def _load_row_tiles(tile_ref, n):
    return jnp.concatenate([tile_ref[pl.ds(j, n, stride=ROW_TILE), :] for j in range(ROW_TILE)], axis=1)


def _dispatch_body(dest_ref, meta_ref, h_ref, buf_ref, rows_ref, zero_ref, sem, *, tm, n_tok, n_rows):
    i = pl.program_id(0)
    base = i * tm
    _store_row_tiles(rows_ref, h_ref[...])

    def row_copy(r, d):
        return pltpu.make_async_copy(_row_tile(rows_ref, r), _row_tile(buf_ref, d), sem)

    def issue(r, c):
        row_copy(r, dest_ref[base + r]).start(priority=0)
        row_copy(r, dest_ref[n_tok + base + r]).start(priority=1)
        return c

    lax.fori_loop(0, tm, issue, 0, unroll=16)
    all_rows = buf_ref.at[pl.ds(0, TOP_K * tm * ROW_TILE), :]
    pltpu.make_async_copy(all_rows, all_rows, sem).wait()

    @pl.when(i == pl.num_programs(0) - 1)
    def _():
        zero_ref[...] = jnp.zeros_like(zero_ref)

        def zero_row(r):
            return pltpu.make_async_copy(_row_tile(zero_ref, 0), _row_tile(buf_ref, r), sem)

        def start_zero_row(r, c):
            zero_row(r).start()
            return c

        def wait_zero_row(r, c):
            zero_row(0).wait()
            return c

        for e in range(N_EXPERTS):
            lo = meta_ref[e] + meta_ref[N_EXPERTS + e]
            hi = meta_ref[e] + meta_ref[2 * N_EXPERTS + e]
            lax.fori_loop(lo, hi, start_zero_row, 0)
            lax.fori_loop(lo, hi, wait_zero_row, 0)

        def zero_block(b, c):
            r0 = pl.multiple_of(b * (MOE_ROWS * ROW_TILE), MOE_ROWS * ROW_TILE)
            cp = pltpu.make_async_copy(zero_ref, buf_ref.at[pl.ds(r0, MOE_ROWS * ROW_TILE), :], sem)
            cp.start()
            cp.wait()
            return c

        used = meta_ref[N_EXPERTS - 1] + meta_ref[3 * N_EXPERTS - 1]
        lax.fori_loop(used // MOE_ROWS, n_rows // MOE_ROWS, zero_block, 0)


def _dispatch(dest, meta, h, n_rows, tm):
    m, d = h.shape
    return pl.pallas_call(
        functools.partial(_dispatch_body, tm=tm, n_tok=m, n_rows=n_rows),
        grid_spec=pltpu.PrefetchScalarGridSpec(
            num_scalar_prefetch=2,
            grid=(m // tm,),
            in_specs=[pl.BlockSpec((tm, d), lambda i, dest, meta: (i, 0))],
            out_specs=pl.BlockSpec(memory_space=pl.ANY),
            scratch_shapes=[pltpu.VMEM((tm * ROW_TILE, LANES), F32),
                            pltpu.VMEM((MOE_ROWS * ROW_TILE, LANES), F32),
                            pltpu.SemaphoreType.DMA(())]),
        out_shape=jax.ShapeDtypeStruct((n_rows * ROW_TILE, LANES), F32),
        compiler_params=_params(("arbitrary",)),
        name="moe_dispatch",
    )(dest, meta, h)


def _moe_body(be_ref, nv_ref, x_ref, w1_ref, w3_ref, w2_ref, o_ref, acc_ref):
    b = pl.program_id(0)
    f = pl.program_id(1)

    @pl.when(jnp.logical_and(b == 0, f == 0))
    def _():
        acc_ref[...] = jnp.zeros_like(acc_ref)

    @pl.when(b < nv_ref[0])
    def _():
        xb = _load_row_tiles(x_ref, MOE_ROWS).astype(BF16)
        gate = jnp.dot(xb, w1_ref[0], preferred_element_type=F32)
        up = jnp.dot(xb, w3_ref[0], preferred_element_type=F32)
        act = (jax.nn.silu(gate) * up).astype(BF16)
        acc = jnp.where(f == 0, 0.0, acc_ref[...]) + jnp.dot(act, w2_ref[0], preferred_element_type=F32)
        acc_ref[...] = acc
        _store_row_tiles(o_ref, acc)

    @pl.when(jnp.logical_and(b >= nv_ref[0], f == pl.num_programs(1) - 1))
    def _():
        o_ref[...] = jnp.zeros_like(o_ref)


def _moe(block_e, n_valid, buf, w13, w2):
    d = D_MODEL
    n_rows = buf.shape[0] // ROW_TILE
    nb = n_rows // MOE_ROWS
    nf = FFN_DIM // FFN_TILE
    row_block = (MOE_ROWS * ROW_TILE, LANES)

    def blk(b, nv):
        return jnp.minimum(b, nv[0] - 1)

    def ftile(b, f, nv):
        return jnp.where(b < nv[0], f, nf - 1)

    return pl.pallas_call(
        _moe_body,
        grid_spec=pltpu.PrefetchScalarGridSpec(
            num_scalar_prefetch=2,
            grid=(nb, nf),
            in_specs=[pl.BlockSpec(row_block, lambda b, f, be, nv: (blk(b, nv), 0)),
                      pl.BlockSpec((1, d, FFN_TILE),
                                   lambda b, f, be, nv: (be[blk(b, nv)], 0, ftile(b, f, nv))),
                      pl.BlockSpec((1, d, FFN_TILE),
                                   lambda b, f, be, nv: (be[blk(b, nv)], 0, nf + ftile(b, f, nv))),
                      pl.BlockSpec((1, FFN_TILE, d),
                                   lambda b, f, be, nv: (be[blk(b, nv)], ftile(b, f, nv), 0))],
            out_specs=pl.BlockSpec(row_block, lambda b, f, be, nv: (b, 0)),
            scratch_shapes=[pltpu.VMEM((MOE_ROWS, d), F32)]),
        out_shape=jax.ShapeDtypeStruct((n_rows * ROW_TILE, LANES), F32),
        compiler_params=_params(("arbitrary", "arbitrary")),
        name="moe_experts",
    )(block_e, n_valid, buf, w13, w13, w2)


def _combine_ln_body(dest_ref, h_ref, route_ref, g_ref, b_ref, yb_ref, o_ref, rows_ref, sem,
                     *, tm, n_tok):
    i = pl.program_id(0)

    def gather_tile(tile, buf):
        base = tile * tm

        def issue(r, c):
            for k in range(TOP_K):
                pltpu.make_async_copy(_row_tile(yb_ref, dest_ref[k * n_tok + base + r]),
                                      _row_tile(rows_ref.at[buf, k], r), sem.at[buf]).start(priority=k)
            return c

        lax.fori_loop(0, tm, issue, 0, unroll=16)

    @pl.when(i == 0)
    def _():
        gather_tile(0, 0)

    @pl.when(i + 1 < pl.num_programs(0))
    def _():
        gather_tile(i + 1, (i + 1) % 2)

    buf = i % 2
    pltpu.make_async_copy(rows_ref.at[buf], rows_ref.at[buf], sem.at[buf]).wait()
    route = route_ref[...]
    y = (route[:, 4:5] * _load_row_tiles(rows_ref.at[buf, 0], tm)
         + route[:, 5:6] * _load_row_tiles(rows_ref.at[buf, 1], tm))
    z = DEEPNORM_ALPHA * h_ref[...] + y
    o_ref[...] = _layer_norm(z, g_ref[...], b_ref[...])


def _combine_ln(dest, h, route, yb, g, b, tm):
    m, d = h.shape
    return pl.pallas_call(
        functools.partial(_combine_ln_body, tm=tm, n_tok=m),
        grid_spec=pltpu.PrefetchScalarGridSpec(
            num_scalar_prefetch=1,
            grid=(m // tm,),
            in_specs=[pl.BlockSpec((tm, d), lambda i, dest: (i, 0)),
                      pl.BlockSpec((tm, LANES), lambda i, dest: (i, 0)),
                      pl.BlockSpec((1, d), lambda i, dest: (0, 0)),
                      pl.BlockSpec((1, d), lambda i, dest: (0, 0)),
                      pl.BlockSpec(memory_space=pl.ANY)],
            out_specs=pl.BlockSpec((tm, d), lambda i, dest: (i, 0)),
            scratch_shapes=[pltpu.VMEM((2, TOP_K, tm * ROW_TILE, LANES), F32),
                            pltpu.SemaphoreType.DMA((2,))]),
        out_shape=jax.ShapeDtypeStruct((m, d), F32),
        compiler_params=_params(("arbitrary",)),
        name="moe_combine_ln",
    )(dest, h, route, g.reshape(1, -1), b.reshape(1, -1), yb)


def _moe_layer(h, w_router, w13, w2, g, b):
    n_tok, _ = h.shape
    route, cnt = _router(h, w_router, tm=512)
    counts = cnt[0, :N_EXPERTS].astype(jnp.int32)
    padded = (counts + MOE_ROWS - 1) // MOE_ROWS * MOE_ROWS
    pstart = jnp.cumsum(padded) - padded
    experts = route[:, 0:2].astype(jnp.int32)
    ranks = route[:, 2:4].astype(jnp.int32)
    dest = (pstart[experts] + ranks).T.reshape(-1)
    n_rows = n_tok * TOP_K + N_EXPERTS * MOE_ROWS
    n_blocks = n_rows // MOE_ROWS
    block_end = jnp.cumsum(padded // MOE_ROWS)
    block_e = jnp.sum(jnp.arange(n_blocks)[:, None] >= block_end[None, :], axis=1)
    block_e = jnp.minimum(block_e, N_EXPERTS - 1).astype(jnp.int32)
    n_valid = block_end[-1:].astype(jnp.int32)
    meta = jnp.concatenate([pstart, counts, padded]).astype(jnp.int32)
    buf = _dispatch(dest, meta, h, n_rows, tm=512)
    yb = _moe(block_e, n_valid, buf, w13, w2)
    return _combine_ln(dest, h, route, yb, g, b, tm=512)


def kernel(x, mem, a_w_in, a_conv_w, a_conv_b, a_w_rgate, a_b_rgate, a_w_igate, a_b_igate, a_lambda,
           a_w_out, w_kv_shared, b_w_q, b_lambda, b_subln_g, b_w_out, mem_w_kv, ffn_w13, ffn_w2,
           moe_router, moe_w13, moe_w2, ln_g, ln_b):
    bsz, t, d = x.shape
    n_tok = bsz * t
    n_mem = mem.shape[1]
    bf = lambda a: a.astype(BF16)

    kv_mem = _proj(mem.reshape(bsz * n_mem, d), bf(jnp.concatenate([mem_w_kv[0], mem_w_kv[1]], axis=1)),
                   [(0, 2 * MEM_W), (2 * MEM_W, 4 * MEM_W)], [None, None], [BF16, BF16],
                   tm=min(1024, bsz * n_mem), name="mem_kv")
    kv_mem = [a.reshape(bsz, n_mem, 2 * MEM_W) for a in kv_mem]

    x2 = x.reshape(n_tok, d)
    gate, u_pre, q_mem = _proj(
        x2, bf(a_w_in[0]), [(0, D_RNN), (D_RNN, 2 * D_RNN), (2 * D_RNN, 2 * D_RNN + MEM_W)],
        [jax.nn.gelu, None, None], [BF16, BF16, BF16], tm=512, name="proj_in")
    w_ri = bf(jnp.concatenate([a_w_rgate[0], a_w_igate[0]], axis=-1))
    rnn = _rglru(u_pre.reshape(bsz, t, D_RNN), gate.reshape(bsz, t, D_RNN), a_conv_w[0], a_conv_b[0],
                 w_ri, a_b_rgate[0], a_b_igate[0], a_lambda[0], tt=512)
    h = _outproj_ln(rnn, q_mem.reshape(bsz, t, MEM_W), kv_mem[0], bf(a_w_out[0]), x,
                    ln_g[0, 0], ln_b[0, 0], tm=512, name="outproj_ln_a")
    h = _ffn_ln(h.reshape(n_tok, d), bf(ffn_w13[0]), bf(ffn_w2[0]), ln_g[0, 1], ln_b[0, 1], tm=512)

    layer = 1
    lam_init = 0.8 - 0.6 * math.exp(-0.3 * layer)
    w_cat = bf(jnp.concatenate([w_kv_shared[:, DIFF_QK_W:], b_w_q[0]], axis=1))
    w_k_t = bf(w_kv_shared[:, :DIFF_QK_W].T)
    v_sh, k_t, q_diff, q_mem = _proj_kvq(h, w_cat, w_k_t, bsz, t, tm=512)
    attn = _diffattn(q_diff.reshape(bsz, t, DIFF_QK_W), k_t, v_sh.reshape(bsz, t, DIFF_V_W),
                     b_lambda[0], b_subln_g[0], lam_init, tq=256, tk=256)
    h = _outproj_ln(attn, q_mem.reshape(bsz, t, MEM_W), kv_mem[1], bf(b_w_out[0]),
                    h.reshape(bsz, t, d), ln_g[1, 0], ln_b[1, 0], tm=512, name="outproj_ln_b")
    out = _moe_layer(h.reshape(n_tok, d), moe_router[0], bf(moe_w13[0]), bf(moe_w2[0]),
                     ln_g[1, 1], ln_b[1, 1])
    return out.reshape(bsz, t, d)
```

```python
import functools
import math

import jax
import jax.numpy as jnp
from jax import lax
from jax.experimental import pallas as pl
from jax.experimental.pallas import tpu as pltpu

F32 = jnp.float32
BF16 = jnp.bfloat16

D_MODEL = 1024
DEPTH = 2
D_RNN = D_MODEL
RNN_BLOCKS = 8
RNN_BLOCK_W = D_RNN // RNN_BLOCKS
CONV_W = 4
LRU_C = 8.0
MEM_HEADS = 4
MEM_HEAD_DIM = D_MODEL // 8
MEM_W = MEM_HEADS * MEM_HEAD_DIM
DIFF_HEADS = 8
DIFF_HEAD_DIM = D_MODEL // 16
DIFF_V_DIM = 2 * DIFF_HEAD_DIM
DIFF_QK_W = DIFF_HEADS * 2 * DIFF_HEAD_DIM
DIFF_V_W = DIFF_HEADS * DIFF_V_DIM
FFN_DIM = (7 * D_MODEL) // 2
N_EXPERTS = 8
TOP_K = 2
LN_EPS = 1e-5
DEEPNORM_ALPHA = (2.0 * DEPTH) ** 0.25

LANES = 128
SUBLANES = 8
VMEM_LIMIT = 52 * 1024 * 1024

FFN_TILE = 1792
MOE_ROWS = 512
NEG_BIG = -1e30
LOG2E = math.log2(math.e)


def _params(semantics):
    return pltpu.CompilerParams(dimension_semantics=semantics, vmem_limit_bytes=VMEM_LIMIT)


def _layer_norm(z, g, b):
    mu = jnp.mean(z, axis=-1, keepdims=True)
    zc = z - mu
    var = jnp.mean(zc * zc, axis=-1, keepdims=True)
    return zc * lax.rsqrt(var + LN_EPS) * g + b


def _proj_body(x_ref, w_ref, *o_refs, splits, post):
    xb = x_ref[...].astype(BF16)
    for o_ref, (c0, c1), fn in zip(o_refs, splits, post):
        z = jnp.dot(xb, w_ref[:, c0:c1], preferred_element_type=F32)
        if fn is not None:
            z = fn(z)
        o_ref[...] = z.astype(o_ref.dtype)


def _proj(x, w, splits, post, out_dtypes, tm, name):
    m, k = x.shape
    return pl.pallas_call(
        functools.partial(_proj_body, splits=tuple(splits), post=tuple(post)),
        grid=(m // tm,),
        in_specs=[pl.BlockSpec((tm, k), lambda i: (i, 0)),
                  pl.BlockSpec(w.shape, lambda i: (0, 0))],
        out_specs=[pl.BlockSpec((tm, c1 - c0), lambda i: (i, 0)) for c0, c1 in splits],
        out_shape=[jax.ShapeDtypeStruct((m, c1 - c0), dt) for (c0, c1), dt in zip(splits, out_dtypes)],
        compiler_params=_params(("parallel",)),
        name=name,
    )(x, w)


def _proj_kvq_body(x_ref, w_ref, wkt_ref, v_ref, kt_ref, qd_ref, qm_ref):
    xb = x_ref[...].astype(BF16)
    q_scale = DIFF_HEAD_DIM ** -0.5 * LOG2E
    v_ref[...] = jnp.dot(xb, w_ref[:, 0:DIFF_V_W], preferred_element_type=F32).astype(BF16)
    kt_ref[0] = lax.dot_general(wkt_ref[...], xb, (((1,), (1,)), ((), ())),
                                preferred_element_type=F32).astype(BF16)
    qd = jnp.dot(xb, w_ref[:, DIFF_V_W:DIFF_V_W + DIFF_QK_W], preferred_element_type=F32)
    qd_ref[...] = (qd * q_scale).astype(BF16)
    qm_ref[...] = jnp.dot(xb, w_ref[:, DIFF_V_W + DIFF_QK_W:], preferred_element_type=F32).astype(BF16)


def _proj_kvq(h, w_cat, w_k_t, bsz, t, tm):
    m, d = h.shape
    nt = t // tm
    rows = lambda width: pl.BlockSpec((tm, width), lambda i: (i, 0))
    return pl.pallas_call(
        _proj_kvq_body,
        grid=(m // tm,),
        in_specs=[rows(d), pl.BlockSpec(w_cat.shape, lambda i: (0, 0)),
                  pl.BlockSpec(w_k_t.shape, lambda i: (0, 0))],
        out_specs=[rows(DIFF_V_W), pl.BlockSpec((1, DIFF_QK_W, tm), lambda i: (i // nt, 0, i % nt)),
                   rows(DIFF_QK_W), rows(MEM_W)],
        out_shape=[jax.ShapeDtypeStruct((m, DIFF_V_W), BF16),
                   jax.ShapeDtypeStruct((bsz, DIFF_QK_W, t), BF16),
                   jax.ShapeDtypeStruct((m, DIFF_QK_W), BF16),
                   jax.ShapeDtypeStruct((m, MEM_W), BF16)],
        compiler_params=_params(("parallel",)),
        name="proj_kvq",
    )(h, w_cat, w_k_t)


def _rglru_body(u_ref, g_ref, cw_ref, cb_ref, wri_ref, br_ref, bi_ref, lam_ref, o_ref,
                ext_ref, rec_ref, a_ref, b_ref, h_ref, *, tt):
    t = pl.program_id(1)

    @pl.when(t == 0)
    def _():
        ext_ref[0:SUBLANES, :] = jnp.zeros((SUBLANES, D_RNN), F32)
        h_ref[...] = jnp.zeros((SUBLANES, D_RNN), F32)

    ext_ref[SUBLANES:, :] = u_ref[0].astype(F32)
    rec = cb_ref[...] + cw_ref[CONV_W - 1:CONV_W, :] * ext_ref[SUBLANES:SUBLANES + tt, :]
    for j in range(CONV_W - 1):
        off = SUBLANES - (CONV_W - 1) + j
        rec = rec + cw_ref[j:j + 1, :] * ext_ref[off:off + tt, :]
    rec_ref[...] = rec
    ext_ref[0:SUBLANES, :] = ext_ref[tt:tt + SUBLANES, :]

    sp = jax.nn.softplus(-lam_ref[...])
    first_row = (lax.broadcasted_iota(jnp.int32, (SUBLANES, RNN_BLOCK_W), 0) + t * tt) == 0
    for n in range(RNN_BLOCKS):
        blk = slice(n * RNN_BLOCK_W, (n + 1) * RNN_BLOCK_W)
        u_n = rec_ref[:, blk]
        ri = jnp.dot(u_n.astype(BF16), wri_ref[n], preferred_element_type=F32)
        r_t = jnp.tanh(0.5 * (ri[:, :RNN_BLOCK_W] + br_ref[:, blk]))
        i = 0.5 * jnp.tanh(0.5 * (ri[:, RNN_BLOCK_W:] + bi_ref[:, blk])) + 0.5
        half = (-0.5 * LRU_C) * sp[:, blk]
        log_a = half * r_t + half
        a = jnp.exp(log_a)
        th = jnp.tanh(-log_a)
        m2 = 2.0 * th / (1.0 + th)
        mult = jnp.where(m2 > 0.0, m2 * lax.rsqrt(m2), 0.0)
        mult = jnp.concatenate([jnp.where(first_row, 1.0, mult[:SUBLANES]), mult[SUBLANES:]], axis=0)
        a_ref[:, blk] = a
        b_ref[:, blk] = mult * (i * u_n)

    row = lax.broadcasted_iota(jnp.int32, (SUBLANES, D_RNN), 0)

    def tile(i, h_prev):
        r0 = pl.multiple_of(i * SUBLANES, SUBLANES)
        a = a_ref[pl.ds(r0, SUBLANES), :]
        b = b_ref[pl.ds(r0, SUBLANES), :]
        for s in (1, 2, 4):
            a_s = pltpu.roll(a, s, 0)
            b_s = pltpu.roll(b, s, 0)
            keep = row >= s
            b = jnp.where(keep, a * b_s + b, b)
            a = jnp.where(keep, a * a_s, a)
        h = b + a * h_prev
        b_ref[pl.ds(r0, SUBLANES), :] = h
        return jnp.broadcast_to(h[SUBLANES - 1:SUBLANES, :], (SUBLANES, D_RNN))

    h_ref[...] = lax.fori_loop(0, tt // SUBLANES, tile, h_ref[...], unroll=4)
    o_ref[0] = (b_ref[...] * g_ref[0].astype(F32)).astype(o_ref.dtype)


def _rglru(u_pre, gate, conv_w, conv_b, w_ri, b_r, b_i, lam, tt):
    bsz, t, c = u_pre.shape
    row = lambda a: a.reshape(1, c)
    full = lambda shape: pl.BlockSpec(shape, lambda b, i: (0,) * len(shape))
    return pl.pallas_call(
        functools.partial(_rglru_body, tt=tt),
        grid=(bsz, t // tt),
        in_specs=[pl.BlockSpec((1, tt, c), lambda b, i: (b, i, 0)),
                  pl.BlockSpec((1, tt, c), lambda b, i: (b, i, 0)),
                  full((CONV_W, c)), full((1, c)), full(w_ri.shape),
                  full((1, c)), full((1, c)), full((1, c))],
        out_specs=pl.BlockSpec((1, tt, c), lambda b, i: (b, i, 0)),
        out_shape=jax.ShapeDtypeStruct((bsz, t, c), BF16),
        scratch_shapes=[pltpu.VMEM((tt + SUBLANES, c), F32), pltpu.VMEM((tt, c), F32),
                        pltpu.VMEM((tt, c), F32), pltpu.VMEM((tt, c), F32),
                        pltpu.VMEM((SUBLANES, c), F32)],
        compiler_params=_params(("parallel", "arbitrary")),
        name="rglru",
    )(u_pre, gate, conv_w, row(conv_b), w_ri, row(b_r), row(b_i), row(lam))


def _outproj_ln_body(a_ref, qm_ref, kv_ref, w_ref, res_ref, g_ref, b_ref, o_ref):
    mix_w = a_ref.shape[-1]
    y = jnp.dot(a_ref[0], w_ref[0:mix_w, :], preferred_element_type=F32)
    scale = MEM_HEAD_DIM ** -0.5 * LOG2E
    heads = []
    for h in range(MEM_HEADS):
        hs = slice(h * MEM_HEAD_DIM, (h + 1) * MEM_HEAD_DIM)
        vs = slice(MEM_W + h * MEM_HEAD_DIM, MEM_W + (h + 1) * MEM_HEAD_DIM)
        s = lax.dot_general(qm_ref[0, :, hs], kv_ref[0, :, hs], (((1,), (1,)), ((), ())),
                            preferred_element_type=F32)
        e = jnp.exp2((s - jnp.max(s, axis=-1, keepdims=True)) * scale)
        pv = jnp.dot(e.astype(BF16), kv_ref[0, :, vs], preferred_element_type=F32)
        heads.append(pv / jnp.sum(e, axis=-1, keepdims=True))
    mem_out = jnp.concatenate(heads, axis=1).astype(BF16)
    y = y + jnp.dot(mem_out, w_ref[mix_w:, :], preferred_element_type=F32)
    z = DEEPNORM_ALPHA * res_ref[0] + y
    o_ref[0] = _layer_norm(z, g_ref[...], b_ref[...])


def _outproj_ln(mix, q_mem, kv_mem, w_out, resid, g, b, tm, name):
    bsz, t, mix_w = mix.shape
    n_mem = kv_mem.shape[1]
    return pl.pallas_call(
        _outproj_ln_body,
        grid=(bsz, t // tm),
        in_specs=[pl.BlockSpec((1, tm, mix_w), lambda b, i: (b, i, 0)),
                  pl.BlockSpec((1, tm, MEM_W), lambda b, i: (b, i, 0)),
                  pl.BlockSpec((1, n_mem, 2 * MEM_W), lambda b, i: (b, 0, 0)),
                  pl.BlockSpec(w_out.shape, lambda b, i: (0, 0)),
                  pl.BlockSpec((1, tm, D_MODEL), lambda b, i: (b, i, 0)),
                  pl.BlockSpec((1, D_MODEL), lambda b, i: (0, 0)),
                  pl.BlockSpec((1, D_MODEL), lambda b, i: (0, 0))],
        out_specs=pl.BlockSpec((1, tm, D_MODEL), lambda b, i: (b, i, 0)),
        out_shape=jax.ShapeDtypeStruct((bsz, t, D_MODEL), F32),
        compiler_params=_params(("parallel", "parallel")),
        name=name,
    )(mix, q_mem, kv_mem, w_out, resid, g.reshape(1, -1), b.reshape(1, -1))


def _ffn_ln_body(h_ref, w13_ref, w2_ref, g_ref, b_ref, o_ref):
    h = h_ref[...]
    hb = h.astype(BF16)
    acc = None
    for c in range(FFN_DIM // FFN_TILE):
        cols = slice(c * FFN_TILE, (c + 1) * FFN_TILE)
        up_cols = slice(FFN_DIM + c * FFN_TILE, FFN_DIM + (c + 1) * FFN_TILE)
        gate = jnp.dot(hb, w13_ref[:, cols], preferred_element_type=F32)
        up = jnp.dot(hb, w13_ref[:, up_cols], preferred_element_type=F32)
        act = (jax.nn.silu(gate) * up).astype(BF16)
        part = jnp.dot(act, w2_ref[cols, :], preferred_element_type=F32)
        acc = part if acc is None else acc + part
    o_ref[...] = _layer_norm(DEEPNORM_ALPHA * h + acc, g_ref[...], b_ref[...])


def _ffn_ln(h, w13, w2, g, b, tm):
    m, d = h.shape
    resident = lambda shape: pl.BlockSpec(shape, lambda i: (0, 0), pipeline_mode=pl.Buffered(1))
    return pl.pallas_call(
        _ffn_ln_body,
        grid=(m // tm,),
        in_specs=[pl.BlockSpec((tm, d), lambda i: (i, 0)),
                  resident(w13.shape), resident(w2.shape),
                  pl.BlockSpec((1, d), lambda i: (0, 0)),
                  pl.BlockSpec((1, d), lambda i: (0, 0))],
        out_specs=pl.BlockSpec((tm, d), lambda i: (i, 0)),
        out_shape=jax.ShapeDtypeStruct((m, d), F32),
        compiler_params=_params(("parallel",)),
        name="ffn_ln",
    )(h, w13, w2, g.reshape(1, -1), b.reshape(1, -1))


ATT_STRIP = 128


def _diffattn_body(pi_ref, pj_ref, lamv_ref, sub_ref, coef_ref, q_ref, kt_ref, v_ref, o_ref,
                   kaug_ref, vaug_ref, qq_ref, mask_ref, s_ref, p_ref, c_ref, m_ref, acc_ref,
                   *, tq, tk, n_pairs, unroll, lam_init):
    t = v_ref.shape[1]
    hd = 2 * DIFF_HEAD_DIM
    n_strips = 2 * tq // ATT_STRIP
    ratio = tq // tk

    feat_row = lax.broadcasted_iota(jnp.int32, (LANES, tk), 0)
    in_block = lax.broadcasted_iota(jnp.int32, (LANES, tk), 1)
    for kb in range(t // tk):
        feat = jnp.where(feat_row < 3, kb, jnp.where(feat_row < 6, in_block, 0))
        kaug_ref[kb, 0:hd, :] = kt_ref[0, :, kb * tk:(kb + 1) * tk]
        kaug_ref[kb, hd:, :] = feat.astype(F32).astype(BF16)
    vaug_ref[:, 0:DIFF_V_DIM] = v_ref[0]
    vaug_ref[:, DIFF_V_DIM:] = jnp.ones((t, LANES), BF16)
    lane = lax.broadcasted_iota(jnp.int32, (tq, hd), 1)
    coef_rows = jnp.broadcast_to(coef_ref[0], (2 * tq, LANES))
    for qb in range(t // tq):
        q = q_ref[0, qb * tq:(qb + 1) * tq, :]
        zero = jnp.zeros_like(q)
        qq_ref[qb * 2 * tq:qb * 2 * tq + tq, 0:hd] = jnp.where(lane < DIFF_HEAD_DIM, q, zero)
        qq_ref[qb * 2 * tq + tq:(qb + 1) * 2 * tq, 0:hd] = jnp.where(lane < DIFF_HEAD_DIM, zero, q)
        qq_ref[qb * 2 * tq:(qb + 1) * 2 * tq, hd:] = coef_rows
    q_row = lax.broadcasted_iota(jnp.int32, (tq, tk), 0)
    k_col = lax.broadcasted_iota(jnp.int32, (tq, tk), 1)
    mask_ref[0] = jnp.zeros((tq, tk), F32)
    for v in range(1, ratio + 1):
        mask_ref[v] = jnp.where(k_col + (v - 1) * tk <= q_row, 0.0, NEG_BIG)
    m_ref[...] = jnp.full(m_ref.shape, NEG_BIG, F32)
    acc_ref[...] = jnp.zeros_like(acc_ref)

    def stage_qk(p, slot):
        q0 = pl.multiple_of(pi_ref[p] * (2 * tq), 2 * tq)
        s_ref[slot] = jnp.dot(qq_ref[pl.ds(q0, 2 * tq), :], kaug_ref[pj_ref[p]],
                              preferred_element_type=F32)

    def stage_sm(p, slot):
        i = pi_ref[p]
        diag = jnp.maximum(pj_ref[p] - ratio * i + 1, 0)
        for r in range(n_strips):
            rows = slice(r * ATT_STRIP, (r + 1) * ATT_STRIP)
            mrows = slice((r * ATT_STRIP) % tq, (r * ATT_STRIP) % tq + ATT_STRIP)
            s = s_ref[slot, rows, :] + mask_ref[diag, mrows, :]
            m_prev = m_ref[i, rows, :]
            m_next = jnp.maximum(m_prev, jnp.max(s, axis=-1, keepdims=True))
            p_blk = jnp.exp2(s - jnp.tile(m_next, (1, tk // LANES)))
            p_ref[slot, rows, :] = p_blk.astype(BF16)
            c_ref[slot, rows, :] = jnp.exp2(m_prev - m_next)
            m_ref[i, rows, :] = m_next

    def stage_pv(p, slot):
        i = pi_ref[p]
        k0 = pl.multiple_of(pj_ref[p] * tk, tk)
        pv = jnp.dot(p_ref[slot], vaug_ref[pl.ds(k0, tk), :], preferred_element_type=F32)
        acc_ref[i] = acc_ref[i] * jnp.tile(c_ref[slot], (1, 2)) + pv

    stage_qk(0, 0)
    stage_qk(1, 1)
    stage_sm(0, 0)

    def pipeline_step(step, c):
        for u in range(unroll):
            p = unroll * step + u
            stage_qk(p + 2, u % 2)
            stage_sm(p + 1, (u + 1) % 2)
            stage_pv(p, u % 2)
        return c

    lax.fori_loop(0, n_pairs // unroll, pipeline_step, 0)

    lv = lamv_ref[...]
    lam = (jnp.exp(jnp.sum(lv[0:1] * lv[1:2], axis=-1, keepdims=True))
           - jnp.exp(jnp.sum(lv[2:3] * lv[3:4], axis=-1, keepdims=True)) + lam_init)

    def finalize(i, c):
        o = acc_ref[i, :, 0:DIFF_V_DIM] / acc_ref[i, :, DIFF_V_DIM:]
        o = o[:tq] - lam * o[tq:]
        o = o * lax.rsqrt(jnp.mean(o * o, axis=-1, keepdims=True) + LN_EPS) * sub_ref[...]
        r0 = pl.multiple_of(i * tq, tq)
        o_ref[0, pl.ds(r0, tq), :] = (o * (1.0 - lam_init)).astype(o_ref.dtype)
        return c

    lax.fori_loop(0, t // tq, finalize, 0, unroll=4)


def _diffattn(q, k_t, v, lam_vecs, subln_g, lam_init, tq, tk):
    bsz, t, _ = q.shape
    slopes = jnp.exp2(-8.0 * (jnp.arange(DIFF_HEADS, dtype=F32) + 1.0) / DIFF_HEADS) * LOG2E
    def pieces(c):
        c1 = c.astype(BF16)
        c2 = (c - c1.astype(F32)).astype(BF16)
        c3 = (c - c1.astype(F32) - c2.astype(F32)).astype(BF16)
        return [c1, c2, c3]

    coef = jnp.zeros((DIFF_HEADS, 1, LANES), BF16)
    coef = coef.at[:, 0, 0:6].set(jnp.stack(pieces(slopes * tk) + pieces(slopes), axis=1))
    hd = 2 * DIFF_HEAD_DIM
    nq = t // tq
    nk = t // tk
    ratio = tq // tk
    assert nk <= 256 and tk <= 256, "key block index and in-block position must be exact in bf16"
    pairs = [(i, j) for i in range(nq) for j in range(ratio * (i + 1))]
    n_pairs = len(pairs)
    unroll = next(u for u in (34, 8, 4, 2) if n_pairs % u == 0)
    assert n_pairs % unroll == 0
    pairs = pairs + [pairs[-1]] * 2
    pair_i = jnp.asarray([p[0] for p in pairs], jnp.int32)
    pair_j = jnp.asarray([p[1] for p in pairs], jnp.int32)
    head = lambda shape: pl.BlockSpec(shape, lambda b, h, pi, pj: (b, 0, h))
    return pl.pallas_call(
        functools.partial(_diffattn_body, tq=tq, tk=tk, n_pairs=n_pairs, unroll=unroll, lam_init=lam_init),
        grid_spec=pltpu.PrefetchScalarGridSpec(
            num_scalar_prefetch=2,
            grid=(bsz, DIFF_HEADS),
            in_specs=[pl.BlockSpec(lam_vecs.shape, lambda b, h, pi, pj: (0, 0)),
                      pl.BlockSpec((1, DIFF_V_DIM), lambda b, h, pi, pj: (0, 0)),
                      pl.BlockSpec((1, 1, LANES), lambda b, h, pi, pj: (h, 0, 0)),
                      head((1, t, hd)),
                      pl.BlockSpec((1, hd, t), lambda b, h, pi, pj: (b, h, 0)),
                      head((1, t, DIFF_V_DIM))],
            out_specs=head((1, t, DIFF_V_DIM)),
            scratch_shapes=[pltpu.VMEM((nk, hd + LANES, tk), BF16),
                            pltpu.VMEM((t, DIFF_V_DIM + LANES), BF16),
                            pltpu.VMEM((2 * t, hd + LANES), BF16),
                            pltpu.VMEM((ratio + 1, tq, tk), F32),
                            pltpu.VMEM((2, 2 * tq, tk), F32),
                            pltpu.VMEM((2, 2 * tq, tk), BF16),
                            pltpu.VMEM((2, 2 * tq, LANES), F32),
                            pltpu.VMEM((nq, 2 * tq, LANES), F32),
                            pltpu.VMEM((nq, 2 * tq, DIFF_V_DIM + LANES), F32)]),
        out_shape=jax.ShapeDtypeStruct((bsz, t, DIFF_V_W), BF16),
        compiler_params=_params(("parallel", "parallel")),
        name="diffattn",
    )(pair_i, pair_j, lam_vecs, subln_g.reshape(1, -1), coef, q, k_t, v)


def _router_body(h_ref, w_ref, o_ref, cnt_ref, carry_ref, *, tm):
    @pl.when(pl.program_id(0) == 0)
    def _():
        carry_ref[...] = jnp.zeros_like(carry_ref)

    h = h_ref[...]
    w = w_ref[...]
    h_hi = h.astype(BF16)
    h_lo = (h - h_hi.astype(F32)).astype(BF16)
    w_hi = w.astype(BF16)
    w_lo = (w - w_hi.astype(F32)).astype(BF16)
    logits = (jnp.dot(h_hi, w_hi, preferred_element_type=F32)
              + (jnp.dot(h_hi, w_lo, preferred_element_type=F32)
                 + jnp.dot(h_lo, w_hi, preferred_element_type=F32)))
    lane = lax.broadcasted_iota(jnp.int32, (tm, LANES), 1)
    lg = jnp.where(lane < N_EXPERTS, logits, -jnp.inf)
    v1 = jnp.max(lg, axis=-1, keepdims=True)
    i1 = jnp.min(jnp.where(lg == v1, lane, LANES), axis=-1, keepdims=True)
    oh1 = lane == i1
    lg2 = jnp.where(oh1, -jnp.inf, lg)
    v2 = jnp.max(lg2, axis=-1, keepdims=True)
    i2 = jnp.min(jnp.where(lg2 == v2, lane, LANES), axis=-1, keepdims=True)
    oh2 = lane == i2
    e2 = jnp.exp(v2 - v1)
    g1 = 1.0 / (1.0 + e2)
    g2 = e2 / (1.0 + e2)

    both = jnp.where(oh1, 1.0, jnp.where(oh2, 1.0, 0.0))
    tri = (lax.broadcasted_iota(jnp.int32, (tm, tm), 0)
           > lax.broadcasted_iota(jnp.int32, (tm, tm), 1))
    tri = jnp.where(tri, 1.0, 0.0).astype(BF16)
    before = jnp.dot(tri, both.astype(BF16), preferred_element_type=F32) + carry_ref[0:1, :]
    rank1 = jnp.sum(jnp.where(oh1, before, 0.0), axis=-1, keepdims=True)
    rank2 = jnp.sum(jnp.where(oh2, before, 0.0), axis=-1, keepdims=True)
    total = carry_ref[...] + jnp.sum(both, axis=0, keepdims=True)
    carry_ref[...] = total
    cnt_ref[...] = total

    out = jnp.where(lane == 0, i1.astype(F32), 0.0)
    out = jnp.where(lane == 1, i2.astype(F32), out)
    out = jnp.where(lane == 2, rank1, out)
    out = jnp.where(lane == 3, rank2, out)
    out = jnp.where(lane == 4, g1, out)
    out = jnp.where(lane == 5, g2, out)
    o_ref[...] = out


def _router(h, w_router, tm):
    m, d = h.shape
    w = jnp.zeros((d, LANES), F32).at[:, :N_EXPERTS].set(w_router)
    return pl.pallas_call(
        functools.partial(_router_body, tm=tm),
        grid=(m // tm,),
        in_specs=[pl.BlockSpec((tm, d), lambda i: (i, 0)),
                  pl.BlockSpec((d, LANES), lambda i: (0, 0))],
        out_specs=[pl.BlockSpec((tm, LANES), lambda i: (i, 0)),
                   pl.BlockSpec((SUBLANES, LANES), lambda i: (0, 0))],
        out_shape=[jax.ShapeDtypeStruct((m, LANES), F32),
                   jax.ShapeDtypeStruct((SUBLANES, LANES), F32)],
        scratch_shapes=[pltpu.VMEM((SUBLANES, LANES), F32)],
        compiler_params=_params(("arbitrary",)),
        name="router",
    )(h, w)


ROW_TILE = D_MODEL // LANES


def _row_tile(ref, r):
    start = r * ROW_TILE if isinstance(r, int) else pl.multiple_of(r * ROW_TILE, ROW_TILE)
    return ref.at[pl.ds(start, ROW_TILE), :]


def _store_row_tiles(tile_ref, x):
    for j in range(ROW_TILE):
        tile_ref[pl.ds(j, x.shape[0], stride=ROW_TILE), :] = x[:, j * LANES:(j + 1) * LANES]


def _load_row_tiles(tile_ref, n):
    return jnp.concatenate([tile_ref[pl.ds(j, n, stride=ROW_TILE), :] for j in range(ROW_TILE)], axis=1)


def _dispatch_body(dest_ref, meta_ref, h_ref, buf_ref, rows_ref, zero_ref, sem, *, tm, n_tok, n_rows):
    i = pl.program_id(0)
    base = i * tm
    _store_row_tiles(rows_ref, h_ref[...])

    def row_copy(r, d):
        return pltpu.make_async_copy(_row_tile(rows_ref, r), _row_tile(buf_ref, d), sem)

    def issue(r, c):
        row_copy(r, dest_ref[base + r]).start(priority=0)
        row_copy(r, dest_ref[n_tok + base + r]).start(priority=1)
        return c

    lax.fori_loop(0, tm, issue, 0, unroll=16)
    all_rows = buf_ref.at[pl.ds(0, TOP_K * tm * ROW_TILE), :]
    pltpu.make_async_copy(all_rows, all_rows, sem).wait()

    @pl.when(i == pl.num_programs(0) - 1)
    def _():
        zero_ref[...] = jnp.zeros_like(zero_ref)

        def zero_row(r):
            return pltpu.make_async_copy(_row_tile(zero_ref, 0), _row_tile(buf_ref, r), sem)

        def start_zero_row(r, c):
            zero_row(r).start()
            return c

        def wait_zero_row(r, c):
            zero_row(0).wait()
            return c

        for e in range(N_EXPERTS):
            lo = meta_ref[e] + meta_ref[N_EXPERTS + e]
            hi = meta_ref[e] + meta_ref[2 * N_EXPERTS + e]
            lax.fori_loop(lo, hi, start_zero_row, 0)
            lax.fori_loop(lo, hi, wait_zero_row, 0)

        def zero_block(b, c):
            r0 = pl.multiple_of(b * (MOE_ROWS * ROW_TILE), MOE_ROWS * ROW_TILE)
            cp = pltpu.make_async_copy(zero_ref, buf_ref.at[pl.ds(r0, MOE_ROWS * ROW_TILE), :], sem)
            cp.start()
            cp.wait()
            return c

        used = meta_ref[N_EXPERTS - 1] + meta_ref[3 * N_EXPERTS - 1]
        lax.fori_loop(used // MOE_ROWS, n_rows // MOE_ROWS, zero_block, 0)


def _dispatch(dest, meta, h, n_rows, tm):
    m, d = h.shape
    return pl.pallas_call(
        functools.partial(_dispatch_body, tm=tm, n_tok=m, n_rows=n_rows),
        grid_spec=pltpu.PrefetchScalarGridSpec(
            num_scalar_prefetch=2,
            grid=(m // tm,),
            in_specs=[pl.BlockSpec((tm, d), lambda i, dest, meta: (i, 0))],
            out_specs=pl.BlockSpec(memory_space=pl.ANY),
            scratch_shapes=[pltpu.VMEM((tm * ROW_TILE, LANES), F32),
                            pltpu.VMEM((MOE_ROWS * ROW_TILE, LANES), F32),
                            pltpu.SemaphoreType.DMA(())]),
        out_shape=jax.ShapeDtypeStruct((n_rows * ROW_TILE, LANES), F32),
        compiler_params=_params(("arbitrary",)),
        name="moe_dispatch",
    )(dest, meta, h)


def _moe_body(be_ref, nv_ref, x_ref, w1_ref, w3_ref, w2_ref, o_ref, acc_ref):
    b = pl.program_id(0)
    f = pl.program_id(1)

    @pl.when(jnp.logical_and(b == 0, f == 0))
    def _():
        acc_ref[...] = jnp.zeros_like(acc_ref)

    @pl.when(b < nv_ref[0])
    def _():
        xb = _load_row_tiles(x_ref, MOE_ROWS).astype(BF16)
        gate = jnp.dot(xb, w1_ref[0], preferred_element_type=F32)
        up = jnp.dot(xb, w3_ref[0], preferred_element_type=F32)
        act = (jax.nn.silu(gate) * up).astype(BF16)
        acc = jnp.where(f == 0, 0.0, acc_ref[...]) + jnp.dot(act, w2_ref[0], preferred_element_type=F32)
        acc_ref[...] = acc
        _store_row_tiles(o_ref, acc)

    @pl.when(jnp.logical_and(b >= nv_ref[0], f == pl.num_programs(1) - 1))
    def _():
        o_ref[...] = jnp.zeros_like(o_ref)


def _moe(block_e, n_valid, buf, w13, w2):
    d = D_MODEL
    n_rows = buf.shape[0] // ROW_TILE
    nb = n_rows // MOE_ROWS
    nf = FFN_DIM // FFN_TILE
    row_block = (MOE_ROWS * ROW_TILE, LANES)

    def blk(b, nv):
        return jnp.minimum(b, nv[0] - 1)

    def ftile(b, f, nv):
        return jnp.where(b < nv[0], f, nf - 1)

    return pl.pallas_call(
        _moe_body,
        grid_spec=pltpu.PrefetchScalarGridSpec(
            num_scalar_prefetch=2,
            grid=(nb, nf),
            in_specs=[pl.BlockSpec(row_block, lambda b, f, be, nv: (blk(b, nv), 0)),
                      pl.BlockSpec((1, d, FFN_TILE),
                                   lambda b, f, be, nv: (be[blk(b, nv)], 0, ftile(b, f, nv))),
                      pl.BlockSpec((1, d, FFN_TILE),
                                   lambda b, f, be, nv: (be[blk(b, nv)], 0, nf + ftile(b, f, nv))),
                      pl.BlockSpec((1, FFN_TILE, d),
                                   lambda b, f, be, nv: (be[blk(b, nv)], ftile(b, f, nv), 0))],
            out_specs=pl.BlockSpec(row_block, lambda b, f, be, nv: (b, 0)),
            scratch_shapes=[pltpu.VMEM((MOE_ROWS, d), F32)]),
        out_shape=jax.ShapeDtypeStruct((n_rows * ROW_TILE, LANES), F32),
        compiler_params=_params(("arbitrary", "arbitrary")),
        name="moe_experts",
    )(block_e, n_valid, buf, w13, w13, w2)


def _combine_ln_body(dest_ref, h_ref, route_ref, g_ref, b_ref, yb_ref, o_ref, rows_ref, sem,
                     *, tm, n_tok):
    i = pl.program_id(0)

    def gather_tile(tile, buf):
        base = tile * tm

        def issue(r, c):
            for k in range(TOP_K):
                pltpu.make_async_copy(_row_tile(yb_ref, dest_ref[k * n_tok + base + r]),
                                      _row_tile(rows_ref.at[buf, k], r), sem.at[buf]).start(priority=k)
            return c

        lax.fori_loop(0, tm, issue, 0, unroll=16)

    @pl.when(i == 0)
    def _():
        gather_tile(0, 0)

    @pl.when(i + 1 < pl.num_programs(0))
    def _():
        gather_tile(i + 1, (i + 1) % 2)

    buf = i % 2
    pltpu.make_async_copy(rows_ref.at[buf], rows_ref.at[buf], sem.at[buf]).wait()
    route = route_ref[...]
    y = (route[:, 4:5] * _load_row_tiles(rows_ref.at[buf, 0], tm)
         + route[:, 5:6] * _load_row_tiles(rows_ref.at[buf, 1], tm))
    z = DEEPNORM_ALPHA * h_ref[...] + y
    o_ref[...] = _layer_norm(z, g_ref[...], b_ref[...])


def _combine_ln(dest, h, route, yb, g, b, tm):
    m, d = h.shape
    return pl.pallas_call(
        functools.partial(_combine_ln_body, tm=tm, n_tok=m),
        grid_spec=pltpu.PrefetchScalarGridSpec(
            num_scalar_prefetch=1,
            grid=(m // tm,),
            in_specs=[pl.BlockSpec((tm, d), lambda i, dest: (i, 0)),
                      pl.BlockSpec((tm, LANES), lambda i, dest: (i, 0)),
                      pl.BlockSpec((1, d), lambda i, dest: (0, 0)),
                      pl.BlockSpec((1, d), lambda i, dest: (0, 0)),
                      pl.BlockSpec(memory_space=pl.ANY)],
            out_specs=pl.BlockSpec((tm, d), lambda i, dest: (i, 0)),
            scratch_shapes=[pltpu.VMEM((2, TOP_K, tm * ROW_TILE, LANES), F32),
                            pltpu.SemaphoreType.DMA((2,))]),
        out_shape=jax.ShapeDtypeStruct((m, d), F32),
        compiler_params=_params(("arbitrary",)),
        name="moe_combine_ln",
    )(dest, h, route, g.reshape(1, -1), b.reshape(1, -1), yb)


def _moe_layer(h, w_router, w13, w2, g, b):
    n_tok, _ = h.shape
    route, cnt = _router(h, w_router, tm=512)
    counts = cnt[0, :N_EXPERTS].astype(jnp.int32)
    padded = (counts + MOE_ROWS - 1) // MOE_ROWS * MOE_ROWS
    pstart = jnp.cumsum(padded) - padded
    experts = route[:, 0:2].astype(jnp.int32)
    ranks = route[:, 2:4].astype(jnp.int32)
    dest = (pstart[experts] + ranks).T.reshape(-1)
    n_rows = n_tok * TOP_K + N_EXPERTS * MOE_ROWS
    n_blocks = n_rows // MOE_ROWS
    block_end = jnp.cumsum(padded // MOE_ROWS)
    block_e = jnp.sum(jnp.arange(n_blocks)[:, None] >= block_end[None, :], axis=1)
    block_e = jnp.minimum(block_e, N_EXPERTS - 1).astype(jnp.int32)
    n_valid = block_end[-1:].astype(jnp.int32)
    meta = jnp.concatenate([pstart, counts, padded]).astype(jnp.int32)
    buf = _dispatch(dest, meta, h, n_rows, tm=512)
    yb = _moe(block_e, n_valid, buf, w13, w2)
    return _combine_ln(dest, h, route, yb, g, b, tm=512)


def kernel(x, mem, a_w_in, a_conv_w, a_conv_b, a_w_rgate, a_b_rgate, a_w_igate, a_b_igate, a_lambda,
           a_w_out, w_kv_shared, b_w_q, b_lambda, b_subln_g, b_w_out, mem_w_kv, ffn_w13, ffn_w2,
           moe_router, moe_w13, moe_w2, ln_g, ln_b):
    bsz, t, d = x.shape
    n_tok = bsz * t
    n_mem = mem.shape[1]
    bf = lambda a: a.astype(BF16)

    kv_mem = _proj(mem.reshape(bsz * n_mem, d), bf(jnp.concatenate([mem_w_kv[0], mem_w_kv[1]], axis=1)),
                   [(0, 2 * MEM_W), (2 * MEM_W, 4 * MEM_W)], [None, None], [BF16, BF16],
                   tm=min(1024, bsz * n_mem), name="mem_kv")
    kv_mem = [a.reshape(bsz, n_mem, 2 * MEM_W) for a in kv_mem]

    x2 = x.reshape(n_tok, d)
    gate, u_pre, q_mem = _proj(
        x2, bf(a_w_in[0]), [(0, D_RNN), (D_RNN, 2 * D_RNN), (2 * D_RNN, 2 * D_RNN + MEM_W)],
        [jax.nn.gelu, None, None], [BF16, BF16, BF16], tm=1024, name="proj_in")
    w_ri = bf(jnp.concatenate([a_w_rgate[0], a_w_igate[0]], axis=-1))
    rnn = _rglru(u_pre.reshape(bsz, t, D_RNN), gate.reshape(bsz, t, D_RNN), a_conv_w[0], a_conv_b[0],
                 w_ri, a_b_rgate[0], a_b_igate[0], a_lambda[0], tt=512)
    h = _outproj_ln(rnn, q_mem.reshape(bsz, t, MEM_W), kv_mem[0], bf(a_w_out[0]), x,
                    ln_g[0, 0], ln_b[0, 0], tm=1024, name="outproj_ln_a")
    h = _ffn_ln(h.reshape(n_tok, d), bf(ffn_w13[0]), bf(ffn_w2[0]), ln_g[0, 1], ln_b[0, 1], tm=512)

    layer = 1
    lam_init = 0.8 - 0.6 * math.exp(-0.3 * layer)
    w_cat = bf(jnp.concatenate([w_kv_shared[:, DIFF_QK_W:], b_w_q[0]], axis=1))
    w_k_t = bf(w_kv_shared[:, :DIFF_QK_W].T)
    v_sh, k_t, q_diff, q_mem = _proj_kvq(h, w_cat, w_k_t, bsz, t, tm=1024)
    attn = _diffattn(q_diff.reshape(bsz, t, DIFF_QK_W), k_t, v_sh.reshape(bsz, t, DIFF_V_W),
                     b_lambda[0], b_subln_g[0], lam_init, tq=256, tk=256)
    h = _outproj_ln(attn, q_mem.reshape(bsz, t, MEM_W), kv_mem[1], bf(b_w_out[0]),
                    h.reshape(bsz, t, d), ln_g[1, 0], ln_b[1, 0], tm=1024, name="outproj_ln_b")
    out = _moe_layer(h.reshape(n_tok, d), moe_router[0], bf(moe_w13[0]), bf(moe_w2[0]),
                     ln_g[1, 1], ln_b[1, 1])
    return out.reshape(bsz, t, d)
```

```python
import functools
import math

import jax
import jax.numpy as jnp
from jax import lax
from jax.experimental import pallas as pl
from jax.experimental.pallas import tpu as pltpu

F32 = jnp.float32
BF16 = jnp.bfloat16

D_MODEL = 1024
DEPTH = 2
D_RNN = D_MODEL
RNN_BLOCKS = 8
RNN_BLOCK_W = D_RNN // RNN_BLOCKS
CONV_W = 4
LRU_C = 8.0
MEM_HEADS = 4
MEM_HEAD_DIM = D_MODEL // 8
MEM_W = MEM_HEADS * MEM_HEAD_DIM
DIFF_HEADS = 8
DIFF_HEAD_DIM = D_MODEL // 16
DIFF_V_DIM = 2 * DIFF_HEAD_DIM
DIFF_QK_W = DIFF_HEADS * 2 * DIFF_HEAD_DIM
DIFF_V_W = DIFF_HEADS * DIFF_V_DIM
FFN_DIM = (7 * D_MODEL) // 2
N_EXPERTS = 8
TOP_K = 2
LN_EPS = 1e-5
DEEPNORM_ALPHA = (2.0 * DEPTH) ** 0.25

LANES = 128
SUBLANES = 8
VMEM_LIMIT = 52 * 1024 * 1024

FFN_TILE = 1792
MOE_ROWS = 512
NEG_BIG = -1e30
LOG2E = math.log2(math.e)


def _params(semantics):
    return pltpu.CompilerParams(dimension_semantics=semantics, vmem_limit_bytes=VMEM_LIMIT)


def _layer_norm(z, g, b):
    mu = jnp.mean(z, axis=-1, keepdims=True)
    zc = z - mu
    var = jnp.mean(zc * zc, axis=-1, keepdims=True)
    return zc * lax.rsqrt(var + LN_EPS) * g + b


def _proj_body(x_ref, w_ref, *o_refs, splits, post):
    xb = x_ref[...].astype(BF16)
    for o_ref, (c0, c1), fn in zip(o_refs, splits, post):
        z = jnp.dot(xb, w_ref[:, c0:c1], preferred_element_type=F32)
        if fn is not None:
            z = fn(z)
        o_ref[...] = z.astype(o_ref.dtype)


def _proj(x, w, splits, post, out_dtypes, tm, name):
    m, k = x.shape
    return pl.pallas_call(
        functools.partial(_proj_body, splits=tuple(splits), post=tuple(post)),
        grid=(m // tm,),
        in_specs=[pl.BlockSpec((tm, k), lambda i: (i, 0)),
                  pl.BlockSpec(w.shape, lambda i: (0, 0))],
        out_specs=[pl.BlockSpec((tm, c1 - c0), lambda i: (i, 0)) for c0, c1 in splits],
        out_shape=[jax.ShapeDtypeStruct((m, c1 - c0), dt) for (c0, c1), dt in zip(splits, out_dtypes)],
        compiler_params=_params(("parallel",)),
        name=name,
    )(x, w)


def _proj_kvq_body(x_ref, w_ref, wkt_ref, v_ref, kt_ref, qd_ref, qm_ref):
    xb = x_ref[...].astype(BF16)
    q_scale = DIFF_HEAD_DIM ** -0.5 * LOG2E
    v_ref[...] = jnp.dot(xb, w_ref[:, 0:DIFF_V_W], preferred_element_type=F32).astype(BF16)
    kt_ref[0] = lax.dot_general(wkt_ref[...], xb, (((1,), (1,)), ((), ())),
                                preferred_element_type=F32).astype(BF16)
    qd = jnp.dot(xb, w_ref[:, DIFF_V_W:DIFF_V_W + DIFF_QK_W], preferred_element_type=F32)
    qd_ref[...] = (qd * q_scale).astype(BF16)
    qm_ref[...] = jnp.dot(xb, w_ref[:, DIFF_V_W + DIFF_QK_W:], preferred_element_type=F32).astype(BF16)


def _proj_kvq(h, w_cat, w_k_t, bsz, t, tm):
    m, d = h.shape
    nt = t // tm
    rows = lambda width: pl.BlockSpec((tm, width), lambda i: (i, 0))
    return pl.pallas_call(
        _proj_kvq_body,
        grid=(m // tm,),
        in_specs=[rows(d), pl.BlockSpec(w_cat.shape, lambda i: (0, 0)),
                  pl.BlockSpec(w_k_t.shape, lambda i: (0, 0))],
        out_specs=[rows(DIFF_V_W), pl.BlockSpec((1, DIFF_QK_W, tm), lambda i: (i // nt, 0, i % nt)),
                   rows(DIFF_QK_W), rows(MEM_W)],
        out_shape=[jax.ShapeDtypeStruct((m, DIFF_V_W), BF16),
                   jax.ShapeDtypeStruct((bsz, DIFF_QK_W, t), BF16),
                   jax.ShapeDtypeStruct((m, DIFF_QK_W), BF16),
                   jax.ShapeDtypeStruct((m, MEM_W), BF16)],
        compiler_params=_params(("parallel",)),
        name="proj_kvq",
    )(h, w_cat, w_k_t)


def _rglru_body(u_ref, g_ref, cw_ref, cb_ref, wri_ref, br_ref, bi_ref, lam_ref, o_ref,
                ext_ref, rec_ref, a_ref, b_ref, h_ref, *, tt):
    t = pl.program_id(1)

    @pl.when(t == 0)
    def _():
        ext_ref[0:SUBLANES, :] = jnp.zeros((SUBLANES, D_RNN), F32)
        h_ref[...] = jnp.zeros((SUBLANES, D_RNN), F32)

    ext_ref[SUBLANES:, :] = u_ref[0].astype(F32)
    rec = cb_ref[...] + cw_ref[CONV_W - 1:CONV_W, :] * ext_ref[SUBLANES:SUBLANES + tt, :]
    for j in range(CONV_W - 1):
        off = SUBLANES - (CONV_W - 1) + j
        rec = rec + cw_ref[j:j + 1, :] * ext_ref[off:off + tt, :]
    rec_ref[...] = rec
    ext_ref[0:SUBLANES, :] = ext_ref[tt:tt + SUBLANES, :]

    sp = jax.nn.softplus(-lam_ref[...])
    first_row = (lax.broadcasted_iota(jnp.int32, (SUBLANES, RNN_BLOCK_W), 0) + t * tt) == 0
    for n in range(RNN_BLOCKS):
        blk = slice(n * RNN_BLOCK_W, (n + 1) * RNN_BLOCK_W)
        u_n = rec_ref[:, blk]
        ri = jnp.dot(u_n.astype(BF16), wri_ref[n], preferred_element_type=F32)
        r_t = jnp.tanh(0.5 * (ri[:, :RNN_BLOCK_W] + br_ref[:, blk]))
        i = 0.5 * jnp.tanh(0.5 * (ri[:, RNN_BLOCK_W:] + bi_ref[:, blk])) + 0.5
        half = (-0.5 * LRU_C) * sp[:, blk]
        log_a = half * r_t + half
        a = jnp.exp(log_a)
        th = jnp.tanh(-log_a)
        m2 = 2.0 * th / (1.0 + th)
        mult = jnp.where(m2 > 0.0, m2 * lax.rsqrt(m2), 0.0)
        mult = jnp.concatenate([jnp.where(first_row, 1.0, mult[:SUBLANES]), mult[SUBLANES:]], axis=0)
        a_ref[:, blk] = a
        b_ref[:, blk] = mult * (i * u_n)

    row = lax.broadcasted_iota(jnp.int32, (SUBLANES, D_RNN), 0)

    def tile(i, h_prev):
        r0 = pl.multiple_of(i * SUBLANES, SUBLANES)
        a = a_ref[pl.ds(r0, SUBLANES), :]
        b = b_ref[pl.ds(r0, SUBLANES), :]
        for s in (1, 2, 4):
            a_s = pltpu.roll(a, s, 0)
            b_s = pltpu.roll(b, s, 0)
            keep = row >= s
            b = jnp.where(keep, a * b_s + b, b)
            a = jnp.where(keep, a * a_s, a)
        h = b + a * h_prev
        b_ref[pl.ds(r0, SUBLANES), :] = h
        return jnp.broadcast_to(h[SUBLANES - 1:SUBLANES, :], (SUBLANES, D_RNN))

    h_ref[...] = lax.fori_loop(0, tt // SUBLANES, tile, h_ref[...], unroll=4)
    o_ref[0] = (b_ref[...] * g_ref[0].astype(F32)).astype(o_ref.dtype)


def _rglru(u_pre, gate, conv_w, conv_b, w_ri, b_r, b_i, lam, tt):
    bsz, t, c = u_pre.shape
    row = lambda a: a.reshape(1, c)
    full = lambda shape: pl.BlockSpec(shape, lambda b, i: (0,) * len(shape))
    return pl.pallas_call(
        functools.partial(_rglru_body, tt=tt),
        grid=(bsz, t // tt),
        in_specs=[pl.BlockSpec((1, tt, c), lambda b, i: (b, i, 0)),
                  pl.BlockSpec((1, tt, c), lambda b, i: (b, i, 0)),
                  full((CONV_W, c)), full((1, c)), full(w_ri.shape),
                  full((1, c)), full((1, c)), full((1, c))],
        out_specs=pl.BlockSpec((1, tt, c), lambda b, i: (b, i, 0)),
        out_shape=jax.ShapeDtypeStruct((bsz, t, c), BF16),
        scratch_shapes=[pltpu.VMEM((tt + SUBLANES, c), F32), pltpu.VMEM((tt, c), F32),
                        pltpu.VMEM((tt, c), F32), pltpu.VMEM((tt, c), F32),
                        pltpu.VMEM((SUBLANES, c), F32)],
        compiler_params=_params(("parallel", "arbitrary")),
        name="rglru",
    )(u_pre, gate, conv_w, row(conv_b), w_ri, row(b_r), row(b_i), row(lam))


def _outproj_ln_body(a_ref, qm_ref, kv_ref, w_ref, res_ref, g_ref, b_ref, o_ref):
    mix_w = a_ref.shape[-1]
    y = jnp.dot(a_ref[0], w_ref[0:mix_w, :], preferred_element_type=F32)
    scale = MEM_HEAD_DIM ** -0.5 * LOG2E
    heads = []
    for h in range(MEM_HEADS):
        hs = slice(h * MEM_HEAD_DIM, (h + 1) * MEM_HEAD_DIM)
        vs = slice(MEM_W + h * MEM_HEAD_DIM, MEM_W + (h + 1) * MEM_HEAD_DIM)
        s = lax.dot_general(qm_ref[0, :, hs], kv_ref[0, :, hs], (((1,), (1,)), ((), ())),
                            preferred_element_type=F32)
        e = jnp.exp2((s - jnp.max(s, axis=-1, keepdims=True)) * scale)
        pv = jnp.dot(e.astype(BF16), kv_ref[0, :, vs], preferred_element_type=F32)
        heads.append(pv / jnp.sum(e, axis=-1, keepdims=True))
    mem_out = jnp.concatenate(heads, axis=1).astype(BF16)
    y = y + jnp.dot(mem_out, w_ref[mix_w:, :], preferred_element_type=F32)
    z = DEEPNORM_ALPHA * res_ref[0] + y
    o_ref[0] = _layer_norm(z, g_ref[...], b_ref[...])


def _outproj_ln(mix, q_mem, kv_mem, w_out, resid, g, b, tm, name):
    bsz, t, mix_w = mix.shape
    n_mem = kv_mem.shape[1]
    return pl.pallas_call(
        _outproj_ln_body,
        grid=(bsz, t // tm),
        in_specs=[pl.BlockSpec((1, tm, mix_w), lambda b, i: (b, i, 0)),
                  pl.BlockSpec((1, tm, MEM_W), lambda b, i: (b, i, 0)),
                  pl.BlockSpec((1, n_mem, 2 * MEM_W), lambda b, i: (b, 0, 0)),
                  pl.BlockSpec(w_out.shape, lambda b, i: (0, 0)),
                  pl.BlockSpec((1, tm, D_MODEL), lambda b, i: (b, i, 0)),
                  pl.BlockSpec((1, D_MODEL), lambda b, i: (0, 0)),
                  pl.BlockSpec((1, D_MODEL), lambda b, i: (0, 0))],
        out_specs=pl.BlockSpec((1, tm, D_MODEL), lambda b, i: (b, i, 0)),
        out_shape=jax.ShapeDtypeStruct((bsz, t, D_MODEL), F32),
        compiler_params=_params(("parallel", "parallel")),
        name=name,
    )(mix, q_mem, kv_mem, w_out, resid, g.reshape(1, -1), b.reshape(1, -1))


def _ffn_ln_body(h_ref, w13_ref, w2_ref, g_ref, b_ref, o_ref):
    h = h_ref[...]
    hb = h.astype(BF16)
    acc = None
    for c in range(FFN_DIM // FFN_TILE):
        cols = slice(c * FFN_TILE, (c + 1) * FFN_TILE)
        up_cols = slice(FFN_DIM + c * FFN_TILE, FFN_DIM + (c + 1) * FFN_TILE)
        gate = jnp.dot(hb, w13_ref[:, cols], preferred_element_type=F32)
        up = jnp.dot(hb, w13_ref[:, up_cols], preferred_element_type=F32)
        act = (jax.nn.silu(gate) * up).astype(BF16)
        part = jnp.dot(act, w2_ref[cols, :], preferred_element_type=F32)
        acc = part if acc is None else acc + part
    o_ref[...] = _layer_norm(DEEPNORM_ALPHA * h + acc, g_ref[...], b_ref[...])


def _ffn_ln(h, w13, w2, g, b, tm):
    m, d = h.shape
    resident = lambda shape: pl.BlockSpec(shape, lambda i: (0, 0), pipeline_mode=pl.Buffered(1))
    return pl.pallas_call(
        _ffn_ln_body,
        grid=(m // tm,),
        in_specs=[pl.BlockSpec((tm, d), lambda i: (i, 0)),
                  resident(w13.shape), resident(w2.shape),
                  pl.BlockSpec((1, d), lambda i: (0, 0)),
                  pl.BlockSpec((1, d), lambda i: (0, 0))],
        out_specs=pl.BlockSpec((tm, d), lambda i: (i, 0)),
        out_shape=jax.ShapeDtypeStruct((m, d), F32),
        compiler_params=_params(("parallel",)),
        name="ffn_ln",
    )(h, w13, w2, g.reshape(1, -1), b.reshape(1, -1))


ATT_STRIP = 128


def _diffattn_body(pi_ref, pj_ref, lamv_ref, sub_ref, coef_ref, q_ref, kt_ref, v_ref, o_ref,
                   kaug_ref, vaug_ref, qq_ref, mask_ref, s_ref, p_ref, c_ref, m_ref, acc_ref,
                   *, tq, tk, n_pairs, unroll, lam_init):
    t = v_ref.shape[1]
    hd = 2 * DIFF_HEAD_DIM
    n_strips = 2 * tq // ATT_STRIP
    ratio = tq // tk

    feat_row = lax.broadcasted_iota(jnp.int32, (LANES, tk), 0)
    in_block = lax.broadcasted_iota(jnp.int32, (LANES, tk), 1)
    for kb in range(t // tk):
        feat = jnp.where(feat_row < 3, kb, jnp.where(feat_row < 6, in_block, 0))
        kaug_ref[kb, 0:hd, :] = kt_ref[0, :, kb * tk:(kb + 1) * tk]
        kaug_ref[kb, hd:, :] = feat.astype(F32).astype(BF16)
    vaug_ref[:, 0:DIFF_V_DIM] = v_ref[0]
    vaug_ref[:, DIFF_V_DIM:] = jnp.ones((t, LANES), BF16)
    lane = lax.broadcasted_iota(jnp.int32, (tq, hd), 1)
    coef_rows = jnp.broadcast_to(coef_ref[0], (2 * tq, LANES))
    for qb in range(t // tq):
        q = q_ref[0, qb * tq:(qb + 1) * tq, :]
        zero = jnp.zeros_like(q)
        qq_ref[qb * 2 * tq:qb * 2 * tq + tq, 0:hd] = jnp.where(lane < DIFF_HEAD_DIM, q, zero)
        qq_ref[qb * 2 * tq + tq:(qb + 1) * 2 * tq, 0:hd] = jnp.where(lane < DIFF_HEAD_DIM, zero, q)
        qq_ref[qb * 2 * tq:(qb + 1) * 2 * tq, hd:] = coef_rows
    q_row = lax.broadcasted_iota(jnp.int32, (tq, tk), 0)
    k_col = lax.broadcasted_iota(jnp.int32, (tq, tk), 1)
    mask_ref[0] = jnp.zeros((tq, tk), F32)
    for v in range(1, ratio + 1):
        mask_ref[v] = jnp.where(k_col + (v - 1) * tk <= q_row, 0.0, NEG_BIG)
    m_ref[...] = jnp.full(m_ref.shape, NEG_BIG, F32)
    acc_ref[...] = jnp.zeros_like(acc_ref)

    def stage_qk(p, slot):
        q0 = pl.multiple_of(pi_ref[p] * (2 * tq), 2 * tq)
        s_ref[slot] = jnp.dot(qq_ref[pl.ds(q0, 2 * tq), :], kaug_ref[pj_ref[p]],
                              preferred_element_type=F32)

    def stage_sm(p, slot):
        i = pi_ref[p]
        diag = jnp.maximum(pj_ref[p] - ratio * i + 1, 0)
        for r in range(n_strips):
            rows = slice(r * ATT_STRIP, (r + 1) * ATT_STRIP)
            mrows = slice((r * ATT_STRIP) % tq, (r * ATT_STRIP) % tq + ATT_STRIP)
            s = s_ref[slot, rows, :] + mask_ref[diag, mrows, :]
            m_prev = m_ref[i, rows, :]
            m_next = jnp.maximum(m_prev, jnp.max(s, axis=-1, keepdims=True))
            p_blk = jnp.exp2(s - jnp.tile(m_next, (1, tk // LANES)))
            p_ref[slot, rows, :] = p_blk.astype(BF16)
            c_ref[slot, rows, :] = jnp.exp2(m_prev - m_next)
            m_ref[i, rows, :] = m_next

    def stage_pv(p, slot):
        i = pi_ref[p]
        k0 = pl.multiple_of(pj_ref[p] * tk, tk)
        pv = jnp.dot(p_ref[slot], vaug_ref[pl.ds(k0, tk), :], preferred_element_type=F32)
        acc_ref[i] = acc_ref[i] * jnp.tile(c_ref[slot], (1, 2)) + pv

    stage_qk(0, 0)
    stage_qk(1, 1)
    stage_sm(0, 0)

    def pipeline_step(step, c):
        for u in range(unroll):
            p = unroll * step + u
            stage_qk(p + 2, u % 2)
            stage_sm(p + 1, (u + 1) % 2)
            stage_pv(p, u % 2)
        return c

    lax.fori_loop(0, n_pairs // unroll, pipeline_step, 0)

    lv = lamv_ref[...]
    lam = (jnp.exp(jnp.sum(lv[0:1] * lv[1:2], axis=-1, keepdims=True))
           - jnp.exp(jnp.sum(lv[2:3] * lv[3:4], axis=-1, keepdims=True)) + lam_init)

    def finalize(i, c):
        o = acc_ref[i, :, 0:DIFF_V_DIM] / acc_ref[i, :, DIFF_V_DIM:]
        o = o[:tq] - lam * o[tq:]
        o = o * lax.rsqrt(jnp.mean(o * o, axis=-1, keepdims=True) + LN_EPS) * sub_ref[...]
        r0 = pl.multiple_of(i * tq, tq)
        o_ref[0, pl.ds(r0, tq), :] = (o * (1.0 - lam_init)).astype(o_ref.dtype)
        return c

    lax.fori_loop(0, t // tq, finalize, 0, unroll=4)


def _diffattn(q, k_t, v, lam_vecs, subln_g, lam_init, tq, tk):
    bsz, t, _ = q.shape
    slopes = jnp.exp2(-8.0 * (jnp.arange(DIFF_HEADS, dtype=F32) + 1.0) / DIFF_HEADS) * LOG2E
    def pieces(c):
        c1 = c.astype(BF16)
        c2 = (c - c1.astype(F32)).astype(BF16)
        c3 = (c - c1.astype(F32) - c2.astype(F32)).astype(BF16)
        return [c1, c2, c3]

    coef = jnp.zeros((DIFF_HEADS, 1, LANES), BF16)
    coef = coef.at[:, 0, 0:6].set(jnp.stack(pieces(slopes * tk) + pieces(slopes), axis=1))
    hd = 2 * DIFF_HEAD_DIM
    nq = t // tq
    nk = t // tk
    ratio = tq // tk
    assert nk <= 256 and tk <= 256, "key block index and in-block position must be exact in bf16"
    pairs = [(i, j) for i in range(nq) for j in range(ratio * (i + 1))]
    n_pairs = len(pairs)
    unroll = next(u for u in (34, 8, 4, 2) if n_pairs % u == 0)
    assert n_pairs % unroll == 0
    pairs = pairs + [pairs[-1]] * 2
    pair_i = jnp.asarray([p[0] for p in pairs], jnp.int32)
    pair_j = jnp.asarray([p[1] for p in pairs], jnp.int32)
    head = lambda shape: pl.BlockSpec(shape, lambda b, h, pi, pj: (b, 0, h))
    return pl.pallas_call(
        functools.partial(_diffattn_body, tq=tq, tk=tk, n_pairs=n_pairs, unroll=unroll, lam_init=lam_init),
        grid_spec=pltpu.PrefetchScalarGridSpec(
            num_scalar_prefetch=2,
            grid=(bsz, DIFF_HEADS),
            in_specs=[pl.BlockSpec(lam_vecs.shape, lambda b, h, pi, pj: (0, 0)),
                      pl.BlockSpec((1, DIFF_V_DIM), lambda b, h, pi, pj: (0, 0)),
                      pl.BlockSpec((1, 1, LANES), lambda b, h, pi, pj: (h, 0, 0)),
                      head((1, t, hd)),
                      pl.BlockSpec((1, hd, t), lambda b, h, pi, pj: (b, h, 0)),
                      head((1, t, DIFF_V_DIM))],
            out_specs=head((1, t, DIFF_V_DIM)),
            scratch_shapes=[pltpu.VMEM((nk, hd + LANES, tk), BF16),
                            pltpu.VMEM((t, DIFF_V_DIM + LANES), BF16),
                            pltpu.VMEM((2 * t, hd + LANES), BF16),
                            pltpu.VMEM((ratio + 1, tq, tk), F32),
                            pltpu.VMEM((2, 2 * tq, tk), F32),
                            pltpu.VMEM((2, 2 * tq, tk), BF16),
                            pltpu.VMEM((2, 2 * tq, LANES), F32),
                            pltpu.VMEM((nq, 2 * tq, LANES), F32),
                            pltpu.VMEM((nq, 2 * tq, DIFF_V_DIM + LANES), F32)]),
        out_shape=jax.ShapeDtypeStruct((bsz, t, DIFF_V_W), BF16),
        compiler_params=_params(("parallel", "parallel")),
        name="diffattn",
    )(pair_i, pair_j, lam_vecs, subln_g.reshape(1, -1), coef, q, k_t, v)


def _router_body(h_ref, w_ref, o_ref, cnt_ref, carry_ref, *, tm):
    @pl.when(pl.program_id(0) == 0)
    def _():
        carry_ref[...] = jnp.zeros_like(carry_ref)

    h = h_ref[...]
    w = w_ref[...]
    h_hi = h.astype(BF16)
    h_lo = (h - h_hi.astype(F32)).astype(BF16)
    w_hi = w.astype(BF16)
    w_lo = (w - w_hi.astype(F32)).astype(BF16)
    logits = (jnp.dot(h_hi, w_hi, preferred_element_type=F32)
              + (jnp.dot(h_hi, w_lo, preferred_element_type=F32)
                 + jnp.dot(h_lo, w_hi, preferred_element_type=F32)))
    lane = lax.broadcasted_iota(jnp.int32, (tm, LANES), 1)
    lg = jnp.where(lane < N_EXPERTS, logits, -jnp.inf)
    v1 = jnp.max(lg, axis=-1, keepdims=True)
    i1 = jnp.min(jnp.where(lg == v1, lane, LANES), axis=-1, keepdims=True)
    oh1 = lane == i1
    lg2 = jnp.where(oh1, -jnp.inf, lg)
    v2 = jnp.max(lg2, axis=-1, keepdims=True)
    i2 = jnp.min(jnp.where(lg2 == v2, lane, LANES), axis=-1, keepdims=True)
    oh2 = lane == i2
    e2 = jnp.exp(v2 - v1)
    g1 = 1.0 / (1.0 + e2)
    g2 = e2 / (1.0 + e2)

    both = jnp.where(oh1, 1.0, jnp.where(oh2, 1.0, 0.0))
    tri = (lax.broadcasted_iota(jnp.int32, (tm, tm), 0)
           > lax.broadcasted_iota(jnp.int32, (tm, tm), 1))
    tri = jnp.where(tri, 1.0, 0.0).astype(BF16)
    before = jnp.dot(tri, both.astype(BF16), preferred_element_type=F32) + carry_ref[0:1, :]
    rank1 = jnp.sum(jnp.where(oh1, before, 0.0), axis=-1, keepdims=True)
    rank2 = jnp.sum(jnp.where(oh2, before, 0.0), axis=-1, keepdims=True)
    total = carry_ref[...] + jnp.sum(both, axis=0, keepdims=True)
    carry_ref[...] = total
    cnt_ref[...] = total

    out = jnp.where(lane == 0, i1.astype(F32), 0.0)
    out = jnp.where(lane == 1, i2.astype(F32), out)
    out = jnp.where(lane == 2, rank1, out)
    out = jnp.where(lane == 3, rank2, out)
    out = jnp.where(lane == 4, g1, out)
    out = jnp.where(lane == 5, g2, out)
    o_ref[...] = out


def _router(h, w_router, tm):
    m, d = h.shape
    w = jnp.zeros((d, LANES), F32).at[:, :N_EXPERTS].set(w_router)
    return pl.pallas_call(
        functools.partial(_router_body, tm=tm),
        grid=(m // tm,),
        in_specs=[pl.BlockSpec((tm, d), lambda i: (i, 0)),
                  pl.BlockSpec((d, LANES), lambda i: (0, 0))],
        out_specs=[pl.BlockSpec((tm, LANES), lambda i: (i, 0)),
                   pl.BlockSpec((SUBLANES, LANES), lambda i: (0, 0))],
        out_shape=[jax.ShapeDtypeStruct((m, LANES), F32),
                   jax.ShapeDtypeStruct((SUBLANES, LANES), F32)],
        scratch_shapes=[pltpu.VMEM((SUBLANES, LANES), F32)],
        compiler_params=_params(("arbitrary",)),
        name="router",
    )(h, w)


ROW_TILE = D_MODEL // LANES


def _row_tile(ref, r):
    start = r * ROW_TILE if isinstance(r, int) else pl.multiple_of(r * ROW_TILE, ROW_TILE)
    return ref.at[pl.ds(start, ROW_TILE), :]


def _store_row_tiles(tile_ref, x):
    for j in range(ROW_TILE):
        tile_ref[pl.ds(j, x.shape[0], stride=ROW_TILE), :] = x[:, j * LANES:(j + 1) * LANES]


def _load_row_tiles(tile_ref, n):
    return jnp.concatenate([tile_ref[pl.ds(j, n, stride=ROW_TILE), :] for j in range(ROW_TILE)], axis=1)


def _dispatch_body(dest_ref, meta_ref, h_ref, buf_ref, rows_ref, zero_ref, sem, *, tm, n_tok, n_rows):
    i = pl.program_id(0)
    base = i * tm
    _store_row_tiles(rows_ref, h_ref[...])

    def row_copy(r, d):
        return pltpu.make_async_copy(_row_tile(rows_ref, r), _row_tile(buf_ref, d), sem)

    def issue(r, c):
        row_copy(r, dest_ref[base + r]).start(priority=0)
        row_copy(r, dest_ref[n_tok + base + r]).start(priority=1)
        return c

    lax.fori_loop(0, tm, issue, 0, unroll=16)
    all_rows = buf_ref.at[pl.ds(0, TOP_K * tm * ROW_TILE), :]
    pltpu.make_async_copy(all_rows, all_rows, sem).wait()

    @pl.when(i == pl.num_programs(0) - 1)
    def _():
        zero_ref[...] = jnp.zeros_like(zero_ref)

        def zero_row(r):
            return pltpu.make_async_copy(_row_tile(zero_ref, 0), _row_tile(buf_ref, r), sem)

        def start_zero_row(r, c):
            zero_row(r).start()
            return c

        def wait_zero_row(r, c):
            zero_row(0).wait()
            return c

        for e in range(N_EXPERTS):
            lo = meta_ref[e] + meta_ref[N_EXPERTS + e]
            hi = meta_ref[e] + meta_ref[2 * N_EXPERTS + e]
            lax.fori_loop(lo, hi, start_zero_row, 0)
            lax.fori_loop(lo, hi, wait_zero_row, 0)

        def zero_block(b, c):
            r0 = pl.multiple_of(b * (MOE_ROWS * ROW_TILE), MOE_ROWS * ROW_TILE)
            cp = pltpu.make_async_copy(zero_ref, buf_ref.at[pl.ds(r0, MOE_ROWS * ROW_TILE), :], sem)
            cp.start()
            cp.wait()
            return c

        used = meta_ref[N_EXPERTS - 1] + meta_ref[3 * N_EXPERTS - 1]
        lax.fori_loop(used // MOE_ROWS, n_rows // MOE_ROWS, zero_block, 0)


def _dispatch(dest, meta, h, n_rows, tm):
    m, d = h.shape
    return pl.pallas_call(
        functools.partial(_dispatch_body, tm=tm, n_tok=m, n_rows=n_rows),
        grid_spec=pltpu.PrefetchScalarGridSpec(
            num_scalar_prefetch=2,
            grid=(m // tm,),
            in_specs=[pl.BlockSpec((tm, d), lambda i, dest, meta: (i, 0))],
            out_specs=pl.BlockSpec(memory_space=pl.ANY),
            scratch_shapes=[pltpu.VMEM((tm * ROW_TILE, LANES), F32),
                            pltpu.VMEM((MOE_ROWS * ROW_TILE, LANES), F32),
                            pltpu.SemaphoreType.DMA(())]),
        out_shape=jax.ShapeDtypeStruct((n_rows * ROW_TILE, LANES), F32),
        compiler_params=_params(("arbitrary",)),
        name="moe_dispatch",
    )(dest, meta, h)


def _moe_body(be_ref, nv_ref, x_ref, w13_ref, w2_ref, o_ref):
    b = pl.program_id(0)

    @pl.when(b < nv_ref[0])
    def _():
        xb = _load_row_tiles(x_ref, MOE_ROWS).astype(BF16)
        acc = None
        for c in range(FFN_DIM // FFN_TILE):
            cols = slice(c * FFN_TILE, (c + 1) * FFN_TILE)
            up_cols = slice(FFN_DIM + c * FFN_TILE, FFN_DIM + (c + 1) * FFN_TILE)
            gate = jnp.dot(xb, w13_ref[0, :, cols], preferred_element_type=F32)
            up = jnp.dot(xb, w13_ref[0, :, up_cols], preferred_element_type=F32)
            act = (jax.nn.silu(gate) * up).astype(BF16)
            part = jnp.dot(act, w2_ref[0, cols, :], preferred_element_type=F32)
            acc = part if acc is None else acc + part
        _store_row_tiles(o_ref, acc)

    @pl.when(b >= nv_ref[0])
    def _():
        o_ref[...] = jnp.zeros_like(o_ref)


def _moe(block_e, n_valid, buf, w13, w2):
    d = D_MODEL
    n_rows = buf.shape[0] // ROW_TILE
    nb = n_rows // MOE_ROWS
    row_block = (MOE_ROWS * ROW_TILE, LANES)

    def blk(b, nv):
        return jnp.minimum(b, nv[0] - 1)

    def expert(shape):
        return pl.BlockSpec(shape, lambda b, be, nv: (be[blk(b, nv)], 0, 0), pipeline_mode=pl.Buffered(1))

    return pl.pallas_call(
        _moe_body,
        grid_spec=pltpu.PrefetchScalarGridSpec(
            num_scalar_prefetch=2,
            grid=(nb,),
            in_specs=[pl.BlockSpec(row_block, lambda b, be, nv: (blk(b, nv), 0)),
                      expert((1, d, 2 * FFN_DIM)), expert((1, FFN_DIM, d))],
            out_specs=pl.BlockSpec(row_block, lambda b, be, nv: (b, 0))),
        out_shape=jax.ShapeDtypeStruct((n_rows * ROW_TILE, LANES), F32),
        compiler_params=_params(("arbitrary",)),
        name="moe_experts",
    )(block_e, n_valid, buf, w13, w2)


def _combine_ln_body(dest_ref, h_ref, route_ref, g_ref, b_ref, yb_ref, o_ref, rows_ref, sem,
                     *, tm, n_tok):
    i = pl.program_id(0)

    def gather_tile(tile, buf):
        base = tile * tm

        def issue(r, c):
            for k in range(TOP_K):
                pltpu.make_async_copy(_row_tile(yb_ref, dest_ref[k * n_tok + base + r]),
                                      _row_tile(rows_ref.at[buf, k], r), sem.at[buf]).start(priority=k)
            return c

        lax.fori_loop(0, tm, issue, 0, unroll=16)

    @pl.when(i == 0)
    def _():
        gather_tile(0, 0)

    @pl.when(i + 1 < pl.num_programs(0))
    def _():
        gather_tile(i + 1, (i + 1) % 2)

    buf = i % 2
    pltpu.make_async_copy(rows_ref.at[buf], rows_ref.at[buf], sem.at[buf]).wait()
    route = route_ref[...]
    y = (route[:, 4:5] * _load_row_tiles(rows_ref.at[buf, 0], tm)
         + route[:, 5:6] * _load_row_tiles(rows_ref.at[buf, 1], tm))
    z = DEEPNORM_ALPHA * h_ref[...] + y
    o_ref[...] = _layer_norm(z, g_ref[...], b_ref[...])


def _combine_ln(dest, h, route, yb, g, b, tm):
    m, d = h.shape
    return pl.pallas_call(
        functools.partial(_combine_ln_body, tm=tm, n_tok=m),
        grid_spec=pltpu.PrefetchScalarGridSpec(
            num_scalar_prefetch=1,
            grid=(m // tm,),
            in_specs=[pl.BlockSpec((tm, d), lambda i, dest: (i, 0)),
                      pl.BlockSpec((tm, LANES), lambda i, dest: (i, 0)),
                      pl.BlockSpec((1, d), lambda i, dest: (0, 0)),
                      pl.BlockSpec((1, d), lambda i, dest: (0, 0)),
                      pl.BlockSpec(memory_space=pl.ANY)],
            out_specs=pl.BlockSpec((tm, d), lambda i, dest: (i, 0)),
            scratch_shapes=[pltpu.VMEM((2, TOP_K, tm * ROW_TILE, LANES), F32),
                            pltpu.SemaphoreType.DMA((2,))]),
        out_shape=jax.ShapeDtypeStruct((m, d), F32),
        compiler_params=_params(("arbitrary",)),
        name="moe_combine_ln",
    )(dest, h, route, g.reshape(1, -1), b.reshape(1, -1), yb)


def _moe_layer(h, w_router, w13, w2, g, b):
    n_tok, _ = h.shape
    route, cnt = _router(h, w_router, tm=512)
    counts = cnt[0, :N_EXPERTS].astype(jnp.int32)
    padded = (counts + MOE_ROWS - 1) // MOE_ROWS * MOE_ROWS
    pstart = jnp.cumsum(padded) - padded
    experts = route[:, 0:2].astype(jnp.int32)
    ranks = route[:, 2:4].astype(jnp.int32)
    dest = (pstart[experts] + ranks).T.reshape(-1)
    n_rows = n_tok * TOP_K + N_EXPERTS * MOE_ROWS
    n_blocks = n_rows // MOE_ROWS
    block_end = jnp.cumsum(padded // MOE_ROWS)
    block_e = jnp.sum(jnp.arange(n_blocks)[:, None] >= block_end[None, :], axis=1)
    block_e = jnp.minimum(block_e, N_EXPERTS - 1).astype(jnp.int32)
    n_valid = block_end[-1:].astype(jnp.int32)
    meta = jnp.concatenate([pstart, counts, padded]).astype(jnp.int32)
    buf = _dispatch(dest, meta, h, n_rows, tm=512)
    yb = _moe(block_e, n_valid, buf, w13, w2)
    return _combine_ln(dest, h, route, yb, g, b, tm=512)


def kernel(x, mem, a_w_in, a_conv_w, a_conv_b, a_w_rgate, a_b_rgate, a_w_igate, a_b_igate, a_lambda,
           a_w_out, w_kv_shared, b_w_q, b_lambda, b_subln_g, b_w_out, mem_w_kv, ffn_w13, ffn_w2,
           moe_router, moe_w13, moe_w2, ln_g, ln_b):
    bsz, t, d = x.shape
    n_tok = bsz * t
    n_mem = mem.shape[1]
    bf = lambda a: a.astype(BF16)

    kv_mem = _proj(mem.reshape(bsz * n_mem, d), bf(jnp.concatenate([mem_w_kv[0], mem_w_kv[1]], axis=1)),
                   [(0, 2 * MEM_W), (2 * MEM_W, 4 * MEM_W)], [None, None], [BF16, BF16],
                   tm=min(1024, bsz * n_mem), name="mem_kv")
    kv_mem = [a.reshape(bsz, n_mem, 2 * MEM_W) for a in kv_mem]

    x2 = x.reshape(n_tok, d)
    gate, u_pre, q_mem = _proj(
        x2, bf(a_w_in[0]), [(0, D_RNN), (D_RNN, 2 * D_RNN), (2 * D_RNN, 2 * D_RNN + MEM_W)],
        [jax.nn.gelu, None, None], [BF16, BF16, BF16], tm=1024, name="proj_in")
    w_ri = bf(jnp.concatenate([a_w_rgate[0], a_w_igate[0]], axis=-1))
    rnn = _rglru(u_pre.reshape(bsz, t, D_RNN), gate.reshape(bsz, t, D_RNN), a_conv_w[0], a_conv_b[0],
                 w_ri, a_b_rgate[0], a_b_igate[0], a_lambda[0], tt=512)
    h = _outproj_ln(rnn, q_mem.reshape(bsz, t, MEM_W), kv_mem[0], bf(a_w_out[0]), x,
                    ln_g[0, 0], ln_b[0, 0], tm=1024, name="outproj_ln_a")
    h = _ffn_ln(h.reshape(n_tok, d), bf(ffn_w13[0]), bf(ffn_w2[0]), ln_g[0, 1], ln_b[0, 1], tm=512)

    layer = 1
    lam_init = 0.8 - 0.6 * math.exp(-0.3 * layer)
    w_cat = bf(jnp.concatenate([w_kv_shared[:, DIFF_QK_W:], b_w_q[0]], axis=1))
    w_k_t = bf(w_kv_shared[:, :DIFF_QK_W].T)
    v_sh, k_t, q_diff, q_mem = _proj_kvq(h, w_cat, w_k_t, bsz, t, tm=1024)
    attn = _diffattn(q_diff.reshape(bsz, t, DIFF_QK_W), k_t, v_sh.reshape(bsz, t, DIFF_V_W),
                     b_lambda[0], b_subln_g[0], lam_init, tq=256, tk=256)
    h = _outproj_ln(attn, q_mem.reshape(bsz, t, MEM_W), kv_mem[1], bf(b_w_out[0]),
                    h.reshape(bsz, t, d), ln_g[1, 0], ln_b[1, 0], tm=1024, name="outproj_ln_b")
    out = _moe_layer(h.reshape(n_tok, d), moe_router[0], bf(moe_w13[0]), bf(moe_w2[0]),
                     ln_g[1, 1], ln_b[1, 1])
    return out.reshape(bsz, t, d)
```

```python
import functools
import math

import jax
import jax.numpy as jnp
from jax import lax
from jax.experimental import pallas as pl
from jax.experimental.pallas import tpu as pltpu

F32 = jnp.float32
BF16 = jnp.bfloat16

D_MODEL = 1024
DEPTH = 2
D_RNN = D_MODEL
RNN_BLOCKS = 8
RNN_BLOCK_W = D_RNN // RNN_BLOCKS
CONV_W = 4
LRU_C = 8.0
MEM_HEADS = 4
MEM_HEAD_DIM = D_MODEL // 8
MEM_W = MEM_HEADS * MEM_HEAD_DIM
DIFF_HEADS = 8
DIFF_HEAD_DIM = D_MODEL // 16
DIFF_V_DIM = 2 * DIFF_HEAD_DIM
DIFF_QK_W = DIFF_HEADS * 2 * DIFF_HEAD_DIM
DIFF_V_W = DIFF_HEADS * DIFF_V_DIM
FFN_DIM = (7 * D_MODEL) // 2
N_EXPERTS = 8
TOP_K = 2
LN_EPS = 1e-5
DEEPNORM_ALPHA = (2.0 * DEPTH) ** 0.25

LANES = 128
SUBLANES = 8
VMEM_LIMIT = 52 * 1024 * 1024

FFN_TILE = 1792
MOE_ROWS = 512
NEG_BIG = -1e30
LOG2E = math.log2(math.e)


def _params(semantics):
    return pltpu.CompilerParams(dimension_semantics=semantics, vmem_limit_bytes=VMEM_LIMIT)


def _layer_norm(z, g, b):
    mu = jnp.mean(z, axis=-1, keepdims=True)
    zc = z - mu
    var = jnp.mean(zc * zc, axis=-1, keepdims=True)
    return zc * lax.rsqrt(var + LN_EPS) * g + b


def _proj_body(x_ref, w_ref, *o_refs, splits, post):
    xb = x_ref[...].astype(BF16)
    for o_ref, (c0, c1), fn in zip(o_refs, splits, post):
        z = jnp.dot(xb, w_ref[:, c0:c1], preferred_element_type=F32)
        if fn is not None:
            z = fn(z)
        o_ref[...] = z.astype(o_ref.dtype)


def _proj(x, w, splits, post, out_dtypes, tm, name):
    m, k = x.shape
    return pl.pallas_call(
        functools.partial(_proj_body, splits=tuple(splits), post=tuple(post)),
        grid=(m // tm,),
        in_specs=[pl.BlockSpec((tm, k), lambda i: (i, 0)),
                  pl.BlockSpec(w.shape, lambda i: (0, 0))],
        out_specs=[pl.BlockSpec((tm, c1 - c0), lambda i: (i, 0)) for c0, c1 in splits],
        out_shape=[jax.ShapeDtypeStruct((m, c1 - c0), dt) for (c0, c1), dt in zip(splits, out_dtypes)],
        compiler_params=_params(("parallel",)),
        name=name,
    )(x, w)


def _proj_kvq_body(x_ref, w_ref, wkt_ref, v_ref, kt_ref, qd_ref, qm_ref):
    xb = x_ref[...].astype(BF16)
    q_scale = DIFF_HEAD_DIM ** -0.5 * LOG2E
    v_ref[...] = jnp.dot(xb, w_ref[:, 0:DIFF_V_W], preferred_element_type=F32).astype(BF16)
    kt_ref[0] = lax.dot_general(wkt_ref[...], xb, (((1,), (1,)), ((), ())),
                                preferred_element_type=F32).astype(BF16)
    qd = jnp.dot(xb, w_ref[:, DIFF_V_W:DIFF_V_W + DIFF_QK_W], preferred_element_type=F32)
    qd_ref[...] = (qd * q_scale).astype(BF16)
    qm_ref[...] = jnp.dot(xb, w_ref[:, DIFF_V_W + DIFF_QK_W:], preferred_element_type=F32).astype(BF16)


def _proj_kvq(h, w_cat, w_k_t, bsz, t, tm):
    m, d = h.shape
    nt = t // tm
    rows = lambda width: pl.BlockSpec((tm, width), lambda i: (i, 0))
    return pl.pallas_call(
        _proj_kvq_body,
        grid=(m // tm,),
        in_specs=[rows(d), pl.BlockSpec(w_cat.shape, lambda i: (0, 0)),
                  pl.BlockSpec(w_k_t.shape, lambda i: (0, 0))],
        out_specs=[rows(DIFF_V_W), pl.BlockSpec((1, DIFF_QK_W, tm), lambda i: (i // nt, 0, i % nt)),
                   rows(DIFF_QK_W), rows(MEM_W)],
        out_shape=[jax.ShapeDtypeStruct((m, DIFF_V_W), BF16),
                   jax.ShapeDtypeStruct((bsz, DIFF_QK_W, t), BF16),
                   jax.ShapeDtypeStruct((m, DIFF_QK_W), BF16),
                   jax.ShapeDtypeStruct((m, MEM_W), BF16)],
        compiler_params=_params(("parallel",)),
        name="proj_kvq",
    )(h, w_cat, w_k_t)


def _rglru_body(u_ref, g_ref, cw_ref, cb_ref, wri_ref, br_ref, bi_ref, lam_ref, o_ref,
                ext_ref, rec_ref, a_ref, b_ref, h_ref, *, tt):
    t = pl.program_id(1)

    @pl.when(t == 0)
    def _():
        ext_ref[0:SUBLANES, :] = jnp.zeros((SUBLANES, D_RNN), F32)
        h_ref[...] = jnp.zeros((SUBLANES, D_RNN), F32)

    ext_ref[SUBLANES:, :] = u_ref[0].astype(F32)
    rec = cb_ref[...] + cw_ref[CONV_W - 1:CONV_W, :] * ext_ref[SUBLANES:SUBLANES + tt, :]
    for j in range(CONV_W - 1):
        off = SUBLANES - (CONV_W - 1) + j
        rec = rec + cw_ref[j:j + 1, :] * ext_ref[off:off + tt, :]
    rec_ref[...] = rec
    ext_ref[0:SUBLANES, :] = ext_ref[tt:tt + SUBLANES, :]

    sp = jax.nn.softplus(-lam_ref[...])
    first_row = (lax.broadcasted_iota(jnp.int32, (SUBLANES, RNN_BLOCK_W), 0) + t * tt) == 0
    for n in range(RNN_BLOCKS):
        blk = slice(n * RNN_BLOCK_W, (n + 1) * RNN_BLOCK_W)
        u_n = rec_ref[:, blk]
        ri = jnp.dot(u_n.astype(BF16), wri_ref[n], preferred_element_type=F32)
        r_t = jnp.tanh(0.5 * (ri[:, :RNN_BLOCK_W] + br_ref[:, blk]))
        i = 0.5 * jnp.tanh(0.5 * (ri[:, RNN_BLOCK_W:] + bi_ref[:, blk])) + 0.5
        half = (-0.5 * LRU_C) * sp[:, blk]
        log_a = half * r_t + half
        a = jnp.exp(log_a)
        th = jnp.tanh(-log_a)
        m2 = 2.0 * th / (1.0 + th)
        mult = jnp.where(m2 > 0.0, m2 * lax.rsqrt(m2), 0.0)
        mult = jnp.concatenate([jnp.where(first_row, 1.0, mult[:SUBLANES]), mult[SUBLANES:]], axis=0)
        a_ref[:, blk] = a
        b_ref[:, blk] = mult * (i * u_n)

    row = lax.broadcasted_iota(jnp.int32, (SUBLANES, D_RNN), 0)

    def tile(i, h_prev):
        r0 = pl.multiple_of(i * SUBLANES, SUBLANES)
        a = a_ref[pl.ds(r0, SUBLANES), :]
        b = b_ref[pl.ds(r0, SUBLANES), :]
        for s in (1, 2, 4):
            a_s = pltpu.roll(a, s, 0)
            b_s = pltpu.roll(b, s, 0)
            keep = row >= s
            b = jnp.where(keep, a * b_s + b, b)
            a = jnp.where(keep, a * a_s, a)
        h = b + a * h_prev
        b_ref[pl.ds(r0, SUBLANES), :] = h
        return jnp.broadcast_to(h[SUBLANES - 1:SUBLANES, :], (SUBLANES, D_RNN))

    h_ref[...] = lax.fori_loop(0, tt // SUBLANES, tile, h_ref[...], unroll=4)
    o_ref[0] = (b_ref[...] * g_ref[0].astype(F32)).astype(o_ref.dtype)


def _rglru(u_pre, gate, conv_w, conv_b, w_ri, b_r, b_i, lam, tt):
    bsz, t, c = u_pre.shape
    row = lambda a: a.reshape(1, c)
    full = lambda shape: pl.BlockSpec(shape, lambda b, i: (0,) * len(shape))
    return pl.pallas_call(
        functools.partial(_rglru_body, tt=tt),
        grid=(bsz, t // tt),
        in_specs=[pl.BlockSpec((1, tt, c), lambda b, i: (b, i, 0)),
                  pl.BlockSpec((1, tt, c), lambda b, i: (b, i, 0)),
                  full((CONV_W, c)), full((1, c)), full(w_ri.shape),
                  full((1, c)), full((1, c)), full((1, c))],
        out_specs=pl.BlockSpec((1, tt, c), lambda b, i: (b, i, 0)),
        out_shape=jax.ShapeDtypeStruct((bsz, t, c), BF16),
        scratch_shapes=[pltpu.VMEM((tt + SUBLANES, c), F32), pltpu.VMEM((tt, c), F32),
                        pltpu.VMEM((tt, c), F32), pltpu.VMEM((tt, c), F32),
                        pltpu.VMEM((SUBLANES, c), F32)],
        compiler_params=_params(("parallel", "arbitrary")),
        name="rglru",
    )(u_pre, gate, conv_w, row(conv_b), w_ri, row(b_r), row(b_i), row(lam))


def _outproj_ln_body(a_ref, qm_ref, kv_ref, w_ref, res_ref, g_ref, b_ref, o_ref):
    mix_w = a_ref.shape[-1]
    y = jnp.dot(a_ref[0], w_ref[0:mix_w, :], preferred_element_type=F32)
    scale = MEM_HEAD_DIM ** -0.5 * LOG2E
    heads = []
    for h in range(MEM_HEADS):
        hs = slice(h * MEM_HEAD_DIM, (h + 1) * MEM_HEAD_DIM)
        vs = slice(MEM_W + h * MEM_HEAD_DIM, MEM_W + (h + 1) * MEM_HEAD_DIM)
        s = lax.dot_general(qm_ref[0, :, hs], kv_ref[0, :, hs], (((1,), (1,)), ((), ())),
                            preferred_element_type=F32)
        e = jnp.exp2((s - jnp.max(s, axis=-1, keepdims=True)) * scale)
        pv = jnp.dot(e.astype(BF16), kv_ref[0, :, vs], preferred_element_type=F32)
        heads.append(pv / jnp.sum(e, axis=-1, keepdims=True))
    mem_out = jnp.concatenate(heads, axis=1).astype(BF16)
    y = y + jnp.dot(mem_out, w_ref[mix_w:, :], preferred_element_type=F32)
    z = DEEPNORM_ALPHA * res_ref[0] + y
    o_ref[0] = _layer_norm(z, g_ref[...], b_ref[...])


def _outproj_ln(mix, q_mem, kv_mem, w_out, resid, g, b, tm, name):
    bsz, t, mix_w = mix.shape
    n_mem = kv_mem.shape[1]
    return pl.pallas_call(
        _outproj_ln_body,
        grid=(bsz, t // tm),
        in_specs=[pl.BlockSpec((1, tm, mix_w), lambda b, i: (b, i, 0)),
                  pl.BlockSpec((1, tm, MEM_W), lambda b, i: (b, i, 0)),
                  pl.BlockSpec((1, n_mem, 2 * MEM_W), lambda b, i: (b, 0, 0)),
                  pl.BlockSpec(w_out.shape, lambda b, i: (0, 0)),
                  pl.BlockSpec((1, tm, D_MODEL), lambda b, i: (b, i, 0)),
                  pl.BlockSpec((1, D_MODEL), lambda b, i: (0, 0)),
                  pl.BlockSpec((1, D_MODEL), lambda b, i: (0, 0))],
        out_specs=pl.BlockSpec((1, tm, D_MODEL), lambda b, i: (b, i, 0)),
        out_shape=jax.ShapeDtypeStruct((bsz, t, D_MODEL), F32),
        compiler_params=_params(("parallel", "parallel")),
        name=name,
    )(mix, q_mem, kv_mem, w_out, resid, g.reshape(1, -1), b.reshape(1, -1))


def _ffn_ln_body(h_ref, w13_ref, w2_ref, g_ref, b_ref, o_ref):
    h = h_ref[...]
    hb = h.astype(BF16)
    acc = None
    for c in range(FFN_DIM // FFN_TILE):
        cols = slice(c * FFN_TILE, (c + 1) * FFN_TILE)
        up_cols = slice(FFN_DIM + c * FFN_TILE, FFN_DIM + (c + 1) * FFN_TILE)
        gate = jnp.dot(hb, w13_ref[:, cols], preferred_element_type=F32)
        up = jnp.dot(hb, w13_ref[:, up_cols], preferred_element_type=F32)
        act = (jax.nn.silu(gate) * up).astype(BF16)
        part = jnp.dot(act, w2_ref[cols, :], preferred_element_type=F32)
        acc = part if acc is None else acc + part
    o_ref[...] = _layer_norm(DEEPNORM_ALPHA * h + acc, g_ref[...], b_ref[...])


def _ffn_ln(h, w13, w2, g, b, tm):
    m, d = h.shape
    resident = lambda shape: pl.BlockSpec(shape, lambda i: (0, 0), pipeline_mode=pl.Buffered(1))
    return pl.pallas_call(
        _ffn_ln_body,
        grid=(m // tm,),
        in_specs=[pl.BlockSpec((tm, d), lambda i: (i, 0)),
                  resident(w13.shape), resident(w2.shape),
                  pl.BlockSpec((1, d), lambda i: (0, 0)),
                  pl.BlockSpec((1, d), lambda i: (0, 0))],
        out_specs=pl.BlockSpec((tm, d), lambda i: (i, 0)),
        out_shape=jax.ShapeDtypeStruct((m, d), F32),
        compiler_params=_params(("parallel",)),
        name="ffn_ln",
    )(h, w13, w2, g.reshape(1, -1), b.reshape(1, -1))


ATT_STRIP = 128


def _diffattn_body(pi_ref, pj_ref, lamv_ref, sub_ref, coef_ref, q_ref, kt_ref, v_ref, o_ref,
                   kaug_ref, vaug_ref, qq_ref, mask_ref, s_ref, p_ref, c_ref, m_ref, acc_ref,
                   *, tq, n_pairs, unroll, lam_init):
    t = v_ref.shape[1]
    hd = 2 * DIFF_HEAD_DIM
    n_strips = 2 * tq // ATT_STRIP

    feat_row = lax.broadcasted_iota(jnp.int32, (LANES, tq), 0)
    in_block = lax.broadcasted_iota(jnp.int32, (LANES, tq), 1)
    for kb in range(t // tq):
        feat = jnp.where(feat_row < 3, kb, jnp.where(feat_row < 6, in_block, 0))
        kaug_ref[kb, 0:hd, :] = kt_ref[0, :, kb * tq:(kb + 1) * tq]
        kaug_ref[kb, hd:, :] = feat.astype(F32).astype(BF16)
    vaug_ref[:, 0:DIFF_V_DIM] = v_ref[0]
    vaug_ref[:, DIFF_V_DIM:] = jnp.ones((t, LANES), BF16)
    lane = lax.broadcasted_iota(jnp.int32, (tq, hd), 1)
    coef_rows = jnp.broadcast_to(coef_ref[0], (2 * tq, LANES))
    for qb in range(t // tq):
        q = q_ref[0, qb * tq:(qb + 1) * tq, :]
        zero = jnp.zeros_like(q)
        qq_ref[qb * 2 * tq:qb * 2 * tq + tq, 0:hd] = jnp.where(lane < DIFF_HEAD_DIM, q, zero)
        qq_ref[qb * 2 * tq + tq:(qb + 1) * 2 * tq, 0:hd] = jnp.where(lane < DIFF_HEAD_DIM, zero, q)
        qq_ref[qb * 2 * tq:(qb + 1) * 2 * tq, hd:] = coef_rows
    q_row = lax.broadcasted_iota(jnp.int32, (tq, tq), 0)
    k_col = lax.broadcasted_iota(jnp.int32, (tq, tq), 1)
    mask_ref[0] = jnp.zeros((tq, tq), F32)
    mask_ref[1] = jnp.where(k_col <= q_row, 0.0, NEG_BIG)
    m_ref[...] = jnp.full(m_ref.shape, NEG_BIG, F32)
    acc_ref[...] = jnp.zeros_like(acc_ref)

    def stage_qk(p, slot):
        q0 = pl.multiple_of(pi_ref[p] * (2 * tq), 2 * tq)
        s_ref[slot] = jnp.dot(qq_ref[pl.ds(q0, 2 * tq), :], kaug_ref[pj_ref[p]],
                              preferred_element_type=F32)

    def stage_sm(p, slot):
        i = pi_ref[p]
        diag = (i == pj_ref[p]).astype(jnp.int32)
        for r in range(n_strips):
            rows = slice(r * ATT_STRIP, (r + 1) * ATT_STRIP)
            mrows = slice((r * ATT_STRIP) % tq, (r * ATT_STRIP) % tq + ATT_STRIP)
            s = s_ref[slot, rows, :] + mask_ref[diag, mrows, :]
            m_prev = m_ref[i, rows, :]
            m_next = jnp.maximum(m_prev, jnp.max(s, axis=-1, keepdims=True))
            p_blk = jnp.exp2(s - jnp.tile(m_next, (1, tq // LANES)))
            p_ref[slot, rows, :] = p_blk.astype(BF16)
            c_ref[slot, rows, :] = jnp.exp2(m_prev - m_next)
            m_ref[i, rows, :] = m_next

    def stage_pv(p, slot):
        i = pi_ref[p]
        k0 = pl.multiple_of(pj_ref[p] * tq, tq)
        pv = jnp.dot(p_ref[slot], vaug_ref[pl.ds(k0, tq), :], preferred_element_type=F32)
        acc_ref[i] = acc_ref[i] * jnp.tile(c_ref[slot], (1, 2)) + pv

    stage_qk(0, 0)
    stage_qk(1, 1)
    stage_sm(0, 0)

    def pipeline_step(step, c):
        for u in range(unroll):
            p = unroll * step + u
            stage_qk(p + 2, u % 2)
            stage_sm(p + 1, (u + 1) % 2)
            stage_pv(p, u % 2)
        return c

    lax.fori_loop(0, n_pairs // unroll, pipeline_step, 0)

    lv = lamv_ref[...]
    lam = (jnp.exp(jnp.sum(lv[0:1] * lv[1:2], axis=-1, keepdims=True))
           - jnp.exp(jnp.sum(lv[2:3] * lv[3:4], axis=-1, keepdims=True)) + lam_init)

    def finalize(i, c):
        o = acc_ref[i, :, 0:DIFF_V_DIM] / acc_ref[i, :, DIFF_V_DIM:]
        o = o[:tq] - lam * o[tq:]
        o = o * lax.rsqrt(jnp.mean(o * o, axis=-1, keepdims=True) + LN_EPS) * sub_ref[...]
        r0 = pl.multiple_of(i * tq, tq)
        o_ref[0, pl.ds(r0, tq), :] = (o * (1.0 - lam_init)).astype(o_ref.dtype)
        return c

    lax.fori_loop(0, t // tq, finalize, 0, unroll=4)


def _diffattn(q, k_t, v, lam_vecs, subln_g, lam_init, tq):
    bsz, t, _ = q.shape
    slopes = jnp.exp2(-8.0 * (jnp.arange(DIFF_HEADS, dtype=F32) + 1.0) / DIFF_HEADS) * LOG2E
    def pieces(c):
        c1 = c.astype(BF16)
        c2 = (c - c1.astype(F32)).astype(BF16)
        c3 = (c - c1.astype(F32) - c2.astype(F32)).astype(BF16)
        return [c1, c2, c3]

    coef = jnp.zeros((DIFF_HEADS, 1, LANES), BF16)
    coef = coef.at[:, 0, 0:6].set(jnp.stack(pieces(slopes * tq) + pieces(slopes), axis=1))
    hd = 2 * DIFF_HEAD_DIM
    nq = t // tq
    assert nq <= 256 and tq <= 256, "key block index and in-block position must be exact in bf16"
    pairs = [(i, j) for i in range(nq) for j in range(i + 1)]
    n_pairs = len(pairs)
    unroll = next(u for u in (34, 8, 4, 2) if n_pairs % u == 0)
    assert n_pairs % unroll == 0
    pairs = pairs + [pairs[-1]] * 2
    pair_i = jnp.asarray([p[0] for p in pairs], jnp.int32)
    pair_j = jnp.asarray([p[1] for p in pairs], jnp.int32)
    head = lambda shape: pl.BlockSpec(shape, lambda b, h, pi, pj: (b, 0, h))
    return pl.pallas_call(
        functools.partial(_diffattn_body, tq=tq, n_pairs=n_pairs, unroll=unroll, lam_init=lam_init),
        grid_spec=pltpu.PrefetchScalarGridSpec(
            num_scalar_prefetch=2,
            grid=(bsz, DIFF_HEADS),
            in_specs=[pl.BlockSpec(lam_vecs.shape, lambda b, h, pi, pj: (0, 0)),
                      pl.BlockSpec((1, DIFF_V_DIM), lambda b, h, pi, pj: (0, 0)),
                      pl.BlockSpec((1, 1, LANES), lambda b, h, pi, pj: (h, 0, 0)),
                      head((1, t, hd)),
                      pl.BlockSpec((1, hd, t), lambda b, h, pi, pj: (b, h, 0)),
                      head((1, t, DIFF_V_DIM))],
            out_specs=head((1, t, DIFF_V_DIM)),
            scratch_shapes=[pltpu.VMEM((nq, hd + LANES, tq), BF16),
                            pltpu.VMEM((t, DIFF_V_DIM + LANES), BF16),
                            pltpu.VMEM((2 * t, hd + LANES), BF16),
                            pltpu.VMEM((2, tq, tq), F32),
                            pltpu.VMEM((2, 2 * tq, tq), F32),
                            pltpu.VMEM((2, 2 * tq, tq), BF16),
                            pltpu.VMEM((2, 2 * tq, LANES), F32),
                            pltpu.VMEM((nq, 2 * tq, LANES), F32),
                            pltpu.VMEM((nq, 2 * tq, DIFF_V_DIM + LANES), F32)]),
        out_shape=jax.ShapeDtypeStruct((bsz, t, DIFF_V_W), BF16),
        compiler_params=_params(("parallel", "parallel")),
        name="diffattn",
    )(pair_i, pair_j, lam_vecs, subln_g.reshape(1, -1), coef, q, k_t, v)


def _router_body(h_ref, w_ref, o_ref, cnt_ref, carry_ref, *, tm):
    @pl.when(pl.program_id(0) == 0)
    def _():
        carry_ref[...] = jnp.zeros_like(carry_ref)

    h = h_ref[...]
    w = w_ref[...]
    h_hi = h.astype(BF16)
    h_lo = (h - h_hi.astype(F32)).astype(BF16)
    w_hi = w.astype(BF16)
    w_lo = (w - w_hi.astype(F32)).astype(BF16)
    logits = (jnp.dot(h_hi, w_hi, preferred_element_type=F32)
              + (jnp.dot(h_hi, w_lo, preferred_element_type=F32)
                 + jnp.dot(h_lo, w_hi, preferred_element_type=F32)))
    lane = lax.broadcasted_iota(jnp.int32, (tm, LANES), 1)
    lg = jnp.where(lane < N_EXPERTS, logits, -jnp.inf)
    v1 = jnp.max(lg, axis=-1, keepdims=True)
    i1 = jnp.min(jnp.where(lg == v1, lane, LANES), axis=-1, keepdims=True)
    oh1 = lane == i1
    lg2 = jnp.where(oh1, -jnp.inf, lg)
    v2 = jnp.max(lg2, axis=-1, keepdims=True)
    i2 = jnp.min(jnp.where(lg2 == v2, lane, LANES), axis=-1, keepdims=True)
    oh2 = lane == i2
    e2 = jnp.exp(v2 - v1)
    g1 = 1.0 / (1.0 + e2)
    g2 = e2 / (1.0 + e2)

    both = jnp.where(oh1, 1.0, jnp.where(oh2, 1.0, 0.0))
    tri = (lax.broadcasted_iota(jnp.int32, (tm, tm), 0)
           > lax.broadcasted_iota(jnp.int32, (tm, tm), 1))
    tri = jnp.where(tri, 1.0, 0.0).astype(BF16)
    before = jnp.dot(tri, both.astype(BF16), preferred_element_type=F32) + carry_ref[0:1, :]
    rank1 = jnp.sum(jnp.where(oh1, before, 0.0), axis=-1, keepdims=True)
    rank2 = jnp.sum(jnp.where(oh2, before, 0.0), axis=-1, keepdims=True)
    total = carry_ref[...] + jnp.sum(both, axis=0, keepdims=True)
    carry_ref[...] = total
    cnt_ref[...] = total

    out = jnp.where(lane == 0, i1.astype(F32), 0.0)
    out = jnp.where(lane == 1, i2.astype(F32), out)
    out = jnp.where(lane == 2, rank1, out)
    out = jnp.where(lane == 3, rank2, out)
    out = jnp.where(lane == 4, g1, out)
    out = jnp.where(lane == 5, g2, out)
    o_ref[...] = out


def _router(h, w_router, tm):
    m, d = h.shape
    w = jnp.zeros((d, LANES), F32).at[:, :N_EXPERTS].set(w_router)
    return pl.pallas_call(
        functools.partial(_router_body, tm=tm),
        grid=(m // tm,),
        in_specs=[pl.BlockSpec((tm, d), lambda i: (i, 0)),
                  pl.BlockSpec((d, LANES), lambda i: (0, 0))],
        out_specs=[pl.BlockSpec((tm, LANES), lambda i: (i, 0)),
                   pl.BlockSpec((SUBLANES, LANES), lambda i: (0, 0))],
        out_shape=[jax.ShapeDtypeStruct((m, LANES), F32),
                   jax.ShapeDtypeStruct((SUBLANES, LANES), F32)],
        scratch_shapes=[pltpu.VMEM((SUBLANES, LANES), F32)],
        compiler_params=_params(("arbitrary",)),
        name="router",
    )(h, w)


ROW_TILE = D_MODEL // LANES


def _row_tile(ref, r):
    start = r * ROW_TILE if isinstance(r, int) else pl.multiple_of(r * ROW_TILE, ROW_TILE)
    return ref.at[pl.ds(start, ROW_TILE), :]


def _store_row_tiles(tile_ref, x):
    for j in range(ROW_TILE):
        tile_ref[pl.ds(j, x.shape[0], stride=ROW_TILE), :] = x[:, j * LANES:(j + 1) * LANES]


def _load_row_tiles(tile_ref, n):
    return jnp.concatenate([tile_ref[pl.ds(j, n, stride=ROW_TILE), :] for j in range(ROW_TILE)], axis=1)


def _dispatch_body(dest_ref, meta_ref, h_ref, buf_ref, rows_ref, zero_ref, sem, *, tm, n_tok, n_rows):
    i = pl.program_id(0)
    base = i * tm
    _store_row_tiles(rows_ref, h_ref[...])

    def row_copy(r, d):
        return pltpu.make_async_copy(_row_tile(rows_ref, r), _row_tile(buf_ref, d), sem)

    def issue(r, c):
        row_copy(r, dest_ref[base + r]).start(priority=0)
        row_copy(r, dest_ref[n_tok + base + r]).start(priority=1)
        return c

    lax.fori_loop(0, tm, issue, 0, unroll=16)
    all_rows = buf_ref.at[pl.ds(0, TOP_K * tm * ROW_TILE), :]
    pltpu.make_async_copy(all_rows, all_rows, sem).wait()

    @pl.when(i == pl.num_programs(0) - 1)
    def _():
        zero_ref[...] = jnp.zeros_like(zero_ref)

        def zero_row(r):
            return pltpu.make_async_copy(_row_tile(zero_ref, 0), _row_tile(buf_ref, r), sem)

        def start_zero_row(r, c):
            zero_row(r).start()
            return c

        def wait_zero_row(r, c):
            zero_row(0).wait()
            return c

        for e in range(N_EXPERTS):
            lo = meta_ref[e] + meta_ref[N_EXPERTS + e]
            hi = meta_ref[e] + meta_ref[2 * N_EXPERTS + e]
            lax.fori_loop(lo, hi, start_zero_row, 0)
            lax.fori_loop(lo, hi, wait_zero_row, 0)

        def zero_block(b, c):
            r0 = pl.multiple_of(b * (MOE_ROWS * ROW_TILE), MOE_ROWS * ROW_TILE)
            cp = pltpu.make_async_copy(zero_ref, buf_ref.at[pl.ds(r0, MOE_ROWS * ROW_TILE), :], sem)
            cp.start()
            cp.wait()
            return c

        used = meta_ref[N_EXPERTS - 1] + meta_ref[3 * N_EXPERTS - 1]
        lax.fori_loop(used // MOE_ROWS, n_rows // MOE_ROWS, zero_block, 0)


def _dispatch(dest, meta, h, n_rows, tm):
    m, d = h.shape
    return pl.pallas_call(
        functools.partial(_dispatch_body, tm=tm, n_tok=m, n_rows=n_rows),
        grid_spec=pltpu.PrefetchScalarGridSpec(
            num_scalar_prefetch=2,
            grid=(m // tm,),
            in_specs=[pl.BlockSpec((tm, d), lambda i, dest, meta: (i, 0))],
            out_specs=pl.BlockSpec(memory_space=pl.ANY),
            scratch_shapes=[pltpu.VMEM((tm * ROW_TILE, LANES), F32),
                            pltpu.VMEM((MOE_ROWS * ROW_TILE, LANES), F32),
                            pltpu.SemaphoreType.DMA(())]),
        out_shape=jax.ShapeDtypeStruct((n_rows * ROW_TILE, LANES), F32),
        compiler_params=_params(("arbitrary",)),
        name="moe_dispatch",
    )(dest, meta, h)


def _moe_body(be_ref, nv_ref, x_ref, w13_ref, w2_ref, o_ref):
    b = pl.program_id(0)

    @pl.when(b < nv_ref[0])
    def _():
        xb = _load_row_tiles(x_ref, MOE_ROWS).astype(BF16)
        acc = None
        for c in range(FFN_DIM // FFN_TILE):
            cols = slice(c * FFN_TILE, (c + 1) * FFN_TILE)
            up_cols = slice(FFN_DIM + c * FFN_TILE, FFN_DIM + (c + 1) * FFN_TILE)
            gate = jnp.dot(xb, w13_ref[0, :, cols], preferred_element_type=F32)
            up = jnp.dot(xb, w13_ref[0, :, up_cols], preferred_element_type=F32)
            act = (jax.nn.silu(gate) * up).astype(BF16)
            part = jnp.dot(act, w2_ref[0, cols, :], preferred_element_type=F32)
            acc = part if acc is None else acc + part
        _store_row_tiles(o_ref, acc)

    @pl.when(b >= nv_ref[0])
    def _():
        o_ref[...] = jnp.zeros_like(o_ref)


def _moe(block_e, n_valid, buf, w13, w2):
    d = D_MODEL
    n_rows = buf.shape[0] // ROW_TILE
    nb = n_rows // MOE_ROWS
    row_block = (MOE_ROWS * ROW_TILE, LANES)

    def blk(b, nv):
        return jnp.minimum(b, nv[0] - 1)

    def expert(shape):
        return pl.BlockSpec(shape, lambda b, be, nv: (be[blk(b, nv)], 0, 0), pipeline_mode=pl.Buffered(1))

    return pl.pallas_call(
        _moe_body,
        grid_spec=pltpu.PrefetchScalarGridSpec(
            num_scalar_prefetch=2,
            grid=(nb,),
            in_specs=[pl.BlockSpec(row_block, lambda b, be, nv: (blk(b, nv), 0)),
                      expert((1, d, 2 * FFN_DIM)), expert((1, FFN_DIM, d))],
            out_specs=pl.BlockSpec(row_block, lambda b, be, nv: (b, 0))),
        out_shape=jax.ShapeDtypeStruct((n_rows * ROW_TILE, LANES), F32),
        compiler_params=_params(("arbitrary",)),
        name="moe_experts",
    )(block_e, n_valid, buf, w13, w2)


def _combine_ln_body(dest_ref, h_ref, route_ref, g_ref, b_ref, yb_ref, o_ref, rows_ref, sem,
                     *, tm, n_tok):
    i = pl.program_id(0)

    def gather_tile(tile, buf):
        base = tile * tm

        def issue(r, c):
            for k in range(TOP_K):
                pltpu.make_async_copy(_row_tile(yb_ref, dest_ref[k * n_tok + base + r]),
                                      _row_tile(rows_ref.at[buf, k], r), sem.at[buf]).start(priority=k)
            return c

        lax.fori_loop(0, tm, issue, 0, unroll=16)

    @pl.when(i == 0)
    def _():
        gather_tile(0, 0)

    @pl.when(i + 1 < pl.num_programs(0))
    def _():
        gather_tile(i + 1, (i + 1) % 2)

    buf = i % 2
    pltpu.make_async_copy(rows_ref.at[buf], rows_ref.at[buf], sem.at[buf]).wait()
    route = route_ref[...]
    y = (route[:, 4:5] * _load_row_tiles(rows_ref.at[buf, 0], tm)
         + route[:, 5:6] * _load_row_tiles(rows_ref.at[buf, 1], tm))
    z = DEEPNORM_ALPHA * h_ref[...] + y
    o_ref[...] = _layer_norm(z, g_ref[...], b_ref[...])


def _combine_ln(dest, h, route, yb, g, b, tm):
    m, d = h.shape
    return pl.pallas_call(
        functools.partial(_combine_ln_body, tm=tm, n_tok=m),
        grid_spec=pltpu.PrefetchScalarGridSpec(
            num_scalar_prefetch=1,
            grid=(m // tm,),
            in_specs=[pl.BlockSpec((tm, d), lambda i, dest: (i, 0)),
                      pl.BlockSpec((tm, LANES), lambda i, dest: (i, 0)),
                      pl.BlockSpec((1, d), lambda i, dest: (0, 0)),
                      pl.BlockSpec((1, d), lambda i, dest: (0, 0)),
                      pl.BlockSpec(memory_space=pl.ANY)],
            out_specs=pl.BlockSpec((tm, d), lambda i, dest: (i, 0)),
            scratch_shapes=[pltpu.VMEM((2, TOP_K, tm * ROW_TILE, LANES), F32),
                            pltpu.SemaphoreType.DMA((2,))]),
        out_shape=jax.ShapeDtypeStruct((m, d), F32),
        compiler_params=_params(("arbitrary",)),
        name="moe_combine_ln",
    )(dest, h, route, g.reshape(1, -1), b.reshape(1, -1), yb)


def _moe_layer(h, w_router, w13, w2, g, b):
    n_tok, _ = h.shape
    route, cnt = _router(h, w_router, tm=512)
    counts = cnt[0, :N_EXPERTS].astype(jnp.int32)
    padded = (counts + MOE_ROWS - 1) // MOE_ROWS * MOE_ROWS
    pstart = jnp.cumsum(padded) - padded
    experts = route[:, 0:2].astype(jnp.int32)
    ranks = route[:, 2:4].astype(jnp.int32)
    dest = (pstart[experts] + ranks).T.reshape(-1)
    n_rows = n_tok * TOP_K + N_EXPERTS * MOE_ROWS
    n_blocks = n_rows // MOE_ROWS
    block_end = jnp.cumsum(padded // MOE_ROWS)
    block_e = jnp.sum(jnp.arange(n_blocks)[:, None] >= block_end[None, :], axis=1)
    block_e = jnp.minimum(block_e, N_EXPERTS - 1).astype(jnp.int32)
    n_valid = block_end[-1:].astype(jnp.int32)
    meta = jnp.concatenate([pstart, counts, padded]).astype(jnp.int32)
    buf = _dispatch(dest, meta, h, n_rows, tm=512)
    yb = _moe(block_e, n_valid, buf, w13, w2)
    return _combine_ln(dest, h, route, yb, g, b, tm=512)


def kernel(x, mem, a_w_in, a_conv_w, a_conv_b, a_w_rgate, a_b_rgate, a_w_igate, a_b_igate, a_lambda,
           a_w_out, w_kv_shared, b_w_q, b_lambda, b_subln_g, b_w_out, mem_w_kv, ffn_w13, ffn_w2,
           moe_router, moe_w13, moe_w2, ln_g, ln_b):
    bsz, t, d = x.shape
    n_tok = bsz * t
    n_mem = mem.shape[1]
    bf = lambda a: a.astype(BF16)

    kv_mem = _proj(mem.reshape(bsz * n_mem, d), bf(jnp.concatenate([mem_w_kv[0], mem_w_kv[1]], axis=1)),
                   [(0, 2 * MEM_W), (2 * MEM_W, 4 * MEM_W)], [None, None], [BF16, BF16],
                   tm=min(1024, bsz * n_mem), name="mem_kv")
    kv_mem = [a.reshape(bsz, n_mem, 2 * MEM_W) for a in kv_mem]

    x2 = x.reshape(n_tok, d)
    gate, u_pre, q_mem = _proj(
        x2, bf(a_w_in[0]), [(0, D_RNN), (D_RNN, 2 * D_RNN), (2 * D_RNN, 2 * D_RNN + MEM_W)],
        [jax.nn.gelu, None, None], [BF16, BF16, BF16], tm=1024, name="proj_in")
    w_ri = bf(jnp.concatenate([a_w_rgate[0], a_w_igate[0]], axis=-1))
    rnn = _rglru(u_pre.reshape(bsz, t, D_RNN), gate.reshape(bsz, t, D_RNN), a_conv_w[0], a_conv_b[0],
                 w_ri, a_b_rgate[0], a_b_igate[0], a_lambda[0], tt=512)
    h = _outproj_ln(rnn, q_mem.reshape(bsz, t, MEM_W), kv_mem[0], bf(a_w_out[0]), x,
                    ln_g[0, 0], ln_b[0, 0], tm=1024, name="outproj_ln_a")
    h = _ffn_ln(h.reshape(n_tok, d), bf(ffn_w13[0]), bf(ffn_w2[0]), ln_g[0, 1], ln_b[0, 1], tm=512)

    layer = 1
    lam_init = 0.8 - 0.6 * math.exp(-0.3 * layer)
    w_cat = bf(jnp.concatenate([w_kv_shared[:, DIFF_QK_W:], b_w_q[0]], axis=1))
    w_k_t = bf(w_kv_shared[:, :DIFF_QK_W].T)
    v_sh, k_t, q_diff, q_mem = _proj_kvq(h, w_cat, w_k_t, bsz, t, tm=1024)
    attn = _diffattn(q_diff.reshape(bsz, t, DIFF_QK_W), k_t, v_sh.reshape(bsz, t, DIFF_V_W),
                     b_lambda[0], b_subln_g[0], lam_init, tq=256)
    h = _outproj_ln(attn, q_mem.reshape(bsz, t, MEM_W), kv_mem[1], bf(b_w_out[0]),
                    h.reshape(bsz, t, d), ln_g[1, 0], ln_b[1, 0], tm=1024, name="outproj_ln_b")
    out = _moe_layer(h.reshape(n_tok, d), moe_router[0], bf(moe_w13[0]), bf(moe_w2[0]),
                     ln_g[1, 1], ln_b[1, 1])
    return out.reshape(bsz, t, d)
```

```python
import functools
import math

import jax
import jax.numpy as jnp
from jax import lax
from jax.experimental import pallas as pl
from jax.experimental.pallas import tpu as pltpu

F32 = jnp.float32
BF16 = jnp.bfloat16

D_MODEL = 1024
DEPTH = 2
D_RNN = D_MODEL
RNN_BLOCKS = 8
RNN_BLOCK_W = D_RNN // RNN_BLOCKS
CONV_W = 4
LRU_C = 8.0
MEM_HEADS = 4
MEM_HEAD_DIM = D_MODEL // 8
MEM_W = MEM_HEADS * MEM_HEAD_DIM
DIFF_HEADS = 8
DIFF_HEAD_DIM = D_MODEL // 16
DIFF_V_DIM = 2 * DIFF_HEAD_DIM
DIFF_QK_W = DIFF_HEADS * 2 * DIFF_HEAD_DIM
DIFF_V_W = DIFF_HEADS * DIFF_V_DIM
FFN_DIM = (7 * D_MODEL) // 2
N_EXPERTS = 8
TOP_K = 2
LN_EPS = 1e-5
DEEPNORM_ALPHA = (2.0 * DEPTH) ** 0.25

LANES = 128
SUBLANES = 8
VMEM_LIMIT = 52 * 1024 * 1024

FFN_TILE = 1792
MOE_ROWS = 512
NEG_BIG = -1e30
LOG2E = math.log2(math.e)


def _params(semantics):
    return pltpu.CompilerParams(dimension_semantics=semantics, vmem_limit_bytes=VMEM_LIMIT)


def _layer_norm(z, g, b):
    mu = jnp.mean(z, axis=-1, keepdims=True)
    zc = z - mu
    var = jnp.mean(zc * zc, axis=-1, keepdims=True)
    return zc * lax.rsqrt(var + LN_EPS) * g + b


def _proj_body(x_ref, w_ref, *o_refs, splits, post):
    xb = x_ref[...].astype(BF16)
    for o_ref, (c0, c1), fn in zip(o_refs, splits, post):
        z = jnp.dot(xb, w_ref[:, c0:c1], preferred_element_type=F32)
        if fn is not None:
            z = fn(z)
        o_ref[...] = z.astype(o_ref.dtype)


def _proj(x, w, splits, post, out_dtypes, tm, name):
    m, k = x.shape
    return pl.pallas_call(
        functools.partial(_proj_body, splits=tuple(splits), post=tuple(post)),
        grid=(m // tm,),
        in_specs=[pl.BlockSpec((tm, k), lambda i: (i, 0)),
                  pl.BlockSpec(w.shape, lambda i: (0, 0))],
        out_specs=[pl.BlockSpec((tm, c1 - c0), lambda i: (i, 0)) for c0, c1 in splits],
        out_shape=[jax.ShapeDtypeStruct((m, c1 - c0), dt) for (c0, c1), dt in zip(splits, out_dtypes)],
        compiler_params=_params(("parallel",)),
        name=name,
    )(x, w)


def _proj_kvq_body(x_ref, w_ref, wkt_ref, v_ref, kt_ref, qd_ref, qm_ref):
    xb = x_ref[...].astype(BF16)
    q_scale = DIFF_HEAD_DIM ** -0.5 * LOG2E
    v_ref[...] = jnp.dot(xb, w_ref[:, 0:DIFF_V_W], preferred_element_type=F32).astype(BF16)
    kt_ref[0] = lax.dot_general(wkt_ref[...], xb, (((1,), (1,)), ((), ())),
                                preferred_element_type=F32).astype(BF16)
    qd = jnp.dot(xb, w_ref[:, DIFF_V_W:DIFF_V_W + DIFF_QK_W], preferred_element_type=F32)
    qd_ref[...] = (qd * q_scale).astype(BF16)
    qm_ref[...] = jnp.dot(xb, w_ref[:, DIFF_V_W + DIFF_QK_W:], preferred_element_type=F32).astype(BF16)


def _proj_kvq(h, w_cat, w_k_t, bsz, t, tm):
    m, d = h.shape
    nt = t // tm
    rows = lambda width: pl.BlockSpec((tm, width), lambda i: (i, 0))
    return pl.pallas_call(
        _proj_kvq_body,
        grid=(m // tm,),
        in_specs=[rows(d), pl.BlockSpec(w_cat.shape, lambda i: (0, 0)),
                  pl.BlockSpec(w_k_t.shape, lambda i: (0, 0))],
        out_specs=[rows(DIFF_V_W), pl.BlockSpec((1, DIFF_QK_W, tm), lambda i: (i // nt, 0, i % nt)),
                   rows(DIFF_QK_W), rows(MEM_W)],
        out_shape=[jax.ShapeDtypeStruct((m, DIFF_V_W), BF16),
                   jax.ShapeDtypeStruct((bsz, DIFF_QK_W, t), BF16),
                   jax.ShapeDtypeStruct((m, DIFF_QK_W), BF16),
                   jax.ShapeDtypeStruct((m, MEM_W), BF16)],
        compiler_params=_params(("parallel",)),
        name="proj_kvq",
    )(h, w_cat, w_k_t)


def _rglru_body(u_ref, g_ref, cw_ref, cb_ref, wri_ref, br_ref, bi_ref, lam_ref, o_ref,
                ext_ref, rec_ref, a_ref, b_ref, h_ref, *, tt):
    t = pl.program_id(1)

    @pl.when(t == 0)
    def _():
        ext_ref[0:SUBLANES, :] = jnp.zeros((SUBLANES, D_RNN), F32)
        h_ref[...] = jnp.zeros((SUBLANES, D_RNN), F32)

    ext_ref[SUBLANES:, :] = u_ref[0].astype(F32)
    rec = cb_ref[...] + cw_ref[CONV_W - 1:CONV_W, :] * ext_ref[SUBLANES:SUBLANES + tt, :]
    for j in range(CONV_W - 1):
        off = SUBLANES - (CONV_W - 1) + j
        rec = rec + cw_ref[j:j + 1, :] * ext_ref[off:off + tt, :]
    rec_ref[...] = rec
    ext_ref[0:SUBLANES, :] = ext_ref[tt:tt + SUBLANES, :]

    sp = jax.nn.softplus(-lam_ref[...])
    first_row = (lax.broadcasted_iota(jnp.int32, (SUBLANES, RNN_BLOCK_W), 0) + t * tt) == 0
    for n in range(RNN_BLOCKS):
        blk = slice(n * RNN_BLOCK_W, (n + 1) * RNN_BLOCK_W)
        u_n = rec_ref[:, blk]
        ri = jnp.dot(u_n.astype(BF16), wri_ref[n], preferred_element_type=F32)
        r_t = jnp.tanh(0.5 * (ri[:, :RNN_BLOCK_W] + br_ref[:, blk]))
        i = 0.5 * jnp.tanh(0.5 * (ri[:, RNN_BLOCK_W:] + bi_ref[:, blk])) + 0.5
        half = (-0.5 * LRU_C) * sp[:, blk]
        log_a = half * r_t + half
        a = jnp.exp(log_a)
        th = jnp.tanh(-log_a)
        m2 = 2.0 * th / (1.0 + th)
        mult = jnp.where(m2 > 0.0, m2 * lax.rsqrt(m2), 0.0)
        mult = jnp.concatenate([jnp.where(first_row, 1.0, mult[:SUBLANES]), mult[SUBLANES:]], axis=0)
        a_ref[:, blk] = a
        b_ref[:, blk] = mult * (i * u_n)

    row = lax.broadcasted_iota(jnp.int32, (SUBLANES, D_RNN), 0)

    def tile(i, h_prev):
        r0 = pl.multiple_of(i * SUBLANES, SUBLANES)
        a = a_ref[pl.ds(r0, SUBLANES), :]
        b = b_ref[pl.ds(r0, SUBLANES), :]
        for s in (1, 2, 4):
            a_s = pltpu.roll(a, s, 0)
            b_s = pltpu.roll(b, s, 0)
            keep = row >= s
            b = jnp.where(keep, a * b_s + b, b)
            a = jnp.where(keep, a * a_s, a)
        h = b + a * h_prev
        b_ref[pl.ds(r0, SUBLANES), :] = h
        return jnp.broadcast_to(h[SUBLANES - 1:SUBLANES, :], (SUBLANES, D_RNN))

    h_ref[...] = lax.fori_loop(0, tt // SUBLANES, tile, h_ref[...], unroll=4)
    o_ref[0] = (b_ref[...] * g_ref[0].astype(F32)).astype(o_ref.dtype)


def _rglru(u_pre, gate, conv_w, conv_b, w_ri, b_r, b_i, lam, tt):
    bsz, t, c = u_pre.shape
    row = lambda a: a.reshape(1, c)
    full = lambda shape: pl.BlockSpec(shape, lambda b, i: (0,) * len(shape))
    return pl.pallas_call(
        functools.partial(_rglru_body, tt=tt),
        grid=(bsz, t // tt),
        in_specs=[pl.BlockSpec((1, tt, c), lambda b, i: (b, i, 0)),
                  pl.BlockSpec((1, tt, c), lambda b, i: (b, i, 0)),
                  full((CONV_W, c)), full((1, c)), full(w_ri.shape),
                  full((1, c)), full((1, c)), full((1, c))],
        out_specs=pl.BlockSpec((1, tt, c), lambda b, i: (b, i, 0)),
        out_shape=jax.ShapeDtypeStruct((bsz, t, c), BF16),
        scratch_shapes=[pltpu.VMEM((tt + SUBLANES, c), F32), pltpu.VMEM((tt, c), F32),
                        pltpu.VMEM((tt, c), F32), pltpu.VMEM((tt, c), F32),
                        pltpu.VMEM((SUBLANES, c), F32)],
        compiler_params=_params(("parallel", "arbitrary")),
        name="rglru",
    )(u_pre, gate, conv_w, row(conv_b), w_ri, row(b_r), row(b_i), row(lam))


def _outproj_ln_body(a_ref, qm_ref, kv_ref, w_ref, res_ref, g_ref, b_ref, *rest):
    o_ref = rest[0] if len(rest) == 1 else rest[1]
    mix_w = a_ref.shape[-1]
    y = jnp.dot(a_ref[0], w_ref[0:mix_w, :], preferred_element_type=F32)
    scale = MEM_HEAD_DIM ** -0.5 * LOG2E
    heads = []
    for h in range(MEM_HEADS):
        hs = slice(h * MEM_HEAD_DIM, (h + 1) * MEM_HEAD_DIM)
        vs = slice(MEM_W + h * MEM_HEAD_DIM, MEM_W + (h + 1) * MEM_HEAD_DIM)
        s = lax.dot_general(qm_ref[0, :, hs], kv_ref[0, :, hs], (((1,), (1,)), ((), ())),
                            preferred_element_type=F32)
        e = jnp.exp2((s - jnp.max(s, axis=-1, keepdims=True)) * scale)
        pv = jnp.dot(e.astype(BF16), kv_ref[0, :, vs], preferred_element_type=F32)
        heads.append(pv / jnp.sum(e, axis=-1, keepdims=True))
    mem_out = jnp.concatenate(heads, axis=1).astype(BF16)
    y = y + jnp.dot(mem_out, w_ref[mix_w:, :], preferred_element_type=F32)
    z = DEEPNORM_ALPHA * res_ref[0] + y
    h_new = _layer_norm(z, g_ref[...], b_ref[...])
    o_ref[0] = h_new
    if len(rest) > 1:
        w_router_ref, _, route_ref, cnt_ref, carry_ref = rest

        @pl.when(jnp.logical_and(pl.program_id(0) == 0, pl.program_id(1) == 0))
        def _():
            carry_ref[...] = jnp.zeros_like(carry_ref)

        for r in range(h_new.shape[0] // ROUTE_ROWS):
            rows = slice(r * ROUTE_ROWS, (r + 1) * ROUTE_ROWS)
            _route_rows(h_new[rows], w_router_ref, route_ref, rows, cnt_ref, carry_ref)


def _outproj_ln(mix, q_mem, kv_mem, w_out, resid, g, b, tm, name, w_router=None):
    bsz, t, mix_w = mix.shape
    n_mem = kv_mem.shape[1]
    const = lambda shape: pl.BlockSpec(shape, lambda b, i: (0,) * len(shape))
    rows = lambda width: pl.BlockSpec((1, tm, width), lambda b, i: (b, i, 0))
    in_specs = [rows(mix_w), rows(MEM_W), pl.BlockSpec((1, n_mem, 2 * MEM_W), lambda b, i: (b, 0, 0)),
                const(w_out.shape), rows(D_MODEL), const((1, D_MODEL)), const((1, D_MODEL))]
    args = [mix, q_mem, kv_mem, w_out, resid, g.reshape(1, -1), b.reshape(1, -1)]
    out_specs = rows(D_MODEL)
    out_shape = jax.ShapeDtypeStruct((bsz, t, D_MODEL), F32)
    scratch, semantics = [], ("parallel", "parallel")
    if w_router is not None:
        w_pad = jnp.zeros((D_MODEL, LANES), F32).at[:, :N_EXPERTS].set(w_router)
        in_specs.append(const(w_pad.shape))
        args.append(w_pad)
        out_specs = [out_specs, rows(LANES), const((SUBLANES, LANES))]
        out_shape = [out_shape, jax.ShapeDtypeStruct((bsz, t, LANES), F32),
                     jax.ShapeDtypeStruct((SUBLANES, LANES), F32)]
        scratch, semantics = [pltpu.VMEM((SUBLANES, LANES), F32)], ("arbitrary", "arbitrary")
    return pl.pallas_call(
        _outproj_ln_body,
        grid=(bsz, t // tm),
        in_specs=in_specs,
        out_specs=out_specs,
        out_shape=out_shape,
        scratch_shapes=scratch,
        compiler_params=_params(semantics),
        name=name,
    )(*args)


def _ffn_ln_body(h_ref, w13_ref, w2_ref, g_ref, b_ref, o_ref):
    h = h_ref[...]
    hb = h.astype(BF16)
    acc = None
    for c in range(FFN_DIM // FFN_TILE):
        cols = slice(c * FFN_TILE, (c + 1) * FFN_TILE)
        up_cols = slice(FFN_DIM + c * FFN_TILE, FFN_DIM + (c + 1) * FFN_TILE)
        gate = jnp.dot(hb, w13_ref[:, cols], preferred_element_type=F32)
        up = jnp.dot(hb, w13_ref[:, up_cols], preferred_element_type=F32)
        act = (jax.nn.silu(gate) * up).astype(BF16)
        part = jnp.dot(act, w2_ref[cols, :], preferred_element_type=F32)
        acc = part if acc is None else acc + part
    o_ref[...] = _layer_norm(DEEPNORM_ALPHA * h + acc, g_ref[...], b_ref[...])


def _ffn_ln(h, w13, w2, g, b, tm):
    m, d = h.shape
    resident = lambda shape: pl.BlockSpec(shape, lambda i: (0, 0), pipeline_mode=pl.Buffered(1))
    return pl.pallas_call(
        _ffn_ln_body,
        grid=(m // tm,),
        in_specs=[pl.BlockSpec((tm, d), lambda i: (i, 0)),
                  resident(w13.shape), resident(w2.shape),
                  pl.BlockSpec((1, d), lambda i: (0, 0)),
                  pl.BlockSpec((1, d), lambda i: (0, 0))],
        out_specs=pl.BlockSpec((tm, d), lambda i: (i, 0)),
        out_shape=jax.ShapeDtypeStruct((m, d), F32),
        compiler_params=_params(("parallel",)),
        name="ffn_ln",
    )(h, w13, w2, g.reshape(1, -1), b.reshape(1, -1))


ATT_STRIP = 128


def _diffattn_body(pi_ref, pj_ref, lamv_ref, sub_ref, coef_ref, q_ref, kt_ref, v_ref, o_ref,
                   kaug_ref, vaug_ref, qq_ref, mask_ref, s_ref, p_ref, c_ref, m_ref, acc_ref,
                   *, tq, n_pairs, unroll, lam_init):
    t = v_ref.shape[1]
    hd = 2 * DIFF_HEAD_DIM
    n_strips = 2 * tq // ATT_STRIP

    feat_row = lax.broadcasted_iota(jnp.int32, (LANES, tq), 0)
    in_block = lax.broadcasted_iota(jnp.int32, (LANES, tq), 1)
    for kb in range(t // tq):
        feat = jnp.where(feat_row < 3, kb, jnp.where(feat_row < 6, in_block, 0))
        kaug_ref[kb, 0:hd, :] = kt_ref[0, :, kb * tq:(kb + 1) * tq]
        kaug_ref[kb, hd:, :] = feat.astype(F32).astype(BF16)
    vaug_ref[:, 0:DIFF_V_DIM] = v_ref[0]
    vaug_ref[:, DIFF_V_DIM:] = jnp.ones((t, LANES), BF16)
    lane = lax.broadcasted_iota(jnp.int32, (tq, hd), 1)
    coef_rows = jnp.broadcast_to(coef_ref[0], (2 * tq, LANES))
    for qb in range(t // tq):
        q = q_ref[0, qb * tq:(qb + 1) * tq, :]
        zero = jnp.zeros_like(q)
        qq_ref[qb * 2 * tq:qb * 2 * tq + tq, 0:hd] = jnp.where(lane < DIFF_HEAD_DIM, q, zero)
        qq_ref[qb * 2 * tq + tq:(qb + 1) * 2 * tq, 0:hd] = jnp.where(lane < DIFF_HEAD_DIM, zero, q)
        qq_ref[qb * 2 * tq:(qb + 1) * 2 * tq, hd:] = coef_rows
    q_row = lax.broadcasted_iota(jnp.int32, (tq, tq), 0)
    k_col = lax.broadcasted_iota(jnp.int32, (tq, tq), 1)
    mask_ref[0] = jnp.zeros((tq, tq), F32)
    mask_ref[1] = jnp.where(k_col <= q_row, 0.0, NEG_BIG)
    m_ref[...] = jnp.full(m_ref.shape, NEG_BIG, F32)
    acc_ref[...] = jnp.zeros_like(acc_ref)

    def stage_qk(p, slot):
        q0 = pl.multiple_of(pi_ref[p] * (2 * tq), 2 * tq)
        s_ref[slot] = jnp.dot(qq_ref[pl.ds(q0, 2 * tq), :], kaug_ref[pj_ref[p]],
                              preferred_element_type=F32)

    def stage_sm(p, slot):
        i = pi_ref[p]
        diag = (i == pj_ref[p]).astype(jnp.int32)
        for r in range(n_strips):
            rows = slice(r * ATT_STRIP, (r + 1) * ATT_STRIP)
            mrows = slice((r * ATT_STRIP) % tq, (r * ATT_STRIP) % tq + ATT_STRIP)
            s = s_ref[slot, rows, :] + mask_ref[diag, mrows, :]
            m_prev = m_ref[i, rows, :]
            m_next = jnp.maximum(m_prev, jnp.max(s, axis=-1, keepdims=True))
            p_blk = jnp.exp2(s - jnp.tile(m_next, (1, tq // LANES)))
            p_ref[slot, rows, :] = p_blk.astype(BF16)
            c_ref[slot, rows, :] = jnp.exp2(m_prev - m_next)
            m_ref[i, rows, :] = m_next

    def stage_pv(p, slot):
        i = pi_ref[p]
        k0 = pl.multiple_of(pj_ref[p] * tq, tq)
        pv = jnp.dot(p_ref[slot], vaug_ref[pl.ds(k0, tq), :], preferred_element_type=F32)
        acc_ref[i] = acc_ref[i] * jnp.tile(c_ref[slot], (1, 2)) + pv

    stage_qk(0, 0)
    stage_qk(1, 1)
    stage_sm(0, 0)

    def pipeline_step(step, c):
        for u in range(unroll):
            p = unroll * step + u
            stage_qk(p + 2, u % 2)
            stage_sm(p + 1, (u + 1) % 2)
            stage_pv(p, u % 2)
        return c

    lax.fori_loop(0, n_pairs // unroll, pipeline_step, 0)

    lv = lamv_ref[...]
    lam = (jnp.exp(jnp.sum(lv[0:1] * lv[1:2], axis=-1, keepdims=True))
           - jnp.exp(jnp.sum(lv[2:3] * lv[3:4], axis=-1, keepdims=True)) + lam_init)

    def finalize(i, c):
        o = acc_ref[i, :, 0:DIFF_V_DIM] / acc_ref[i, :, DIFF_V_DIM:]
        o = o[:tq] - lam * o[tq:]
        o = o * lax.rsqrt(jnp.mean(o * o, axis=-1, keepdims=True) + LN_EPS) * sub_ref[...]
        r0 = pl.multiple_of(i * tq, tq)
        o_ref[0, pl.ds(r0, tq), :] = (o * (1.0 - lam_init)).astype(o_ref.dtype)
        return c

    lax.fori_loop(0, t // tq, finalize, 0, unroll=4)


def _diffattn(q, k_t, v, lam_vecs, subln_g, lam_init, tq):
    bsz, t, _ = q.shape
    slopes = jnp.exp2(-8.0 * (jnp.arange(DIFF_HEADS, dtype=F32) + 1.0) / DIFF_HEADS) * LOG2E
    def pieces(c):
        c1 = c.astype(BF16)
        c2 = (c - c1.astype(F32)).astype(BF16)
        c3 = (c - c1.astype(F32) - c2.astype(F32)).astype(BF16)
        return [c1, c2, c3]

    coef = jnp.zeros((DIFF_HEADS, 1, LANES), BF16)
    coef = coef.at[:, 0, 0:6].set(jnp.stack(pieces(slopes * tq) + pieces(slopes), axis=1))
    hd = 2 * DIFF_HEAD_DIM
    nq = t // tq
    assert nq <= 256 and tq <= 256, "key block index and in-block position must be exact in bf16"
    pairs = [(i, j) for i in range(nq) for j in range(i + 1)]
    n_pairs = len(pairs)
    unroll = next(u for u in (34, 8, 4, 2) if n_pairs % u == 0)
    assert n_pairs % unroll == 0
    pairs = pairs + [pairs[-1]] * 2
    pair_i = jnp.asarray([p[0] for p in pairs], jnp.int32)
    pair_j = jnp.asarray([p[1] for p in pairs], jnp.int32)
    head = lambda shape: pl.BlockSpec(shape, lambda b, h, pi, pj: (b, 0, h))
    return pl.pallas_call(
        functools.partial(_diffattn_body, tq=tq, n_pairs=n_pairs, unroll=unroll, lam_init=lam_init),
        grid_spec=pltpu.PrefetchScalarGridSpec(
            num_scalar_prefetch=2,
            grid=(bsz, DIFF_HEADS),
            in_specs=[pl.BlockSpec(lam_vecs.shape, lambda b, h, pi, pj: (0, 0)),
                      pl.BlockSpec((1, DIFF_V_DIM), lambda b, h, pi, pj: (0, 0)),
                      pl.BlockSpec((1, 1, LANES), lambda b, h, pi, pj: (h, 0, 0)),
                      head((1, t, hd)),
                      pl.BlockSpec((1, hd, t), lambda b, h, pi, pj: (b, h, 0)),
                      head((1, t, DIFF_V_DIM))],
            out_specs=head((1, t, DIFF_V_DIM)),
            scratch_shapes=[pltpu.VMEM((nq, hd + LANES, tq), BF16),
                            pltpu.VMEM((t, DIFF_V_DIM + LANES), BF16),
                            pltpu.VMEM((2 * t, hd + LANES), BF16),
                            pltpu.VMEM((2, tq, tq), F32),
                            pltpu.VMEM((2, 2 * tq, tq), F32),
                            pltpu.VMEM((2, 2 * tq, tq), BF16),
                            pltpu.VMEM((2, 2 * tq, LANES), F32),
                            pltpu.VMEM((nq, 2 * tq, LANES), F32),
                            pltpu.VMEM((nq, 2 * tq, DIFF_V_DIM + LANES), F32)]),
        out_shape=jax.ShapeDtypeStruct((bsz, t, DIFF_V_W), BF16),
        compiler_params=_params(("parallel", "parallel")),
        name="diffattn",
    )(pair_i, pair_j, lam_vecs, subln_g.reshape(1, -1), coef, q, k_t, v)


ROUTE_ROWS = 512


def _route_rows(h, w_ref, o_ref, rows, cnt_ref, carry_ref):
    tm = ROUTE_ROWS
    w = w_ref[...]
    h_hi = h.astype(BF16)
    h_lo = (h - h_hi.astype(F32)).astype(BF16)
    w_hi = w.astype(BF16)
    w_lo = (w - w_hi.astype(F32)).astype(BF16)
    logits = (jnp.dot(h_hi, w_hi, preferred_element_type=F32)
              + (jnp.dot(h_hi, w_lo, preferred_element_type=F32)
                 + jnp.dot(h_lo, w_hi, preferred_element_type=F32)))
    lane = lax.broadcasted_iota(jnp.int32, (tm, LANES), 1)
    lg = jnp.where(lane < N_EXPERTS, logits, -jnp.inf)
    v1 = jnp.max(lg, axis=-1, keepdims=True)
    i1 = jnp.min(jnp.where(lg == v1, lane, LANES), axis=-1, keepdims=True)
    oh1 = lane == i1
    lg2 = jnp.where(oh1, -jnp.inf, lg)
    v2 = jnp.max(lg2, axis=-1, keepdims=True)
    i2 = jnp.min(jnp.where(lg2 == v2, lane, LANES), axis=-1, keepdims=True)
    oh2 = lane == i2
    e2 = jnp.exp(v2 - v1)
    g1 = 1.0 / (1.0 + e2)
    g2 = e2 / (1.0 + e2)

    both = jnp.where(oh1, 1.0, jnp.where(oh2, 1.0, 0.0))
    tri = (lax.broadcasted_iota(jnp.int32, (tm, tm), 0)
           > lax.broadcasted_iota(jnp.int32, (tm, tm), 1))
    tri = jnp.where(tri, 1.0, 0.0).astype(BF16)
    before = jnp.dot(tri, both.astype(BF16), preferred_element_type=F32) + carry_ref[0:1, :]
    rank1 = jnp.sum(jnp.where(oh1, before, 0.0), axis=-1, keepdims=True)
    rank2 = jnp.sum(jnp.where(oh2, before, 0.0), axis=-1, keepdims=True)
    total = carry_ref[...] + jnp.sum(both, axis=0, keepdims=True)
    carry_ref[...] = total
    cnt_ref[...] = total

    out = jnp.where(lane == 0, i1.astype(F32), 0.0)
    out = jnp.where(lane == 1, i2.astype(F32), out)
    out = jnp.where(lane == 2, rank1, out)
    out = jnp.where(lane == 3, rank2, out)
    out = jnp.where(lane == 4, g1, out)
    out = jnp.where(lane == 5, g2, out)
    o_ref[0, rows, :] = out


ROW_TILE = D_MODEL // LANES


def _row_tile(ref, r):
    start = r * ROW_TILE if isinstance(r, int) else pl.multiple_of(r * ROW_TILE, ROW_TILE)
    return ref.at[pl.ds(start, ROW_TILE), :]


def _store_row_tiles(tile_ref, x):
    for j in range(ROW_TILE):
        tile_ref[pl.ds(j, x.shape[0], stride=ROW_TILE), :] = x[:, j * LANES:(j + 1) * LANES]


def _load_row_tiles(tile_ref, n):
    return jnp.concatenate([tile_ref[pl.ds(j, n, stride=ROW_TILE), :] for j in range(ROW_TILE)], axis=1)


def _dispatch_body(dest_ref, meta_ref, h_ref, buf_ref, rows_ref, zero_ref, sem, *, tm, n_tok, n_rows):
    i = pl.program_id(0)
    base = i * tm
    _store_row_tiles(rows_ref, h_ref[...])

    def row_copy(r, d):
        return pltpu.make_async_copy(_row_tile(rows_ref, r), _row_tile(buf_ref, d), sem)

    def issue(r, c):
        row_copy(r, dest_ref[base + r]).start(priority=0)
        row_copy(r, dest_ref[n_tok + base + r]).start(priority=1)
        return c

    lax.fori_loop(0, tm, issue, 0, unroll=16)
    all_rows = buf_ref.at[pl.ds(0, TOP_K * tm * ROW_TILE), :]
    pltpu.make_async_copy(all_rows, all_rows, sem).wait()

    @pl.when(i == pl.num_programs(0) - 1)
    def _():
        zero_ref[...] = jnp.zeros_like(zero_ref)

        def zero_row(r):
            return pltpu.make_async_copy(_row_tile(zero_ref, 0), _row_tile(buf_ref, r), sem)

        def start_zero_row(r, c):
            zero_row(r).start()
            return c

        def wait_zero_row(r, c):
            zero_row(0).wait()
            return c

        for e in range(N_EXPERTS):
            lo = meta_ref[e] + meta_ref[N_EXPERTS + e]
            hi = meta_ref[e] + meta_ref[2 * N_EXPERTS + e]
            lax.fori_loop(lo, hi, start_zero_row, 0)
            lax.fori_loop(lo, hi, wait_zero_row, 0)

        def zero_block(b, c):
            r0 = pl.multiple_of(b * (MOE_ROWS * ROW_TILE), MOE_ROWS * ROW_TILE)
            cp = pltpu.make_async_copy(zero_ref, buf_ref.at[pl.ds(r0, MOE_ROWS * ROW_TILE), :], sem)
            cp.start()
            cp.wait()
            return c

        used = meta_ref[N_EXPERTS - 1] + meta_ref[3 * N_EXPERTS - 1]
        lax.fori_loop(used // MOE_ROWS, n_rows // MOE_ROWS, zero_block, 0)


def _dispatch(dest, meta, h, n_rows, tm):
    m, d = h.shape
    return pl.pallas_call(
        functools.partial(_dispatch_body, tm=tm, n_tok=m, n_rows=n_rows),
        grid_spec=pltpu.PrefetchScalarGridSpec(
            num_scalar_prefetch=2,
            grid=(m // tm,),
            in_specs=[pl.BlockSpec((tm, d), lambda i, dest, meta: (i, 0))],
            out_specs=pl.BlockSpec(memory_space=pl.ANY),
            scratch_shapes=[pltpu.VMEM((tm * ROW_TILE, LANES), F32),
                            pltpu.VMEM((MOE_ROWS * ROW_TILE, LANES), F32),
                            pltpu.SemaphoreType.DMA(())]),
        out_shape=jax.ShapeDtypeStruct((n_rows * ROW_TILE, LANES), F32),
        compiler_params=_params(("arbitrary",)),
        name="moe_dispatch",
    )(dest, meta, h)


def _moe_body(be_ref, nv_ref, x_ref, w13_ref, w2_ref, o_ref):
    b = pl.program_id(0)

    @pl.when(b < nv_ref[0])
    def _():
        xb = _load_row_tiles(x_ref, MOE_ROWS).astype(BF16)
        acc = None
        for c in range(FFN_DIM // FFN_TILE):
            cols = slice(c * FFN_TILE, (c + 1) * FFN_TILE)
            up_cols = slice(FFN_DIM + c * FFN_TILE, FFN_DIM + (c + 1) * FFN_TILE)
            gate = jnp.dot(xb, w13_ref[0, :, cols], preferred_element_type=F32)
            up = jnp.dot(xb, w13_ref[0, :, up_cols], preferred_element_type=F32)
            act = (jax.nn.silu(gate) * up).astype(BF16)
            part = jnp.dot(act, w2_ref[0, cols, :], preferred_element_type=F32)
            acc = part if acc is None else acc + part
        _store_row_tiles(o_ref, acc)

    @pl.when(b >= nv_ref[0])
    def _():
        o_ref[...] = jnp.zeros_like(o_ref)


def _moe(block_e, n_valid, buf, w13, w2):
    d = D_MODEL
    n_rows = buf.shape[0] // ROW_TILE
    nb = n_rows // MOE_ROWS
    row_block = (MOE_ROWS * ROW_TILE, LANES)

    def blk(b, nv):
        return jnp.minimum(b, nv[0] - 1)

    def expert(shape):
        return pl.BlockSpec(shape, lambda b, be, nv: (be[blk(b, nv)], 0, 0), pipeline_mode=pl.Buffered(1))

    return pl.pallas_call(
        _moe_body,
        grid_spec=pltpu.PrefetchScalarGridSpec(
            num_scalar_prefetch=2,
            grid=(nb,),
            in_specs=[pl.BlockSpec(row_block, lambda b, be, nv: (blk(b, nv), 0)),
                      expert((1, d, 2 * FFN_DIM)), expert((1, FFN_DIM, d))],
            out_specs=pl.BlockSpec(row_block, lambda b, be, nv: (b, 0))),
        out_shape=jax.ShapeDtypeStruct((n_rows * ROW_TILE, LANES), F32),
        compiler_params=_params(("arbitrary",)),
        name="moe_experts",
    )(block_e, n_valid, buf, w13, w2)


def _combine_ln_body(dest_ref, h_ref, route_ref, g_ref, b_ref, yb_ref, o_ref, rows_ref, sem,
                     *, tm, n_tok):
    i = pl.program_id(0)

    def gather_tile(tile, buf):
        base = tile * tm

        def issue(r, c):
            for k in range(TOP_K):
                pltpu.make_async_copy(_row_tile(yb_ref, dest_ref[k * n_tok + base + r]),
                                      _row_tile(rows_ref.at[buf, k], r), sem.at[buf]).start(priority=k)
            return c

        lax.fori_loop(0, tm, issue, 0, unroll=16)

    @pl.when(i == 0)
    def _():
        gather_tile(0, 0)

    @pl.when(i + 1 < pl.num_programs(0))
    def _():
        gather_tile(i + 1, (i + 1) % 2)

    buf = i % 2
    pltpu.make_async_copy(rows_ref.at[buf], rows_ref.at[buf], sem.at[buf]).wait()
    route = route_ref[...]
    y = (route[:, 4:5] * _load_row_tiles(rows_ref.at[buf, 0], tm)
         + route[:, 5:6] * _load_row_tiles(rows_ref.at[buf, 1], tm))
    z = DEEPNORM_ALPHA * h_ref[...] + y
    o_ref[...] = _layer_norm(z, g_ref[...], b_ref[...])


def _combine_ln(dest, h, route, yb, g, b, tm):
    m, d = h.shape
    return pl.pallas_call(
        functools.partial(_combine_ln_body, tm=tm, n_tok=m),
        grid_spec=pltpu.PrefetchScalarGridSpec(
            num_scalar_prefetch=1,
            grid=(m // tm,),
            in_specs=[pl.BlockSpec((tm, d), lambda i, dest: (i, 0)),
                      pl.BlockSpec((tm, LANES), lambda i, dest: (i, 0)),
                      pl.BlockSpec((1, d), lambda i, dest: (0, 0)),
                      pl.BlockSpec((1, d), lambda i, dest: (0, 0)),
                      pl.BlockSpec(memory_space=pl.ANY)],
            out_specs=pl.BlockSpec((tm, d), lambda i, dest: (i, 0)),
            scratch_shapes=[pltpu.VMEM((2, TOP_K, tm * ROW_TILE, LANES), F32),
                            pltpu.SemaphoreType.DMA((2,))]),
        out_shape=jax.ShapeDtypeStruct((m, d), F32),
        compiler_params=_params(("arbitrary",)),
        name="moe_combine_ln",
    )(dest, h, route, g.reshape(1, -1), b.reshape(1, -1), yb)


def _moe_layer(h, route, cnt, w13, w2, g, b):
    n_tok, _ = h.shape
    counts = cnt[0, :N_EXPERTS].astype(jnp.int32)
    padded = (counts + MOE_ROWS - 1) // MOE_ROWS * MOE_ROWS
    pstart = jnp.cumsum(padded) - padded
    experts = route[:, 0:2].astype(jnp.int32)
    ranks = route[:, 2:4].astype(jnp.int32)
    dest = (pstart[experts] + ranks).T.reshape(-1)
    n_rows = n_tok * TOP_K + N_EXPERTS * MOE_ROWS
    n_blocks = n_rows // MOE_ROWS
    block_end = jnp.cumsum(padded // MOE_ROWS)
    block_e = jnp.sum(jnp.arange(n_blocks)[:, None] >= block_end[None, :], axis=1)
    block_e = jnp.minimum(block_e, N_EXPERTS - 1).astype(jnp.int32)
    n_valid = block_end[-1:].astype(jnp.int32)
    meta = jnp.concatenate([pstart, counts, padded]).astype(jnp.int32)
    buf = _dispatch(dest, meta, h, n_rows, tm=512)
    yb = _moe(block_e, n_valid, buf, w13, w2)
    return _combine_ln(dest, h, route, yb, g, b, tm=512)


def kernel(x, mem, a_w_in, a_conv_w, a_conv_b, a_w_rgate, a_b_rgate, a_w_igate, a_b_igate, a_lambda,
           a_w_out, w_kv_shared, b_w_q, b_lambda, b_subln_g, b_w_out, mem_w_kv, ffn_w13, ffn_w2,
           moe_router, moe_w13, moe_w2, ln_g, ln_b):
    bsz, t, d = x.shape
    n_tok = bsz * t
    n_mem = mem.shape[1]
    bf = lambda a: a.astype(BF16)

    kv_mem = _proj(mem.reshape(bsz * n_mem, d), bf(jnp.concatenate([mem_w_kv[0], mem_w_kv[1]], axis=1)),
                   [(0, 2 * MEM_W), (2 * MEM_W, 4 * MEM_W)], [None, None], [BF16, BF16],
                   tm=min(1024, bsz * n_mem), name="mem_kv")
    kv_mem = [a.reshape(bsz, n_mem, 2 * MEM_W) for a in kv_mem]

    x2 = x.reshape(n_tok, d)
    gate, u_pre, q_mem = _proj(
        x2, bf(a_w_in[0]), [(0, D_RNN), (D_RNN, 2 * D_RNN), (2 * D_RNN, 2 * D_RNN + MEM_W)],
        [jax.nn.gelu, None, None], [BF16, BF16, BF16], tm=1024, name="proj_in")
    w_ri = bf(jnp.concatenate([a_w_rgate[0], a_w_igate[0]], axis=-1))
    rnn = _rglru(u_pre.reshape(bsz, t, D_RNN), gate.reshape(bsz, t, D_RNN), a_conv_w[0], a_conv_b[0],
                 w_ri, a_b_rgate[0], a_b_igate[0], a_lambda[0], tt=512)
    h = _outproj_ln(rnn, q_mem.reshape(bsz, t, MEM_W), kv_mem[0], bf(a_w_out[0]), x,
                    ln_g[0, 0], ln_b[0, 0], tm=1024, name="outproj_ln_a")
    h = _ffn_ln(h.reshape(n_tok, d), bf(ffn_w13[0]), bf(ffn_w2[0]), ln_g[0, 1], ln_b[0, 1], tm=512)

    layer = 1
    lam_init = 0.8 - 0.6 * math.exp(-0.3 * layer)
    w_cat = bf(jnp.concatenate([w_kv_shared[:, DIFF_QK_W:], b_w_q[0]], axis=1))
    w_k_t = bf(w_kv_shared[:, :DIFF_QK_W].T)
    v_sh, k_t, q_diff, q_mem = _proj_kvq(h, w_cat, w_k_t, bsz, t, tm=1024)
    attn = _diffattn(q_diff.reshape(bsz, t, DIFF_QK_W), k_t, v_sh.reshape(bsz, t, DIFF_V_W),
                     b_lambda[0], b_subln_g[0], lam_init, tq=256)
    h, route, cnt = _outproj_ln(attn, q_mem.reshape(bsz, t, MEM_W), kv_mem[1], bf(b_w_out[0]),
                                h.reshape(bsz, t, d), ln_g[1, 0], ln_b[1, 0], tm=1024, name="outproj_ln_b",
                                w_router=moe_router[0])
    out = _moe_layer(h.reshape(n_tok, d), route.reshape(n_tok, LANES), cnt, bf(moe_w13[0]), bf(moe_w2[0]),
                     ln_g[1, 1], ln_b[1, 1])
    return out.reshape(bsz, t, d)
```
